```python
import math
import jax, jax.numpy as jnp
from jax import lax
import numpy as np

D_MODEL = 2048
BATCH = 2
SEQ = 4096
DEPTH = 2

GRID_W = 64
CTX_LEN = 256
N_GROUPS = 4
GROUP_W = D_MODEL // N_GROUPS
MIX_W = N_GROUPS * GROUP_W
NORM_EPS = 1e-6
RW_HEAD = 64
RW_HEADS = GROUP_W // RW_HEAD
RW_DECAY_RANK = 64
RW_ICLR_RANK = 64
RW_GATE_RANK = 128
RW_DECAY_SCALE = math.exp(-0.5)
RW_GN_EPS = 64e-5
RW_COLS = 3 * GROUP_W + RW_DECAY_RANK + RW_ICLR_RANK + RW_GATE_RANK
ML_HEAD = 128
ML_HEADS = GROUP_W // ML_HEAD
ML_CHUNK = 64
ML_COLS = 4 * GROUP_W + 4 * ML_HEADS
GD_HEAD = 128
GD_HEADS = GROUP_W // GD_HEAD
GD_CHUNK = 64
GD_CONV = 3
GD_COLS = 4 * GROUP_W + 4 * GD_HEADS
AT_HEAD = 128
AT_Q_HEADS = GROUP_W // AT_HEAD
AT_KV_HEADS = 2
AT_BLOCK = 128
ROPE_THETA = 10000.0
AT_COLS = (AT_Q_HEADS + 2 * AT_KV_HEADS) * AT_HEAD
N_IN = RW_COLS + ML_COLS + GD_COLS + AT_COLS
N_EXPERTS = 16
EC_FACTOR = 2
EXPERT_FF = D_MODEL

kernel_name = 'hybrid_parallel_heads_ec_moe_dit'


def split_sizes(a, sizes):
    cuts = [int(s) for s in np.cumsum(sizes)[:-1]]
    return jnp.split(a, cuts, axis=-1)


def rms_norm(x, g, eps=NORM_EPS):
    xf = x.astype(jnp.float32)
    y = xf * lax.rsqrt(jnp.mean(xf * xf, axis=-1, keepdims=True) + eps)
    return (y * g.astype(jnp.float32)).astype(x.dtype)


def l2_normalize(x, eps=1e-6):
    xf = x.astype(jnp.float32)
    return (xf * lax.rsqrt(jnp.sum(xf * xf, axis=-1, keepdims=True) + eps)).astype(x.dtype)


def to_heads(x, n):
    b, l, w = x.shape
    return x.reshape(b, l, n, w // n).transpose(0, 2, 1, 3)


def from_heads(x):
    b, n, l, d = x.shape
    return x.transpose(0, 2, 1, 3).reshape(b, l, n * d)


def shift_mix(x, mu):
    prev = jnp.pad(x[:, :-1], ((0, 0), (1, 0), (0, 0)))
    nxt = jnp.pad(x[:, 1:], ((0, 0), (0, 1), (0, 0)))
    return x + mu[0] * (prev - x) + mu[1] * (nxt - x)


def centred_conv(x, w):
    k = w.shape[0]
    pad = k // 2
    n = x.shape[1]
    xp = jnp.pad(x, ((0, 0), (pad, pad), (0, 0)))
    return sum(xp[:, j:j + n] * w[j] for j in range(k))


def rope_half(x, ang):
    h = x.shape[-1] // 2
    x1, x2 = x[..., :h], x[..., h:]
    cs, sn = jnp.cos(ang), jnp.sin(ang)
    return jnp.concatenate([x1 * cs - x2 * sn, x2 * cs + x1 * sn], axis=-1).astype(x.dtype)


def rope_2d(x, ang_r, ang_c):
    h = x.shape[-1] // 2
    return jnp.concatenate([rope_half(x[..., :h], ang_r), rope_half(x[..., h:], ang_c)], axis=-1)


def prefix_then_latent(run, ctx_args, lat_args, init):
    y_c, state = run(ctx_args, init)
    y_l, _ = run(lat_args, state)
    return y_c, y_l


def bidirectional(run, ctx_fwd, lat_fwd, ctx_bwd, lat_bwd, init):
    flip = lambda args: tuple(jnp.flip(a, axis=2) for a in args)
    yc_f, yl_f = prefix_then_latent(run, ctx_fwd, lat_fwd, init)
    yc_b, yl_b = prefix_then_latent(run, flip(ctx_bwd), flip(lat_bwd), init)
    return yc_f + jnp.flip(yc_b, axis=2), yl_f + jnp.flip(yl_b, axis=2)


def rwkv_run(args, s0):
    xs = tuple(jnp.moveaxis(a.astype(jnp.float32), 2, 0) for a in args)

    def step(s, inp):
        r, lw, kt, v, kk, a = inp
        sk = jnp.einsum('bhvk,bhk->bhv', s, kk)
        s = s * jnp.exp(lw)[:, :, None, :] - sk[..., None] * (kk * a)[:, :, None, :] + v[..., None] * kt[:, :, None, :]
        return s, jnp.einsum('bhvk,bhk->bhv', s, r)

    s, ys = lax.scan(step, s0, xs)
    return jnp.moveaxis(ys, 0, 2), s


def rwkv_prep(pp, p):
    pp = shift_mix(pp, p['rw_mu'])
    r, k, v, wd, ad, gd = split_sizes(pp, [GROUP_W, GROUP_W, GROUP_W, RW_DECAY_RANK, RW_ICLR_RANK, RW_GATE_RANK])
    g = jax.nn.sigmoid(gd) @ p['rw_g_up']
    kk = from_heads(l2_normalize(to_heads(k * p['rw_k_k'], RW_HEADS)))
    per_dir = []
    for d in range(2):
        logw = -RW_DECAY_SCALE * jax.nn.sigmoid(p['rw_w0'][d] + jnp.tanh(wd) @ p['rw_w_up'][d])
        a = jax.nn.sigmoid(p['rw_a0'][d] + ad @ p['rw_a_up'][d])
        kt = k * (1.0 + (a - 1.0) * p['rw_k_a'])
        per_dir.append((logw, a, kt))
    return r, v, kk, g, per_dir


def rwkv_group(pc, pl, p):
    prep_c, prep_l = rwkv_prep(pc, p), rwkv_prep(pl, p)

    def args(prep, d):
        r, v, kk, _, per_dir = prep
        logw, a, kt = per_dir[d]
        return tuple(to_heads(t, RW_HEADS) for t in (r, logw, kt, v, kk, a))

    s0 = jnp.zeros((pl.shape[0], RW_HEADS, RW_HEAD, RW_HEAD), jnp.float32)
    yc, yl = bidirectional(rwkv_run, args(prep_c, 0), args(prep_l, 0), args(prep_c, 1), args(prep_l, 1), s0)

    def post(y, prep):
        r, v, _, g, per_dir = prep
        y = y.transpose(0, 2, 1, 3)
        mean = jnp.mean(y, axis=-1, keepdims=True)
        var = jnp.mean(jnp.square(y - mean), axis=-1, keepdims=True)
        y = ((y - mean) * lax.rsqrt(var + RW_GN_EPS)).reshape(y.shape[0], y.shape[1], GROUP_W)
        y = y * p['rw_ln_w'] + p['rw_ln_b']
        bh = lambda t: t.reshape(t.shape[0], t.shape[1], RW_HEADS, RW_HEAD)
        bonus = sum(jnp.sum(bh(r) * bh(kt) * p['rw_r_k'], axis=-1, keepdims=True) * bh(v) for _, _, kt in per_dir)
        return ((y + bonus.reshape(y.shape)) * g).astype(pl.dtype)

    return post(yc, prep_c), post(yl, prep_l)


def mlstm_run(args, state):
    q, k, v, li, lf = (a.astype(jnp.float32) for a in args)
    b, h, n, d = q.shape
    nc = n // ML_CHUNK

    def chunks(a):
        return jnp.moveaxis(a.reshape(a.shape[:2] + (nc, ML_CHUNK) + a.shape[3:]), 2, 0)

    tri = jnp.tril(jnp.ones((ML_CHUNK, ML_CHUNK), dtype=bool))

    def step(carry, inp):
        cm, nv, m = carry
        qc, kc, vc, ic, fc = inp
        bcum = jnp.cumsum(fc, axis=-1)
        dlog = jnp.where(tri, bcum[..., :, None] - bcum[..., None, :] + ic[..., None, :], -jnp.inf)
        inter = bcum + m[..., None]
        mt = jnp.maximum(inter, jnp.max(dlog, axis=-1))
        s = jnp.einsum('bhtd,bhsd->bhts', qc, kc) * jnp.exp(dlog - mt[..., None])
        wi = jnp.exp(inter - mt)
        num = jnp.einsum('bhts,bhse->bhte', s, vc) + wi[..., None] * jnp.einsum('bhed,bhtd->bhte', cm, qc)
        den = jnp.sum(s, axis=-1) + wi * jnp.einsum('bhd,bhtd->bht', nv, qc)
        out = num / jnp.maximum(jnp.abs(den), jnp.exp(-mt))[..., None]
        m_new = mt[..., -1]
        wk = jnp.exp(bcum[..., -1:] - bcum + ic - m_new[..., None])
        dc = jnp.exp(bcum[..., -1] + m - m_new)
        cm = dc[..., None, None] * cm + jnp.einsum('bhs,bhse,bhsd->bhed', wk, vc, kc)
        nv = dc[..., None] * nv + jnp.einsum('bhs,bhsd->bhd', wk, kc)
        return (cm, nv, m_new), out

    state, outs = lax.scan(step, state, tuple(chunks(a) for a in (q, k, v, li, lf)))
    return jnp.moveaxis(outs, 0, 2).reshape(b, h, n, d), state


def mlstm_prep(pp, p):
    q, k, v, o, gts = split_sizes(pp, [GROUP_W, GROUP_W, GROUP_W, GROUP_W, 4 * ML_HEADS])
    qh, kh, vh = to_heads(q, ML_HEADS), to_heads(k, ML_HEADS) * ML_HEAD ** -0.5, to_heads(v, ML_HEADS)
    gh = jnp.swapaxes(gts, 1, 2)
    dirs = []
    for d in range(2):
        li = gh[:, d * ML_HEADS:(d + 1) * ML_HEADS] + p['ml_ib'][d][:, None]
        lf = jax.nn.log_sigmoid(gh[:, (2 + d) * ML_HEADS:(3 + d) * ML_HEADS] + p['ml_fb'][d][:, None])
        dirs.append((qh, kh, vh, li, lf))
    return dirs, o


def mlstm_group(pc, pl, p):
    (dc, oc), (dl, ol) = mlstm_prep(pc, p), mlstm_prep(pl, p)
    b = pl.shape[0]
    init = (jnp.zeros((b, ML_HEADS, ML_HEAD, ML_HEAD), jnp.float32),
            jnp.zeros((b, ML_HEADS, ML_HEAD), jnp.float32),
            jnp.zeros((b, ML_HEADS), jnp.float32))
    hc, hl = bidirectional(mlstm_run, dc[0], dl[0], dc[1], dl[1], init)

    def post(hh, o):
        y = hh.transpose(0, 2, 1, 3)
        mean = jnp.mean(y, axis=-1, keepdims=True)
        var = jnp.mean(jnp.square(y - mean), axis=-1, keepdims=True)
        y = ((y - mean) * lax.rsqrt(var + NORM_EPS)).reshape(y.shape[0], y.shape[1], GROUP_W) * p['ml_norm_g']
        return (y * jax.nn.sigmoid(o)).astype(o.dtype)

    return post(hc, oc), post(hl, ol)


def gdn_run(args, state):
    q, k, v, lg, beta = (a.astype(jnp.float32) for a in args)
    b, h, n, _ = q.shape
    dv = v.shape[-1]
    nc, c = n // GD_CHUNK, GD_CHUNK
    blk = lambda a: a.reshape(a.shape[:2] + (nc, c) + a.shape[3:])
    q, k, v, lg, beta = blk(q), blk(k), blk(v), blk(lg), blk(beta)
    gc = jnp.cumsum(lg, axis=-1)
    incl = jnp.tril(jnp.ones((c, c), dtype=bool))
    strict = jnp.tril(jnp.ones((c, c), dtype=bool), k=-1)
    decay = jnp.exp(jnp.where(incl, gc[..., :, None] - gc[..., None, :], -jnp.inf))
    kb = k * beta[..., None]
    a_low = jnp.where(strict, jnp.einsum('bhnid,bhnjd->bhnij', kb, k) * decay, 0.0)
    rhs = jnp.concatenate([v * beta[..., None], kb * jnp.exp(gc)[..., None]], axis=-1)
    sol = lax.linalg.triangular_solve(a_low, rhs, left_side=True, lower=True, unit_diagonal=True)
    u, w = sol[..., :dv], sol[..., dv:]
    attn = jnp.einsum('bhnid,bhnjd->bhnij', q, k) * decay
    qg = q * jnp.exp(gc)[..., None]
    kd = k * jnp.exp(gc[..., -1:] - gc)[..., None]
    gl = jnp.exp(gc[..., -1])

    def step(s, inp):
        uc, wc, qc, kc, ac, glc = inp
        v_new = uc - jnp.einsum('bhld,bhde->bhle', wc, s)
        out = jnp.einsum('bhld,bhde->bhle', qc, s) + jnp.einsum('bhij,bhje->bhie', ac, v_new)
        s = s * glc[..., None, None] + jnp.einsum('bhld,bhle->bhde', kc, v_new)
        return s, out

    xs = tuple(jnp.moveaxis(a, 2, 0) for a in (u, w, qg, kd, attn, gl))
    state, outs = lax.scan(step, state, xs)
    return jnp.moveaxis(outs, 0, 2).reshape(b, h, n, dv), state


def gdn_prep(pp, p):
    qkv, g, gts = split_sizes(pp, [3 * GROUP_W, GROUP_W, 4 * GD_HEADS])
    qkv = jax.nn.silu(centred_conv(qkv, p['gd_conv']))
    q, k, v = split_sizes(qkv, [GROUP_W, GROUP_W, GROUP_W])
    qh = l2_normalize(to_heads(q, GD_HEADS)) * GD_HEAD ** -0.5
    kh = l2_normalize(to_heads(k, GD_HEADS))
    vh = to_heads(v, GD_HEADS)
    gh = jnp.swapaxes(gts, 1, 2)
    dirs = []
    for d in range(2):
        lg = -jnp.exp(p['gd_a_log'][d])[:, None] * jax.nn.softplus(
            gh[:, d * GD_HEADS:(d + 1) * GD_HEADS] + p['gd_dt_bias'][d][:, None])
        beta = jax.nn.sigmoid(gh[:, (2 + d) * GD_HEADS:(3 + d) * GD_HEADS])
        dirs.append((qh, kh, vh, lg, beta))
    return dirs, g


def gdn_group(pc, pl, p):
    (dc, gc_), (dl, gl_) = gdn_prep(pc, p), gdn_prep(pl, p)
    s0 = jnp.zeros((pl.shape[0], GD_HEADS, GD_HEAD, GD_HEAD), jnp.float32)
    oc, ol = bidirectional(gdn_run, dc[0], dl[0], dc[1], dl[1], s0)

    def post(oh, g):
        gh = g.reshape(g.shape[0], g.shape[1], GD_HEADS, GD_HEAD)
        y = rms_norm(oh.transpose(0, 2, 1, 3), p['gd_norm_g']) * jax.nn.silu(gh)
        return y.reshape(g.shape).astype(g.dtype)

    return post(oc, gc_), post(ol, gl_)


def gqa(q, k, v):
    b, hq, lq, d = q.shape
    hkv = k.shape[1]
    qg = q.reshape(b, hkv, hq // hkv, lq, d)
    s = jnp.einsum('bkgqd,bksd->bkgqs', qg, k).astype(jnp.float32) * d ** -0.5
    pr = jax.nn.softmax(s, axis=-1).astype(v.dtype)
    return jnp.einsum('bkgqs,bksd->bkgqd', pr, v).reshape(b, hq, lq, d)


def blocked_gqa(q, k, v):
    b, hq, n, d = q.shape
    nb = n // AT_BLOCK
    qb = jnp.moveaxis(q.reshape(b, hq, nb, AT_BLOCK, d), 2, 0)
    ob = lax.map(lambda blk: gqa(blk, k, v), qb)
    return jnp.moveaxis(ob, 0, 2).reshape(b, hq, n, d)


def attn_group(pc, pl, p, ang_r, ang_c, need_ctx):
    sizes = [AT_Q_HEADS * AT_HEAD, AT_KV_HEADS * AT_HEAD, AT_KV_HEADS * AT_HEAD]
    qc, kc, vc = split_sizes(pc, sizes)
    ql, kl, vl = split_sizes(pl, sizes)
    kc = rms_norm(to_heads(kc, AT_KV_HEADS), p['at_k_norm'])
    vc = to_heads(vc, AT_KV_HEADS)
    ql = rope_2d(rms_norm(to_heads(ql, AT_Q_HEADS), p['at_q_norm']), ang_r, ang_c)
    kl = rope_2d(rms_norm(to_heads(kl, AT_KV_HEADS), p['at_k_norm']), ang_r, ang_c)
    vl = to_heads(vl, AT_KV_HEADS)
    k_all = jnp.concatenate([kc, kl], axis=2)
    v_all = jnp.concatenate([vc, vl], axis=2)
    out_l = from_heads(blocked_gqa(ql, k_all, v_all))
    if not need_ctx:
        return None, out_l
    qc = rms_norm(to_heads(qc, AT_Q_HEADS), p['at_q_norm'])
    return from_heads(gqa(qc, kc, vc)), out_l


def token_mixer(hc, hl, p, ang_r, ang_c, need_ctx):
    pc = hc @ p['w_in']
    pl = hl @ p['w_in']
    sizes = [RW_COLS, ML_COLS, GD_COLS, AT_COLS]
    ac, bc, cc, dc = split_sizes(pc, sizes)
    al, bl, cl, dl = split_sizes(pl, sizes)
    oa = rwkv_group(ac, al, p)
    ob = mlstm_group(bc, bl, p)
    oc = gdn_group(cc, cl, p)
    od = attn_group(dc, dl, p, ang_r, ang_c, need_ctx)
    yl = jnp.concatenate([oa[1], ob[1], oc[1], od[1]], axis=-1) @ p['w_out']
    if not need_ctx:
        return None, yl
    yc = jnp.concatenate([oa[0], ob[0], oc[0], od[0]], axis=-1) @ p['w_out']
    return yc, yl


def expert_choice_ffn(h, w_router, w1, w3, w2):
    b, l, _ = h.shape
    cap = EC_FACTOR * l // N_EXPERTS
    aff = jax.nn.softmax(jnp.einsum('bld,de->ble', h, w_router).astype(jnp.float32), axis=-1)
    gate, idx = lax.top_k(jnp.swapaxes(aff, 1, 2), cap)
    bidx = jnp.arange(b)[:, None, None]
    xs = h[bidx, idx]
    hid = jax.nn.silu(jnp.einsum('becd,edf->becf', xs, w1)) * jnp.einsum('becd,edf->becf', xs, w3)
    y = jnp.einsum('becf,efd->becd', hid, w2) * gate[..., None].astype(h.dtype)
    return jnp.zeros_like(h).at[bidx, idx].add(y)


def modulation(cond, p):
    return jnp.split(jax.nn.silu(cond) @ p['ada_w'] + p['ada_b'], 6, axis=-1)


def modulate(x, g, shift, scale):
    return rms_norm(x, g) * (1.0 + scale) + shift


def trunk_layer(xl, xc, c, c_ctx, p, ang_r, ang_c, last):
    ml = [t[:, None, :] for t in modulation(c, p)]
    mc = modulation(c_ctx, p)
    hl = modulate(xl, p['norm1_g'], ml[0], ml[1])
    hc = modulate(xc, p['norm1_g'], mc[0], mc[1])
    yc, yl = token_mixer(hc, hl, p, ang_r, ang_c, not last)
    xl = xl + ml[2] * yl
    xl = xl + ml[5] * expert_choice_ffn(modulate(xl, p['norm2_g'], ml[3], ml[4]),
                                        p['w_router'], p['w_exp1'], p['w_exp3'], p['w_exp2'])
    if not last:
        xc = xc + mc[2] * yc
        xc = xc + mc[5] * expert_choice_ffn(modulate(xc, p['norm2_g'], mc[3], mc[4]),
                                            p['w_router'], p['w_exp1'], p['w_exp3'], p['w_exp2'])
    return xl, xc


def setup_inputs(seed: int = 0) -> dict:
    key = jax.random.key(seed)

    def nrm(i, shape, scale):
        return jax.random.normal(jax.random.fold_in(key, i), shape, jnp.float32) * scale

    def uni(i, shape, lo, hi):
        return jax.random.uniform(jax.random.fold_in(key, i), shape, jnp.float32, lo, hi)

    d = D_MODEL
    dt = jnp.exp(uni(26, (DEPTH, 2, GD_HEADS), math.log(1e-3), math.log(1e-1)))
    return {
        'x': nrm(0, (BATCH, SEQ, d), 1.0),
        'c': nrm(1, (BATCH, d), 1.0),
        'ctx': nrm(2, (BATCH, CTX_LEN, d), 1.0),
        'c_ctx': nrm(3, (d,), 1.0),
        'ada_w': nrm(4, (DEPTH, d, 6 * d), 0.5 * d ** -0.5),
        'ada_b': nrm(5, (DEPTH, 6 * d), 0.01),
        'norm1_g': 1.0 + nrm(6, (DEPTH, d), 0.02),
        'norm2_g': 1.0 + nrm(7, (DEPTH, d), 0.02),
        'w_in': nrm(8, (DEPTH, d, N_IN), d ** -0.5),
        'w_out': nrm(9, (DEPTH, MIX_W, d), MIX_W ** -0.5),
        'rw_mu': uni(10, (DEPTH, 2, RW_COLS), 0.0, 0.5),
        'rw_w0': nrm(11, (DEPTH, 2, GROUP_W), 0.5),
        'rw_w_up': nrm(12, (DEPTH, 2, RW_DECAY_RANK, GROUP_W), 0.1),
        'rw_a0': nrm(13, (DEPTH, 2, GROUP_W), 0.5),
        'rw_a_up': nrm(14, (DEPTH, 2, RW_ICLR_RANK, GROUP_W), 0.1),
        'rw_g_up': nrm(15, (DEPTH, RW_GATE_RANK, GROUP_W), RW_GATE_RANK ** -0.5),
        'rw_k_k': 0.85 + nrm(16, (DEPTH, GROUP_W), 0.05),
        'rw_k_a': 1.0 + nrm(17, (DEPTH, GROUP_W), 0.05),
        'rw_r_k': nrm(18, (DEPTH, RW_HEADS, RW_HEAD), 0.1),
        'rw_ln_w': 1.0 + nrm(19, (DEPTH, GROUP_W), 0.02),
        'rw_ln_b': nrm(20, (DEPTH, GROUP_W), 0.01),
        'ml_ib': nrm(21, (DEPTH, 2, ML_HEADS), 0.1),
        'ml_fb': jnp.linspace(3.0, 6.0, ML_HEADS, dtype=jnp.float32) + nrm(22, (DEPTH, 2, ML_HEADS), 0.1),
        'ml_norm_g': 1.0 + nrm(23, (DEPTH, GROUP_W), 0.02),
        'gd_conv': nrm(24, (DEPTH, GD_CONV, 3 * GROUP_W), GD_CONV ** -0.5),
        'gd_a_log': jnp.log(uni(25, (DEPTH, 2, GD_HEADS), 1.0, 16.0)),
        'gd_dt_bias': dt + jnp.log(-jnp.expm1(-dt)),
        'gd_norm_g': 1.0 + nrm(27, (DEPTH, GD_HEAD), 0.02),
        'at_q_norm': 1.0 + nrm(28, (DEPTH, AT_HEAD), 0.02),
        'at_k_norm': 1.0 + nrm(29, (DEPTH, AT_HEAD), 0.02),
        'w_router': nrm(30, (DEPTH, d, N_EXPERTS), d ** -0.5),
        'w_exp1': nrm(31, (DEPTH, N_EXPERTS, d, EXPERT_FF), d ** -0.5),
        'w_exp3': nrm(32, (DEPTH, N_EXPERTS, d, EXPERT_FF), d ** -0.5),
        'w_exp2': nrm(33, (DEPTH, N_EXPERTS, EXPERT_FF, d), EXPERT_FF ** -0.5),
        'final_g': 1.0 + nrm(34, (d,), 0.02),
    }


def reference(x, c, ctx, c_ctx, ada_w, ada_b, norm1_g, norm2_g, w_in, w_out,
              rw_mu, rw_w0, rw_w_up, rw_a0, rw_a_up, rw_g_up, rw_k_k, rw_k_a, rw_r_k, rw_ln_w, rw_ln_b,
              ml_ib, ml_fb, ml_norm_g, gd_conv, gd_a_log, gd_dt_bias, gd_norm_g,
              at_q_norm, at_k_norm, w_router, w_exp1, w_exp3, w_exp2, final_g):
    n_lat = x.shape[1]
    ROWS = n_lat // GRID_W
    row = jnp.broadcast_to(jnp.arange(ROWS)[:, None], (ROWS, GRID_W)).reshape(-1).astype(jnp.float32)
    col = jnp.broadcast_to(jnp.arange(GRID_W)[None, :], (ROWS, GRID_W)).reshape(-1).astype(jnp.float32)
    axis_dim = AT_HEAD // 2
    inv_freq = ROPE_THETA ** (-jnp.arange(0, axis_dim, 2, dtype=jnp.float32) / axis_dim)
    ang_r = row[:, None] * inv_freq[None, :]
    ang_c = col[:, None] * inv_freq[None, :]
    xl, xc = x, ctx
    for layer in range(DEPTH):
        p = {
            'ada_w': ada_w[layer], 'ada_b': ada_b[layer],
            'norm1_g': norm1_g[layer], 'norm2_g': norm2_g[layer],
            'w_in': w_in[layer], 'w_out': w_out[layer],
            'rw_mu': rw_mu[layer], 'rw_w0': rw_w0[layer], 'rw_w_up': rw_w_up[layer],
            'rw_a0': rw_a0[layer], 'rw_a_up': rw_a_up[layer], 'rw_g_up': rw_g_up[layer],
            'rw_k_k': rw_k_k[layer], 'rw_k_a': rw_k_a[layer], 'rw_r_k': rw_r_k[layer],
            'rw_ln_w': rw_ln_w[layer], 'rw_ln_b': rw_ln_b[layer],
            'ml_ib': ml_ib[layer], 'ml_fb': ml_fb[layer], 'ml_norm_g': ml_norm_g[layer],
            'gd_conv': gd_conv[layer], 'gd_a_log': gd_a_log[layer], 'gd_dt_bias': gd_dt_bias[layer],
            'gd_norm_g': gd_norm_g[layer],
            'at_q_norm': at_q_norm[layer], 'at_k_norm': at_k_norm[layer],
            'w_router': w_router[layer], 'w_exp1': w_exp1[layer], 'w_exp3': w_exp3[layer],
            'w_exp2': w_exp2[layer],
        }
        xl, xc = trunk_layer(xl, xc, c, c_ctx, p, ang_r, ang_c, layer == DEPTH - 1)
    return rms_norm(xl, final_g)
```

```python
import functools
import math

import jax
import jax.numpy as jnp
from jax import lax
from jax.experimental import pallas as pl
from jax.experimental.pallas import tpu as pltpu

F32 = jnp.float32
BF16 = jnp.bfloat16

NORM_EPS = 1e-6
GRID_W = 64
N_GROUPS = 4
RW_HEAD = 64
RW_DECAY_RANK = 64
RW_ICLR_RANK = 64
RW_GATE_RANK = 128
RW_DECAY_SCALE = math.exp(-0.5)
RW_GN_EPS = 64e-5
RW_CHUNK = 64
ML_HEAD = 128
ML_CHUNK = 256
GD_HEAD = 128
GD_CHUNK = 64
AT_HEAD = 128
AT_KV_HEADS = 2
ROPE_THETA = 10000.0
AT_TQ = 256
N_EXPERTS = 16
EC_FACTOR = 2

VMEM_LIMIT = 56 * 1024 * 1024
LANES = 128


def _cparams(sem):
    return pltpu.CompilerParams(dimension_semantics=sem, vmem_limit_bytes=VMEM_LIMIT)


def _pick_tile(n, limit, mult=16):
    best = None
    for cand in range(mult, min(n, limit) + 1, mult):
        if n % cand == 0:
            best = cand
    assert best is not None, (n, limit, mult)
    return best


_DIMS = {
    "nn": (((1,), (0,)), ((), ())),
    "nt": (((1,), (1,)), ((), ())),
    "tn": (((0,), (0,)), ((), ())),
}


def _split(a, n):
    if a.dtype == BF16:
        return [a]
    pieces, rest = [], a
    for i in range(n):
        p = rest.astype(BF16)
        pieces.append(p)
        if i + 1 < n:
            rest = rest - p.astype(F32)
    return pieces


def _mm(a, b, dims="nn", pa=1, pb=1):
    ap, bp = _split(a, pa), _split(b, pb)
    top = max(len(ap), len(bp))
    acc = None
    for i, x in enumerate(ap):
        for j, y in enumerate(bp):
            if i + j < top:
                t = lax.dot_general(x, y, _DIMS[dims], preferred_element_type=F32)
                acc = t if acc is None else acc + t
    return acc


def _tri(n, rev, strict):
    t = lax.broadcasted_iota(jnp.int32, (n, n), 0)
    s = lax.broadcasted_iota(jnp.int32, (n, n), 1)
    if rev:
        return (s > t) if strict else (s >= t)
    return (s < t) if strict else (s <= t)


def _chunk_maps(nsteps, nctx):
    fwd = lambda i: i
    bwd = lambda i: jnp.where(i < nctx, nctx - 1 - i, nsteps - 1 - i + nctx)
    return fwd, bwd


def _unit_lower_inverse(a, levels, passes):
    n = a.shape[0]
    eye = (lax.broadcasted_iota(jnp.int32, (n, n), 0) == lax.broadcasted_iota(jnp.int32, (n, n), 1)).astype(F32)
    y = -a
    r = eye + y
    for _ in range(levels):
        y = _mm(y, y, "nn", passes, passes)
        r = r + _mm(r, y, "nn", passes, passes)
    return r


def _mlstm_kernel(qf, kf, vf, gf, gtf, qb, kb, vb, gb, gtb, yf, yb, c_scr, m_scr, *, nb, nh, chunk):
    step = pl.program_id(0)

    @pl.when(step == 0)
    def _():
        c_scr[...] = jnp.zeros_like(c_scr)
        m_scr[...] = jnp.zeros_like(m_scr)

    scale = ML_HEAD ** -0.5
    lane = lax.broadcasted_iota(jnp.int32, (chunk, ML_HEAD), 1)
    ones_col = (lane == 0).astype(F32)
    for d, (q_ref, k_ref, v_ref, g_ref, gt_ref, y_ref) in enumerate(
            ((qf, kf, vf, gf, gtf, yf), (qb, kb, vb, gb, gtb, yb))):
        rev = d == 1
        incl = _tri(chunk, rev, False)
        cum_m = incl.astype(BF16)
        last = 0 if rev else chunk - 1
        for b in range(nb):
            g = g_ref[b]
            gt = gt_ref[b]
            cum_c = _mm(cum_m, g, "nn", 1, 3)
            cum_r = _mm(gt, cum_m, "nt", 3, 1)
            for h in range(nh):
                ci, cf = d * nh + h, (2 + d) * nh + h
                sl = slice(h * ML_HEAD, (h + 1) * ML_HEAD)
                q = q_ref[b, :, sl]
                k = k_ref[b, :, sl] * scale
                v = v_ref[b, :, sl]
                idx = (d * nb + b) * nh + h
                m_old = m_scr[idx][0:1, 0:1]
                bc = cum_c[:, cf:cf + 1]
                br = cum_r[cf:cf + 1, :]
                li_r = gt[ci:ci + 1, :]
                li_c = g[:, ci:ci + 1]
                dlog = jnp.where(incl, bc - br + li_r, -jnp.inf)
                inter = bc + m_old
                mt = jnp.maximum(inter, jnp.max(dlog, axis=1, keepdims=True))
                s = _mm(q, k, "nt") * jnp.exp(dlog - mt)
                wi = jnp.exp(inter - mt)
                v_aug = jnp.concatenate([v, ones_col], axis=1)
                c_old = c_scr[idx]
                acc = _mm(s, v_aug) + wi * _mm(q, c_old)
                num = acc[:, :ML_HEAD]
                den = acc[:, ML_HEAD:ML_HEAD + 1]
                y_ref[b, :, sl] = num / jnp.maximum(jnp.abs(den), jnp.exp(-mt))
                m_new = mt[last:last + 1, :]
                b_last = bc[last:last + 1, :]
                wk = jnp.exp(b_last - bc + li_c - m_new)
                dc = jnp.exp(b_last + m_old - m_new)
                c_scr[idx] = dc * c_old + _mm(k * wk, v_aug, "tn")
                m_scr[idx] = jnp.broadcast_to(m_new, (8, LANES))


def _mlstm_scan(q_src, k_src, v_src, cols, g, gt, ctx_len):
    nb, t, _ = g.shape
    nh = g.shape[-1] // 4
    width = nh * ML_HEAD
    chunk = ML_CHUNK
    nsteps, nctx = t // chunk, ctx_len // chunk
    fwd, bwd = _chunk_maps(nsteps, nctx)

    def specs(cm):
        col = lambda c: pl.BlockSpec((nb, chunk, width), lambda i, c=c: (0, cm(i), c))
        return [col(cols[0]), col(cols[1]), col(cols[2]),
                pl.BlockSpec((nb, chunk, 4 * nh), lambda i: (0, cm(i), 0)),
                pl.BlockSpec((nb, 4 * nh, chunk), lambda i: (0, 0, cm(i)))]

    out_spec = lambda cm: pl.BlockSpec((nb, chunk, width), lambda i: (0, cm(i), 0))
    kern = functools.partial(_mlstm_kernel, nb=nb, nh=nh, chunk=chunk)
    return pl.pallas_call(
        kern,
        grid=(nsteps,),
        in_specs=specs(fwd) + specs(bwd),
        out_specs=[out_spec(fwd), out_spec(bwd)],
        out_shape=[jax.ShapeDtypeStruct((nb, t, width), F32)] * 2,
        scratch_shapes=[pltpu.VMEM((2 * nb * nh, ML_HEAD, 2 * ML_HEAD), F32),
                        pltpu.VMEM((2 * nb * nh, 8, LANES), F32)],
        compiler_params=_cparams(("arbitrary",)),
        name="mlstm_scan",
    )(q_src, k_src, v_src, g, gt, q_src, k_src, v_src, g, gt)


def _head_layer_norm(y, nh, eps):
    b, t, w = y.shape
    yh = y.reshape(b, t, nh, w // nh)
    mean = jnp.mean(yh, axis=-1, keepdims=True)
    var = jnp.mean(jnp.square(yh - mean), axis=-1, keepdims=True)
    return ((yh - mean) * lax.rsqrt(var + eps)).reshape(b, t, w)


def _mlstm_mixer(proj, col0, graw, ml_ib, ml_fb, ml_norm_g, ctx_len):
    nh = graw.shape[-1] // 4
    width = nh * ML_HEAD
    li = graw[..., :2 * nh] + ml_ib.reshape(-1)
    lf = jax.nn.log_sigmoid(graw[..., 2 * nh:] + ml_fb.reshape(-1))
    g = jnp.concatenate([li, lf], axis=-1)
    yf, yb = _mlstm_scan(proj, proj, proj, (col0, col0 + 1, col0 + 2), g, jnp.swapaxes(g, 1, 2), ctx_len)
    y = _head_layer_norm(yf + yb, nh, NORM_EPS) * ml_norm_g
    o = proj[..., (col0 + 3) * width:(col0 + 4) * width]
    return y * jax.nn.sigmoid(o)


GD_INV_PASSES = 2


def _gdn_kernel(qf, kf, vf, gf, gtf, qb, kb, vb, gb, gtb, yf, yb, s_scr, *, nb, nh, chunk):
    step = pl.program_id(0)

    @pl.when(step == 0)
    def _():
        s_scr[...] = jnp.zeros_like(s_scr)

    levels = int(math.log2(chunk)) - 1
    for d, (q_ref, k_ref, v_ref, g_ref, gt_ref, y_ref) in enumerate(
            ((qf, kf, vf, gf, gtf, yf), (qb, kb, vb, gb, gtb, yb))):
        rev = d == 1
        incl = _tri(chunk, rev, False)
        strict = _tri(chunk, rev, True)
        cum_m = incl.astype(BF16)
        last = 0 if rev else chunk - 1
        for b in range(nb):
            g = g_ref[b]
            gt = gt_ref[b, 0]
            cum_c = _mm(cum_m, g, "nn", 1, 3)
            cum_r = _mm(gt, cum_m, "nt", 3, 1)
            for h in range(nh):
                cg, cb = d * nh + h, (2 + d) * nh + h
                sl = slice(h * GD_HEAD, (h + 1) * GD_HEAD)
                q = q_ref[b, :, sl]
                k = k_ref[b, :, sl]
                v = v_ref[b, :, sl]
                idx = (d * nb + b) * nh + h
                gc_c = cum_c[:, cg:cg + 1]
                gc_r = cum_r[cg:cg + 1, :]
                beta = g[:, cb:cb + 1]
                decay = jnp.where(incl, jnp.exp(jnp.where(incl, gc_c - gc_r, 0.0)), 0.0)
                kb_ = k * beta
                a = jnp.where(strict, _mm(kb_, k, "nt", GD_INV_PASSES, GD_INV_PASSES) * decay, 0.0)
                attn = _mm(q, k, "nt") * decay
                eg = jnp.exp(gc_c)
                rhs = jnp.concatenate([v * beta, kb_ * eg], axis=1)
                inv = _unit_lower_inverse(a, levels, GD_INV_PASSES)
                sol = _mm(inv, rhs, "nn", GD_INV_PASSES, GD_INV_PASSES)
                u, w = sol[:, :GD_HEAD], sol[:, GD_HEAD:]
                g_last = gc_c[last:last + 1, :]
                kd = k * jnp.exp(g_last - gc_c)
                s_old = s_scr[idx]
                ws = _mm(jnp.concatenate([w, q * eg], axis=0), s_old)
                v_new = u - ws[:chunk]
                y_ref[b, :, sl] = ws[chunk:] + _mm(attn, v_new)
                s_scr[idx] = s_old * jnp.exp(g_last) + _mm(kd, v_new, "tn")


def _gdn_scan(q, k, v, g, ctx_len):
    nb, t, width = q.shape
    nh = width // GD_HEAD
    chunk = GD_CHUNK
    nsteps, nctx = t // chunk, ctx_len // chunk
    fwd, bwd = _chunk_maps(nsteps, nctx)
    gt = jnp.swapaxes(g.reshape(nb, nsteps, chunk, 4 * nh), 2, 3)

    def specs(cm):
        col = pl.BlockSpec((nb, chunk, width), lambda i: (0, cm(i), 0))
        return [col, col, col,
                pl.BlockSpec((nb, chunk, 4 * nh), lambda i: (0, cm(i), 0)),
                pl.BlockSpec((nb, 1, 4 * nh, chunk), lambda i: (0, cm(i), 0, 0))]

    out_spec = lambda cm: pl.BlockSpec((nb, chunk, width), lambda i: (0, cm(i), 0))
    kern = functools.partial(_gdn_kernel, nb=nb, nh=nh, chunk=chunk)
    return pl.pallas_call(
        kern,
        grid=(nsteps,),
        in_specs=specs(fwd) + specs(bwd),
        out_specs=[out_spec(fwd), out_spec(bwd)],
        out_shape=[jax.ShapeDtypeStruct((nb, t, width), F32)] * 2,
        scratch_shapes=[pltpu.VMEM((2 * nb * nh, GD_HEAD, GD_HEAD), F32)],
        compiler_params=_cparams(("arbitrary",)),
        name="gdn_scan",
    )(q, k, v, g, gt, q, k, v, g, gt)


def _seq_shift(x, ctx_len, delta):
    t = x.shape[1]
    rows = jnp.arange(t)
    if delta < 0:
        sh = jnp.pad(x[:, :delta], ((0, 0), (-delta, 0), (0, 0)))
        ok = (rows + delta >= 0) & ((rows < ctx_len) | (rows + delta >= ctx_len))
    else:
        sh = jnp.pad(x[:, delta:], ((0, 0), (0, delta), (0, 0)))
        ok = (rows + delta < t) & ((rows >= ctx_len) | (rows + delta < ctx_len))
    return jnp.where(ok[None, :, None], sh, 0.0)


def _l2n(x, nh, eps=1e-6):
    b, t, w = x.shape
    xh = x.reshape(b, t, nh, w // nh)
    return (xh * lax.rsqrt(jnp.sum(xh * xh, axis=-1, keepdims=True) + eps)).reshape(b, t, w)


def _gdn_mixer(qkv, gate, graw, gd_conv, gd_a_log, gd_dt_bias, gd_norm_g, ctx_len):
    nh = graw.shape[-1] // 4
    width = nh * GD_HEAD
    pad = gd_conv.shape[0] // 2
    conv = sum(_seq_shift(qkv, ctx_len, j - pad) * gd_conv[j] for j in range(gd_conv.shape[0]))
    qkv = jax.nn.silu(conv)
    q = _l2n(qkv[..., :width], nh) * GD_HEAD ** -0.5
    k = _l2n(qkv[..., width:2 * width], nh)
    v = qkv[..., 2 * width:]
    lg = -jnp.exp(gd_a_log).reshape(-1) * jax.nn.softplus(graw[..., :2 * nh] + gd_dt_bias.reshape(-1))
    beta = jax.nn.sigmoid(graw[..., 2 * nh:])
    yf, yb = _gdn_scan(q, k, v, jnp.concatenate([lg, beta], axis=-1), ctx_len)
    o = yf + yb
    b, t, _ = o.shape
    oh = o.reshape(b, t, nh, GD_HEAD)
    oh = oh * lax.rsqrt(jnp.mean(oh * oh, axis=-1, keepdims=True) + NORM_EPS) * gd_norm_g
    return oh.reshape(b, t, width) * jax.nn.silu(gate)


RW_PASSES = 1


def _rwkv_kernel(*refs, nb, npair, chunk):
    ins, (yf, yb, s_scr) = refs[:12], refs[12:]
    step = pl.program_id(0)

    @pl.when(step == 0)
    def _():
        s_scr[...] = jnp.zeros_like(s_scr)

    levels = int(math.log2(chunk)) - 1
    pw = 2 * RW_HEAD
    lane = lax.broadcasted_iota(jnp.int32, (chunk, pw), 1)
    head0 = lane < RW_HEAD
    same_head = (lax.broadcasted_iota(jnp.int32, (pw, pw), 0) // RW_HEAD
                 == lax.broadcasted_iota(jnp.int32, (pw, pw), 1) // RW_HEAD)
    mid = chunk // 2
    for d, y_ref in enumerate((yf, yb)):
        r_ref, lw_ref, kt_ref, v_ref, kk_ref, b_ref = ins[6 * d:6 * d + 6]
        rev = d == 1
        incl = _tri(chunk, rev, False)
        strict = _tri(chunk, rev, True)
        cum_m = incl.astype(BF16)
        last = 0 if rev else chunk - 1
        for b in range(nb):
            lw_all = lw_ref[b]
            g_all = _mm(cum_m, lw_all, "nn", 1, 3)
            for p in range(npair):
                sl = slice(p * pw, (p + 1) * pw)
                lw, g = lw_all[:, sl], g_all[:, sl]
                r, kt, v, kk, bb = (x[b, :, sl] for x in (r_ref, kt_ref, v_ref, kk_ref, b_ref))
                gh = g - g[mid:mid + 1, :]
                e_pos, e_neg = jnp.exp(gh), jnp.exp(-gh)
                alpha = jnp.exp(gh - lw) * kk
                beta, kappa, rho = bb * e_neg, kt * e_neg, r * e_pos
                g_last = g[last:last + 1, :]
                dl = jnp.exp(g_last - g)
                idx = (d * nb + b) * npair + p
                s_old = s_scr[idx]
                from_state = _mm(jnp.concatenate([jnp.exp(g - lw) * kk, jnp.exp(g) * r], axis=0), s_old, "nt")
                bk = jnp.concatenate([beta, kappa], axis=0)
                e_parts, y_parts = [], []
                for hh in range(2):
                    hm = head0 if hh == 0 else jnp.logical_not(head0)
                    lhs = jnp.concatenate([jnp.where(hm, alpha, 0.0), jnp.where(hm, rho, 0.0)], axis=0)
                    gm = _mm(lhs, bk, "nt", RW_PASSES, RW_PASSES)
                    a_ab = jnp.where(strict, gm[:chunk, :chunk], 0.0)
                    a_ak = jnp.where(strict, gm[:chunk, chunk:], 0.0)
                    r_ab = jnp.where(incl, gm[chunk:, :chunk], 0.0)
                    r_ak = jnp.where(incl, gm[chunk:, chunk:], 0.0)
                    inv = _unit_lower_inverse(a_ab, levels, RW_PASSES)
                    e = _mm(inv, from_state[:chunk] + _mm(a_ak, v), "nn", RW_PASSES, RW_PASSES)
                    y = from_state[chunk:] + _mm(jnp.concatenate([-r_ab, r_ak], axis=1),
                                                 jnp.concatenate([e, v], axis=0))
                    e_parts.append(e)
                    y_parts.append(y)
                e = jnp.where(head0, e_parts[0], e_parts[1])
                y_ref[b, :, sl] = jnp.where(head0, y_parts[0], y_parts[1])
                upd = _mm(jnp.concatenate([-e, v], axis=0), jnp.concatenate([bb * dl, kt * dl], axis=0), "tn")
                s_scr[idx] = s_old * jnp.exp(g_last) + jnp.where(same_head, upd, 0.0)


def _rwkv_scan(r, v, kk, per_dir, ctx_len):
    nb, t, width = r.shape
    chunk = RW_CHUNK
    nsteps, nctx = t // chunk, ctx_len // chunk
    fwd, bwd = _chunk_maps(nsteps, nctx)
    spec = lambda cm: pl.BlockSpec((nb, chunk, width), lambda i: (0, cm(i), 0))
    args = []
    for d in range(2):
        lw, bb, kt = per_dir[d]
        args += [r, lw, kt, v, kk, bb]
    kern = functools.partial(_rwkv_kernel, nb=nb, npair=width // (2 * RW_HEAD), chunk=chunk)
    return pl.pallas_call(
        kern,
        grid=(nsteps,),
        in_specs=[spec(fwd)] * 6 + [spec(bwd)] * 6,
        out_specs=[spec(fwd), spec(bwd)],
        out_shape=[jax.ShapeDtypeStruct((nb, t, width), F32)] * 2,
        scratch_shapes=[pltpu.VMEM((2 * nb * (width // (2 * RW_HEAD)), 2 * RW_HEAD, 2 * RW_HEAD), F32)],
        compiler_params=_cparams(("arbitrary",)),
        name="rwkv_scan",
    )(*args)


def _rwkv_mixer(pa, p, ctx_len):
    width = p['rw_k_k'].shape[0]
    nh = width // RW_HEAD
    mu = p['rw_mu']
    pp = pa + mu[0] * (_seq_shift(pa, ctx_len, -1) - pa) + mu[1] * (_seq_shift(pa, ctx_len, 1) - pa)
    r, k, v = pp[..., :width], pp[..., width:2 * width], pp[..., 2 * width:3 * width]
    o = 3 * width
    wd = pp[..., o:o + RW_DECAY_RANK]
    ad = pp[..., o + RW_DECAY_RANK:o + RW_DECAY_RANK + RW_ICLR_RANK]
    gd = pp[..., o + RW_DECAY_RANK + RW_ICLR_RANK:]
    g = jax.nn.sigmoid(gd) @ p['rw_g_up']
    kk = _l2n(k * p['rw_k_k'], nh)
    per_dir, kts = [], []
    for d in range(2):
        logw = -RW_DECAY_SCALE * jax.nn.sigmoid(p['rw_w0'][d] + jnp.tanh(wd) @ p['rw_w_up'][d])
        a = jax.nn.sigmoid(p['rw_a0'][d] + ad @ p['rw_a_up'][d])
        kt = k * (1.0 + (a - 1.0) * p['rw_k_a'])
        per_dir.append((logw, kk * a, kt))
        kts.append(kt)
    yf, yb = _rwkv_scan(r, v, kk, per_dir, ctx_len)
    y = _head_layer_norm(yf + yb, nh, RW_GN_EPS) * p['rw_ln_w'] + p['rw_ln_b']
    b, t, _ = y.shape
    bh = lambda x: x.reshape(b, t, nh, RW_HEAD)
    bonus = sum(jnp.sum(bh(r) * bh(kt) * p['rw_r_k'], axis=-1, keepdims=True) * bh(v) for kt in kts)
    return (y + bonus.reshape(y.shape)) * g


def _attn_kernel(q_ref, k_ref, v_ref, o_ref, *, ctx_tiles, ctx_len, group):
    qi = pl.program_id(2)

    def run(keys, vals):
        for g in range(group):
            sl = slice(g * AT_HEAD, (g + 1) * AT_HEAD)
            s = lax.dot_general(q_ref[0, :, sl], keys, _DIMS["nt"], preferred_element_type=F32)
            p = jnp.exp(s - jnp.max(s, axis=-1, keepdims=True))
            o = jnp.dot(p.astype(BF16), vals, preferred_element_type=F32)
            o_ref[0, :, sl] = o / jnp.sum(p, axis=-1, keepdims=True)

    @pl.when(qi < ctx_tiles)
    def _():
        run(k_ref[0, :ctx_len, :], v_ref[0, :ctx_len, :])

    @pl.when(qi >= ctx_tiles)
    def _():
        run(k_ref[0], v_ref[0])


def _attention(q, k, v, ctx_len):
    nb, t, qw = q.shape
    hkv = k.shape[-1] // AT_HEAD
    group = qw // AT_HEAD // hkv
    tq = AT_TQ
    assert ctx_len % tq == 0 and t % tq == 0
    kern = functools.partial(_attn_kernel, ctx_tiles=ctx_len // tq, ctx_len=ctx_len, group=group)
    return pl.pallas_call(
        kern,
        grid=(nb, hkv, t // tq),
        in_specs=[pl.BlockSpec((1, tq, group * AT_HEAD), lambda b, h, i: (b, i, h)),
                  pl.BlockSpec((1, t, AT_HEAD), lambda b, h, i: (b, 0, h)),
                  pl.BlockSpec((1, t, AT_HEAD), lambda b, h, i: (b, 0, h))],
        out_specs=pl.BlockSpec((1, tq, group * AT_HEAD), lambda b, h, i: (b, i, h)),
        out_shape=jax.ShapeDtypeStruct((nb, t, qw), F32),
        compiler_params=_cparams(("parallel", "parallel", "arbitrary")),
        name="gqa",
    )(q, k, v)


def _rope_tables(t, ctx_len):
    n_lat = t - ctx_len
    pos = jnp.arange(n_lat)
    row = (pos // GRID_W).astype(F32)
    col = (pos % GRID_W).astype(F32)
    axis_dim = AT_HEAD // 2
    inv_freq = ROPE_THETA ** (-jnp.arange(0, axis_dim, 2, dtype=F32) / axis_dim)
    ang_r = row[:, None] * inv_freq[None, :]
    ang_c = col[:, None] * inv_freq[None, :]
    cos = jnp.concatenate([jnp.cos(ang_r)] * 2 + [jnp.cos(ang_c)] * 2, axis=-1)
    sin = jnp.concatenate([-jnp.sin(ang_r), jnp.sin(ang_r), -jnp.sin(ang_c), jnp.sin(ang_c)], axis=-1)
    cos = jnp.concatenate([jnp.ones((ctx_len, AT_HEAD), F32), cos], axis=0)
    sin = jnp.concatenate([jnp.zeros((ctx_len, AT_HEAD), F32), sin], axis=0)
    return cos, sin


def _attn_mixer(pd, at_q_norm, at_k_norm, ctx_len, rope):
    nb, t, w = pd.shape
    kvw = AT_KV_HEADS * AT_HEAD
    qw = w - 2 * kvw
    cos, sin = rope

    def prep(x, g):
        xh = x.reshape(nb, t, -1, AT_HEAD)
        xh = xh * lax.rsqrt(jnp.mean(xh * xh, axis=-1, keepdims=True) + NORM_EPS) * g
        sw = jnp.flip(xh.reshape(nb, t, -1, 2, 2, AT_HEAD // 4), axis=-2).reshape(xh.shape)
        return (xh * cos[:, None, :] + sw * sin[:, None, :]).reshape(x.shape)

    q = prep(pd[..., :qw], at_q_norm) * AT_HEAD ** -0.5
    k = prep(pd[..., qw:qw + kvw], at_k_norm)
    v = pd[..., qw + kvw:]
    return _attention(q.astype(BF16), k.astype(BF16), v.astype(BF16), ctx_len)


MOD_TN = 1024
MOD_KC = 256


def _mod_kernel(s_ref, w_ref, b_ref, o_ref, *, ncond):
    d = w_ref.shape[1]
    acc = [jnp.zeros((1, w_ref.shape[2]), F32) for _ in range(ncond)]
    for kc in range(d // MOD_KC):
        rows = slice(kc * MOD_KC, (kc + 1) * MOD_KC)
        w = w_ref[0, rows, :]
        for j in range(ncond):
            acc[j] = acc[j] + jnp.sum(w * s_ref[rows, j:j + 1], axis=0, keepdims=True)
    bias = b_ref[0]
    pad = [jnp.zeros_like(bias)] * (o_ref.shape[1] - ncond)
    o_ref[0] = jnp.concatenate([a + bias for a in acc] + pad, axis=0)


def _modulation(conds, ada_w, ada_b):
    depth, d, n = ada_w.shape
    ncond = conds.shape[0]
    s_cols = jnp.pad(jax.nn.silu(conds).T, ((0, 0), (0, 8 - ncond)))
    return pl.pallas_call(
        functools.partial(_mod_kernel, ncond=ncond),
        grid=(depth, n // MOD_TN),
        in_specs=[pl.BlockSpec((d, 8), lambda l, j: (0, 0)),
                  pl.BlockSpec((1, d, MOD_TN), lambda l, j: (l, 0, j)),
                  pl.BlockSpec((1, 1, MOD_TN), lambda l, j: (l, 0, j))],
        out_specs=pl.BlockSpec((1, 8, MOD_TN), lambda l, j: (l, 0, j)),
        out_shape=jax.ShapeDtypeStruct((depth, 8, n), F32),
        compiler_params=_cparams(("parallel", "parallel")),
        name="adaln_modulation",
    )(s_cols, ada_w, ada_b.reshape(depth, 1, n))


PROJ_TM = 544
PROJ_TN = 1408


def _modulated_norm(x, gain, mod_ref, shift_row, row0, ctx_len):
    y = x * lax.rsqrt(jnp.mean(x * x, axis=-1, keepdims=True) + NORM_EPS) * gain
    rows = row0 + lax.broadcasted_iota(jnp.int32, (x.shape[0], 1), 0)
    is_ctx = rows < ctx_len
    shift = jnp.where(is_ctx, mod_ref[0, 0, shift_row:shift_row + 1, :], mod_ref[0, 1, shift_row:shift_row + 1, :])
    scale = jnp.where(is_ctx, mod_ref[0, 0, shift_row + 1:shift_row + 2, :],
                      mod_ref[0, 1, shift_row + 1:shift_row + 2, :])
    return y * (1.0 + scale) + shift


def _norm_proj_kernel(x_ref, g_ref, mod_ref, w_ref, o_ref, h_scr, *, tiles_per_seq, ctx_len):
    row0 = (pl.program_id(0) % tiles_per_seq) * x_ref.shape[0]

    @pl.when(pl.program_id(1) == 0)
    def _():
        h_scr[...] = _modulated_norm(x_ref[...], g_ref[...], mod_ref, 0, row0, ctx_len).astype(BF16)

    o_ref[...] = jnp.dot(h_scr[...], w_ref[...], preferred_element_type=F32)


def _norm_proj(x, gain, mod, w, ctx_len):
    nb, t, d = x.shape
    n = w.shape[1]
    tm, tn = _pick_tile(t, PROJ_TM), PROJ_TN
    assert n % tn == 0
    tps = t // tm
    out = pl.pallas_call(
        functools.partial(_norm_proj_kernel, tiles_per_seq=tps, ctx_len=ctx_len),
        grid=(nb * tps, n // tn),
        in_specs=[pl.BlockSpec((tm, d), lambda i, j: (i, 0)),
                  pl.BlockSpec((1, d), lambda i, j: (0, 0)),
                  pl.BlockSpec((1, 2, 8, d), lambda i, j: (i // tps, 0, 0, 0)),
                  pl.BlockSpec((d, tn), lambda i, j: (0, j))],
        out_specs=pl.BlockSpec((tm, tn), lambda i, j: (i, j)),
        out_shape=jax.ShapeDtypeStruct((nb * t, n), F32),
        scratch_shapes=[pltpu.VMEM((tm, d), BF16)],
        compiler_params=_cparams(("parallel", "arbitrary")),
        name="norm_in_proj",
    )(x.reshape(nb * t, d), gain.reshape(1, d), mod, w)
    return out.reshape(nb, t, n)


OUT_TM = 272


def _out_proj_kernel(ya, yb, yc, yd, w_ref, x_ref, g_ref, mod_ref, wr_hi, wr_lo,
                     xo_ref, h_ref, aff_ref, *, tiles_per_seq, ctx_len, n_experts):
    gw = ya.shape[1]
    acc = None
    for j, y in enumerate((ya, yb, yc, yd)):
        t = jnp.dot(y[...].astype(BF16), w_ref[j * gw:(j + 1) * gw, :], preferred_element_type=F32)
        acc = t if acc is None else acc + t
    tm = x_ref.shape[0]
    row0 = (pl.program_id(0) % tiles_per_seq) * tm
    rows = row0 + lax.broadcasted_iota(jnp.int32, (tm, 1), 0)
    gate = jnp.where(rows < ctx_len, mod_ref[0, 0, 2:3, :], mod_ref[0, 1, 2:3, :])
    x_new = x_ref[...] + gate * acc
    xo_ref[...] = x_new
    h = _modulated_norm(x_new, g_ref[...], mod_ref, 3, row0, ctx_len)
    h_ref[...] = h.astype(BF16)
    h_hi = h.astype(BF16)
    h_lo = (h - h_hi.astype(F32)).astype(BF16)
    logits = (jnp.dot(h_hi, wr_hi[...], preferred_element_type=F32)
              + jnp.dot(h_lo, wr_hi[...], preferred_element_type=F32)
              + jnp.dot(h_hi, wr_lo[...], preferred_element_type=F32))
    lane = lax.broadcasted_iota(jnp.int32, logits.shape, 1)
    logits = jnp.where(lane < n_experts, logits, -jnp.inf)
    p = jnp.exp(logits - jnp.max(logits, axis=-1, keepdims=True))
    aff_ref[...] = p / jnp.sum(p, axis=-1, keepdims=True)


def _out_proj(ys, w_out, x, gain, mod, w_router, ctx_len):
    nb, t, d = x.shape
    gw = ys[0].shape[-1]
    ne = w_router.shape[1]
    tm = _pick_tile(t, OUT_TM)
    tps = t // tm
    wr =jnp.pad(w_router, ((0, 0), (0, LANES - ne)))
    wr_hi = wr.astype(BF16)
    wr_lo = (wr - wr_hi.astype(F32)).astype(BF16)
    row = lambda w: pl.BlockSpec((tm, w), lambda i: (i, 0))
    full = lambda a: pl.BlockSpec(a.shape, lambda i: (0,) * a.ndim)
    xo, h, aff = pl.pallas_call(
        functools.partial(_out_proj_kernel, tiles_per_seq=tps, ctx_len=ctx_len, n_experts=ne),
        grid=(nb * tps,),
        in_specs=[row(gw)] * 4 + [full(w_out), row(d), pl.BlockSpec((1, d), lambda i: (0, 0)),
                                  pl.BlockSpec((1, 2, 8, d), lambda i: (i // tps, 0, 0, 0)),
                                  full(wr_hi), full(wr_lo)],
        out_specs=[row(d), row(d), row(LANES)],
        out_shape=[jax.ShapeDtypeStruct((nb * t, d), F32), jax.ShapeDtypeStruct((nb * t, d), BF16),
                   jax.ShapeDtypeStruct((nb * t, LANES), F32)],
        compiler_params=_cparams(("parallel",)),
        name="out_proj_norm_router",
    )(*[y.reshape(nb * t, gw) for y in ys], w_out, x.reshape(nb * t, d), gain.reshape(1, d), mod, wr_hi, wr_lo)
    return xo.reshape(nb, t, d), h.reshape(nb, t, d), aff.reshape(nb, t, LANES)[..., :ne]


FFN_TF = 256


def _ffn_kernel(x_ref, w1_ref, w3_ref, w2_ref, o_ref):
    @pl.when(pl.program_id(1) == 0)
    def _():
        o_ref[...] = jnp.zeros_like(o_ref)

    xs = x_ref[0]
    a = jnp.dot(xs, w1_ref[0].astype(BF16), preferred_element_type=F32)
    b = jnp.dot(xs, w3_ref[0].astype(BF16), preferred_element_type=F32)
    hid = (a * jax.nn.sigmoid(a) * b).astype(BF16)
    o_ref[0] += jnp.dot(hid, w2_ref[0].astype(BF16), preferred_element_type=F32)


def _expert_ffn(xs, w1, w3, w2):
    ne, r, d = xs.shape
    f = w1.shape[2]
    tf = FFN_TF
    return pl.pallas_call(
        _ffn_kernel,
        grid=(ne, f // tf),
        in_specs=[pl.BlockSpec((1, r, d), lambda e, j: (e, 0, 0)),
                  pl.BlockSpec((1, d, tf), lambda e, j: (e, 0, j)),
                  pl.BlockSpec((1, d, tf), lambda e, j: (e, 0, j)),
                  pl.BlockSpec((1, tf, d), lambda e, j: (e, j, 0))],
        out_specs=pl.BlockSpec((1, r, d), lambda e, j: (e, 0, 0)),
        out_shape=jax.ShapeDtypeStruct((ne, r, d), F32),
        compiler_params=_cparams(("parallel", "arbitrary")),
        name="expert_ffn",
    )(xs, w1, w3, w2)


def _moe(h, aff, segments, w1, w3, w2):
    nb, t, d = h.shape
    ne = aff.shape[-1]
    picks, xs = [], []
    for start, length in segments:
        cap = EC_FACTOR * length // ne
        gate, idx = lax.top_k(jnp.swapaxes(aff[:, start:start + length], 1, 2), cap)
        idx = idx + start
        picks.append((gate, idx))
        rows = jnp.take_along_axis(h[:, None], idx[..., None], axis=2)
        xs.append(jnp.swapaxes(rows, 0, 1).reshape(ne, nb * cap, d))
    y = _expert_ffn(jnp.concatenate(xs, axis=1), w1, w3, w2)
    out = jnp.zeros((nb, t, d), F32)
    bidx = jnp.arange(nb)[:, None, None]
    off = 0
    for (start, length), (gate, idx) in zip(segments, picks):
        cap = gate.shape[-1]
        ys = jnp.swapaxes(y[:, off:off + nb * cap].reshape(ne, nb, cap, d), 0, 1)
        out = out.at[bidx, idx].add(ys * gate[..., None])
        off += nb * cap
    return out


def _relayout_w_in(w_in, gw, rw_cols, ngate):
    ml0 = rw_cols
    gd0 = ml0 + 4 * gw + ngate
    at0 = gd0 + 4 * gw + ngate
    parts = [w_in[:, ml0:ml0 + 4 * gw], w_in[:, gd0:gd0 + 4 * gw], w_in[:, at0:], w_in[:, :rw_cols],
             w_in[:, ml0 + 4 * gw:gd0], w_in[:, gd0 + 4 * gw:at0]]
    n = sum(p.shape[1] for p in parts)
    padded = -(-n // PROJ_TN) * PROJ_TN
    parts.append(jnp.zeros((w_in.shape[0], padded - n), w_in.dtype))
    return jnp.concatenate(parts, axis=1).astype(BF16)


def kernel(x, c, ctx, c_ctx, ada_w, ada_b, norm1_g, norm2_g, w_in, w_out, rw_mu, rw_w0, rw_w_up, rw_a0, rw_a_up,
           rw_g_up, rw_k_k, rw_k_a, rw_r_k, rw_ln_w, rw_ln_b, ml_ib, ml_fb, ml_norm_g, gd_conv, gd_a_log,
           gd_dt_bias, gd_norm_g, at_q_norm, at_k_norm, w_router, w_exp1, w_exp3, w_exp2, final_g):
    nb, n_lat, d = x.shape
    ctx_len = ctx.shape[1]
    t = ctx_len + n_lat
    depth = ada_w.shape[0]
    gw = d // N_GROUPS
    rw_cols = rw_mu.shape[-1]
    ngate = 4 * ml_ib.shape[-1]
    at_cols = w_in.shape[-1] - rw_cols - 2 * (4 * gw + ngate)

    xs = jnp.concatenate([ctx, x], axis=1)
    mods = _modulation(jnp.concatenate([c, c_ctx[None]], axis=0), ada_w, ada_b)
    rope = _rope_tables(t, ctx_len)

    for layer in range(depth):
        last = layer == depth - 1
        m = mods[layer, :nb + 1].reshape(nb + 1, 6, d)
        m = jnp.pad(m, ((0, 0), (0, 2), (0, 0)))
        mod = jnp.stack([jnp.broadcast_to(m[nb], (nb, 8, d)), m[:nb]], axis=1)
        proj = _norm_proj(xs, norm1_g[layer], mod, _relayout_w_in(w_in[layer], gw, rw_cols, ngate), ctx_len)
        o_gd, o_at, o_rw = 4 * gw, 8 * gw, 8 * gw + at_cols
        o_g = o_rw + rw_cols
        p_rw = {k: v[layer] for k, v in dict(
            rw_mu=rw_mu, rw_w0=rw_w0, rw_w_up=rw_w_up, rw_a0=rw_a0, rw_a_up=rw_a_up, rw_g_up=rw_g_up,
            rw_k_k=rw_k_k, rw_k_a=rw_k_a, rw_r_k=rw_r_k, rw_ln_w=rw_ln_w, rw_ln_b=rw_ln_b).items()}
        ya = _rwkv_mixer(proj[..., o_rw:o_rw + rw_cols], p_rw, ctx_len)
        yb = _mlstm_mixer(proj, 0, proj[..., o_g:o_g + ngate], ml_ib[layer], ml_fb[layer], ml_norm_g[layer], ctx_len)
        yc = _gdn_mixer(proj[..., o_gd:o_gd + 3 * gw], proj[..., o_gd + 3 * gw:o_gd + 4 * gw],
                        proj[..., o_g + ngate:o_g + 2 * ngate], gd_conv[layer], gd_a_log[layer],
                        gd_dt_bias[layer], gd_norm_g[layer], ctx_len)
        yd = _attn_mixer(proj[..., o_at:o_at + at_cols], at_q_norm[layer], at_k_norm[layer], ctx_len, rope)
        xs, h2, aff = _out_proj((ya, yb, yc, yd), w_out[layer].astype(BF16), xs, norm2_g[layer], mod,
                                w_router[layer], ctx_len)
        segments = [(ctx_len, n_lat)] if last else [(0, ctx_len), (ctx_len, n_lat)]
        moe = _moe(h2, aff, segments, w_exp1[layer], w_exp3[layer], w_exp2[layer])
        rows = jnp.arange(t)[None, :, None]
        gate2 = jnp.where(rows < ctx_len, mod[:, 0, 5][:, None, :], mod[:, 1, 5][:, None, :])
        xs = xs + gate2 * moe

    xl = xs[:, ctx_len:]
    return xl * lax.rsqrt(jnp.mean(xl * xl, axis=-1, keepdims=True) + NORM_EPS) * final_g
```

```python
import functools
import math

import jax
import jax.numpy as jnp
from jax import lax
from jax.experimental import pallas as pl
from jax.experimental.pallas import tpu as pltpu

F32 = jnp.float32
BF16 = jnp.bfloat16

NORM_EPS = 1e-6
GRID_W = 64
N_GROUPS = 4
RW_HEAD = 64
RW_DECAY_RANK = 64
RW_ICLR_RANK = 64
RW_GATE_RANK = 128
RW_DECAY_SCALE = math.exp(-0.5)
RW_GN_EPS = 64e-5
RW_CHUNK = 64
ML_HEAD = 128
ML_CHUNK = 256
GD_HEAD = 128
GD_CHUNK = 64
AT_HEAD = 128
AT_KV_HEADS = 2
ROPE_THETA = 10000.0
AT_TQ = 256
N_EXPERTS = 16
EC_FACTOR = 2

VMEM_LIMIT = 56 * 1024 * 1024
LANES = 128


def _cparams(sem):
    return pltpu.CompilerParams(dimension_semantics=sem, vmem_limit_bytes=VMEM_LIMIT)


def _pick_tile(n, limit, mult=16):
    best = None
    for cand in range(mult, min(n, limit) + 1, mult):
        if n % cand == 0:
            best = cand
    assert best is not None, (n, limit, mult)
    return best


_DIMS = {
    "nn": (((1,), (0,)), ((), ())),
    "nt": (((1,), (1,)), ((), ())),
    "tn": (((0,), (0,)), ((), ())),
}


def _split(a, n):
    if a.dtype == BF16:
        return [a]
    pieces, rest = [], a
    for i in range(n):
        p = rest.astype(BF16)
        pieces.append(p)
        if i + 1 < n:
            rest = rest - p.astype(F32)
    return pieces


def _mm(a, b, dims="nn", pa=1, pb=1):
    ap, bp = _split(a, pa), _split(b, pb)
    top = max(len(ap), len(bp))
    acc = None
    for i, x in enumerate(ap):
        for j, y in enumerate(bp):
            if i + j < top:
                t = lax.dot_general(x, y, _DIMS[dims], preferred_element_type=F32)
                acc = t if acc is None else acc + t
    return acc


def _tri(n, rev, strict, reps=1):
    t = lax.broadcasted_iota(jnp.int32, (n, reps * n), 0)
    s = lax.broadcasted_iota(jnp.int32, (n, reps * n), 1)
    if reps > 1:
        assert n & (n - 1) == 0
        s = s & (n - 1)
    if rev:
        return (s > t) if strict else (s >= t)
    return (s < t) if strict else (s <= t)


def _chunk_maps(nsteps, nctx):
    fwd = lambda i: i
    bwd = lambda i: jnp.where(i < nctx, nctx - 1 - i, nsteps - 1 - i + nctx)
    return fwd, bwd


def _unit_lower_inverse(a, levels, passes):
    n = a.shape[0]
    eye = (lax.broadcasted_iota(jnp.int32, (n, n), 0) == lax.broadcasted_iota(jnp.int32, (n, n), 1)).astype(F32)
    y = -a
    r = eye + y
    for _ in range(levels):
        y = _mm(y, y, "nn", passes, passes)
        r = r + _mm(r, y, "nn", passes, passes)
    return r


def _mlstm_kernel(qf, kf, vf, gf, gtf, qb, kb, vb, gb, gtb, yf, yb, c_scr, m_scr, *, nb, nh, chunk):
    step = pl.program_id(0)

    @pl.when(step == 0)
    def _():
        c_scr[...] = jnp.zeros_like(c_scr)
        m_scr[...] = jnp.zeros_like(m_scr)

    scale = ML_HEAD ** -0.5
    lane = lax.broadcasted_iota(jnp.int32, (chunk, ML_HEAD), 1)
    ones_col = (lane == 0).astype(F32)
    units = []
    for d, (q_ref, k_ref, v_ref, g_ref, gt_ref, y_ref) in enumerate(
            ((qf, kf, vf, gf, gtf, yf), (qb, kb, vb, gb, gtb, yb))):
        rev = d == 1
        incl = _tri(chunk, rev, False)
        cum_m = incl.astype(BF16)
        last = 0 if rev else chunk - 1
        for b in range(nb):
            g = g_ref[b]
            gt = gt_ref[b]
            cum_c = _mm(cum_m, g, "nn", 1, 3)
            cum_r = _mm(gt, cum_m, "nt", 3, 1)
            for h in range(nh):
                ci, cf = d * nh + h, (2 + d) * nh + h
                sl = slice(h * ML_HEAD, (h + 1) * ML_HEAD)
                u = dict(y_ref=y_ref, b=b, sl=sl, idx=(d * nb + b) * nh + h, last=last)
                u["q"] = q_ref[b, :, sl].astype(BF16)
                u["k"] = k_ref[b, :, sl] * scale
                u["v_aug"] = jnp.concatenate([v_ref[b, :, sl], ones_col], axis=1).astype(BF16)
                u["m_old"] = m_scr[u["idx"]][0:1, 0:1]
                u["bc"] = cum_c[:, cf:cf + 1]
                u["li_c"] = g[:, ci:ci + 1]
                dlog = jnp.where(incl, u["bc"] - cum_r[cf:cf + 1, :] + gt[ci:ci + 1, :], -jnp.inf)
                inter = u["bc"] + u["m_old"]
                u["mt"] = jnp.maximum(inter, jnp.max(dlog, axis=1, keepdims=True))
                u["p"] = jnp.exp(dlog - u["mt"])
                u["wi"] = jnp.exp(inter - u["mt"])
                units.append(u)
    for u in units:
        u["s"] = (_mm(u["q"], u["k"], "nt") * u["p"]).astype(BF16)
    for u in units:
        u["c_old"] = c_scr[u["idx"]]
        u["acc"] = _mm(u["s"], u["v_aug"]) + u["wi"] * _mm(u["q"], u["c_old"])
    for u in units:
        num = u["acc"][:, :ML_HEAD]
        den = u["acc"][:, ML_HEAD:ML_HEAD + 1]
        u["y_ref"][u["b"], :, u["sl"]] = num / jnp.maximum(jnp.abs(den), jnp.exp(-u["mt"]))
    for u in units:
        last = u["last"]
        m_new = u["mt"][last:last + 1, :]
        b_last = u["bc"][last:last + 1, :]
        wk = jnp.exp(b_last - u["bc"] + u["li_c"] - m_new)
        dc = jnp.exp(b_last + u["m_old"] - m_new)
        c_scr[u["idx"]] = dc * u["c_old"] + _mm(u["k"] * wk, u["v_aug"], "tn")
        m_scr[u["idx"]] = jnp.broadcast_to(m_new, (8, LANES))


def _mlstm_scan(q_src, k_src, v_src, cols, g, gt, ctx_len):
    nb, t, _ = g.shape
    nh = g.shape[-1] // 4
    width = nh * ML_HEAD
    chunk = ML_CHUNK
    nsteps, nctx = t // chunk, ctx_len // chunk
    fwd, bwd = _chunk_maps(nsteps, nctx)

    def specs(cm):
        col = lambda c: pl.BlockSpec((nb, chunk, width), lambda i, c=c: (0, cm(i), c))
        return [col(cols[0]), col(cols[1]), col(cols[2]),
                pl.BlockSpec((nb, chunk, 4 * nh), lambda i: (0, cm(i), 0)),
                pl.BlockSpec((nb, 4 * nh, chunk), lambda i: (0, 0, cm(i)))]

    out_spec = lambda cm: pl.BlockSpec((nb, chunk, width), lambda i: (0, cm(i), 0))
    kern = functools.partial(_mlstm_kernel, nb=nb, nh=nh, chunk=chunk)
    return pl.pallas_call(
        kern,
        grid=(nsteps,),
        in_specs=specs(fwd) + specs(bwd),
        out_specs=[out_spec(fwd), out_spec(bwd)],
        out_shape=[jax.ShapeDtypeStruct((nb, t, width), F32)] * 2,
        scratch_shapes=[pltpu.VMEM((2 * nb * nh, ML_HEAD, 2 * ML_HEAD), F32),
                        pltpu.VMEM((2 * nb * nh, 8, LANES), F32)],
        compiler_params=_cparams(("arbitrary",)),
        name="mlstm_scan",
    )(q_src, k_src, v_src, g, gt, q_src, k_src, v_src, g, gt)


def _head_layer_norm(y, nh, eps):
    b, t, w = y.shape
    yh = y.reshape(b, t, nh, w // nh)
    mean = jnp.mean(yh, axis=-1, keepdims=True)
    var = jnp.mean(jnp.square(yh - mean), axis=-1, keepdims=True)
    return ((yh - mean) * lax.rsqrt(var + eps)).reshape(b, t, w)


def _mlstm_mixer(proj, col0, graw, ml_ib, ml_fb, ml_norm_g, ctx_len):
    nh = graw.shape[-1] // 4
    width = nh * ML_HEAD
    li = graw[..., :2 * nh] + ml_ib.reshape(-1)
    lf = jax.nn.log_sigmoid(graw[..., 2 * nh:] + ml_fb.reshape(-1))
    g = jnp.concatenate([li, lf], axis=-1)
    yf, yb = _mlstm_scan(proj, proj, proj, (col0, col0 + 1, col0 + 2), g, jnp.swapaxes(g, 1, 2), ctx_len)
    y = _head_layer_norm(yf + yb, nh, NORM_EPS) * ml_norm_g
    o = proj[..., (col0 + 3) * width:(col0 + 4) * width]
    return y * jax.nn.sigmoid(o)


def _gdn_kernel(qf, kf, vf, gf, gtf, qb, kb, vb, gb, gtb, yf, yb, s_scr, *, nb, nh, chunk):
    step = pl.program_id(0)

    @pl.when(step == 0)
    def _():
        s_scr[...] = jnp.zeros_like(s_scr)

    levels = int(math.log2(chunk)) - 1
    c = chunk
    dh = GD_HEAD
    pw = 2 * dh
    first = lax.broadcasted_iota(jnp.int32, (c, pw), 1) < dh
    first2 = lax.broadcasted_iota(jnp.int32, (c, 2 * c), 1) < c
    first2_row = lax.broadcasted_iota(jnp.int32, (1, 2 * c), 1) < c
    first4 = (lax.broadcasted_iota(jnp.int32, (c, 2 * pw), 1) // dh) % 2 == 0
    same_head = (lax.broadcasted_iota(jnp.int32, (pw, pw), 0) // dh
                 == lax.broadcasted_iota(jnp.int32, (pw, pw), 1) // dh)
    same_blk = (lax.broadcasted_iota(jnp.int32, (2 * c, 2 * c), 0) // c
                == lax.broadcasted_iota(jnp.int32, (2 * c, 2 * c), 1) // c)
    eye2 = (lax.broadcasted_iota(jnp.int32, (c, 2 * c), 0)
            == lax.broadcasted_iota(jnp.int32, (c, 2 * c), 1) % c).astype(F32)

    def by_head(x, mask):
        return jnp.concatenate([jnp.where(mask, x, 0.0), jnp.where(mask, 0.0, x)], axis=0)

    def blockdiag(x):
        xb = x.astype(BF16)
        return jnp.where(same_blk, jnp.concatenate([xb, xb], axis=0), jnp.zeros((), BF16))

    units = []
    for d, (q_ref, k_ref, v_ref, g_ref, gt_ref, y_ref) in enumerate(
            ((qf, kf, vf, gf, gtf, yf), (qb, kb, vb, gb, gtb, yb))):
        rev = d == 1
        incl = _tri(c, rev, False)
        incl2 = _tri(c, rev, False, reps=2)
        strict2 = _tri(c, rev, True, reps=2)
        cum_m = incl.astype(BF16)
        cum_m2 = jnp.concatenate([cum_m, cum_m], axis=0)
        last = 0 if rev else c - 1
        for b in range(nb):
            g = g_ref[b]
            gt = gt_ref[b, 0]
            cum_c = _mm(cum_m, g, "nn", 1, 3)
            cum_r = _mm(gt, cum_m2, "nt", 3, 1)
            for hp in range(nh // 2):
                h0, h1 = 2 * hp, 2 * hp + 1
                sl = slice(hp * pw, (hp + 1) * pw)
                u = dict(y_ref=y_ref, b=b, sl=sl, idx=(d * nb + b) * (nh // 2) + hp, strict2=strict2)
                q, k, v = q_ref[b, :, sl], k_ref[b, :, sl], v_ref[b, :, sl]
                col = lambda arr, j: jnp.where(first, arr[:, j + h0:j + h0 + 1], arr[:, j + h1:j + h1 + 1])
                gc = col(cum_c, d * nh)
                beta = col(g, (2 + d) * nh)
                gc_c2 = jnp.where(first2, cum_c[:, d * nh + h0:d * nh + h0 + 1], cum_c[:, d * nh + h1:d * nh + h1 + 1])
                gc_r2 = jnp.where(first2_row, cum_r[d * nh + h0:d * nh + h0 + 1], cum_r[d * nh + h1:d * nh + h1 + 1])
                u["decay"] = jnp.where(incl2, jnp.exp(jnp.where(incl2, gc_c2 - gc_r2, 0.0)), 0.0)
                kb_ = k * beta
                eg = jnp.exp(gc)
                u["kq"] = jnp.concatenate([kb_, q], axis=0).astype(BF16)
                u["k_blk"] = by_head(k, first).astype(BF16)
                u["rhs"] = by_head(jnp.concatenate([v * beta, kb_ * eg], axis=1), first4).astype(BF16)
                u["qg"] = q * eg
                g_last = gc[last:last + 1, :]
                u["kd"] = (k * jnp.exp(g_last - gc)).astype(BF16)
                u["keep"] = jnp.exp(g_last)
                units.append(u)

    for u in units:
        kq = _mm(u["kq"], u["k_blk"], "nt")
        u["y"] = jnp.where(u["strict2"], -kq[:c] * u["decay"], 0.0)
        u["attn"] = (kq[c:] * u["decay"]).astype(BF16)
        u["inv"] = eye2 + u["y"]
        u["bd"] = blockdiag(u["y"])
    for _ in range(levels):
        for u in units:
            u["y"] = _mm(u["y"], u["bd"])
            u["bd"] = blockdiag(u["y"])
        for u in units:
            u["inv"] = u["inv"] + _mm(u["inv"], u["bd"])
    for u in units:
        u["sol"] = _mm(u["inv"], u["rhs"])
    for u in units:
        u["s_old"] = s_scr[u["idx"]]
        u["ws"] = _mm(jnp.concatenate([u["sol"][:, pw:], u["qg"]], axis=0), u["s_old"])
    for u in units:
        u["v_new"] = u["sol"][:, :pw] - u["ws"][:c]
        u["y_ref"][u["b"], :, u["sl"]] = u["ws"][c:] + _mm(u["attn"], by_head(u["v_new"], first))
    for u in units:
        upd = _mm(u["kd"], u["v_new"], "tn")
        s_scr[u["idx"]] = u["s_old"] * u["keep"] + jnp.where(same_head, upd, 0.0)


def _gdn_scan(q, k, v, g, ctx_len):
    nb, t, width = q.shape
    nh = width // GD_HEAD
    chunk = GD_CHUNK
    nsteps, nctx = t // chunk, ctx_len // chunk
    fwd, bwd = _chunk_maps(nsteps, nctx)
    gt = jnp.swapaxes(g.reshape(nb, nsteps, chunk, 4 * nh), 2, 3)

    def specs(cm):
        col = pl.BlockSpec((nb, chunk, width), lambda i: (0, cm(i), 0))
        return [col, col, col,
                pl.BlockSpec((nb, chunk, 4 * nh), lambda i: (0, cm(i), 0)),
                pl.BlockSpec((nb, 1, 4 * nh, chunk), lambda i: (0, cm(i), 0, 0))]

    out_spec = lambda cm: pl.BlockSpec((nb, chunk, width), lambda i: (0, cm(i), 0))
    kern = functools.partial(_gdn_kernel, nb=nb, nh=nh, chunk=chunk)
    return pl.pallas_call(
        kern,
        grid=(nsteps,),
        in_specs=specs(fwd) + specs(bwd),
        out_specs=[out_spec(fwd), out_spec(bwd)],
        out_shape=[jax.ShapeDtypeStruct((nb, t, width), F32)] * 2,
        scratch_shapes=[pltpu.VMEM((nb * nh, 2 * GD_HEAD, 2 * GD_HEAD), F32)],
        compiler_params=_cparams(("arbitrary",)),
        name="gdn_scan",
    )(q, k, v, g, gt, q, k, v, g, gt)


def _seq_shift(x, ctx_len, delta):
    t = x.shape[1]
    rows = jnp.arange(t)
    if delta < 0:
        sh = jnp.pad(x[:, :delta], ((0, 0), (-delta, 0), (0, 0)))
        ok = (rows + delta >= 0) & ((rows < ctx_len) | (rows + delta >= ctx_len))
    else:
        sh = jnp.pad(x[:, delta:], ((0, 0), (0, delta), (0, 0)))
        ok = (rows + delta < t) & ((rows >= ctx_len) | (rows + delta < ctx_len))
    return jnp.where(ok[None, :, None], sh, 0.0)


def _l2n(x, nh, eps=1e-6):
    b, t, w = x.shape
    xh = x.reshape(b, t, nh, w // nh)
    return (xh * lax.rsqrt(jnp.sum(xh * xh, axis=-1, keepdims=True) + eps)).reshape(b, t, w)


def _gdn_mixer(qkv, gate, graw, gd_conv, gd_a_log, gd_dt_bias, gd_norm_g, ctx_len):
    nh = graw.shape[-1] // 4
    width = nh * GD_HEAD
    pad = gd_conv.shape[0] // 2
    conv = sum(_seq_shift(qkv, ctx_len, j - pad) * gd_conv[j] for j in range(gd_conv.shape[0]))
    qkv = jax.nn.silu(conv)
    q = _l2n(qkv[..., :width], nh) * GD_HEAD ** -0.5
    k = _l2n(qkv[..., width:2 * width], nh)
    v = qkv[..., 2 * width:]
    lg = -jnp.exp(gd_a_log).reshape(-1) * jax.nn.softplus(graw[..., :2 * nh] + gd_dt_bias.reshape(-1))
    beta = jax.nn.sigmoid(graw[..., 2 * nh:])
    yf, yb = _gdn_scan(q, k, v, jnp.concatenate([lg, beta], axis=-1), ctx_len)
    o = yf + yb
    b, t, _ = o.shape
    oh = o.reshape(b, t, nh, GD_HEAD)
    oh = oh * lax.rsqrt(jnp.mean(oh * oh, axis=-1, keepdims=True) + NORM_EPS) * gd_norm_g
    return oh.reshape(b, t, width) * jax.nn.silu(gate)


RW_PASSES = 1


def _rwkv_kernel(*refs, nb, npair, chunk):
    ins, (yf, yb, s_scr) = refs[:12], refs[12:]
    step = pl.program_id(0)

    @pl.when(step == 0)
    def _():
        s_scr[...] = jnp.zeros_like(s_scr)

    levels = int(math.log2(chunk)) - 1
    pw = 2 * RW_HEAD
    c = chunk
    head0 = lax.broadcasted_iota(jnp.int32, (c, pw), 1) < RW_HEAD
    same_head = (lax.broadcasted_iota(jnp.int32, (pw, pw), 0) // RW_HEAD
                 == lax.broadcasted_iota(jnp.int32, (pw, pw), 1) // RW_HEAD)
    same_blk = (lax.broadcasted_iota(jnp.int32, (2 * c, 2 * c), 0) // c
                == lax.broadcasted_iota(jnp.int32, (2 * c, 2 * c), 1) // c)
    eye2 = (lax.broadcasted_iota(jnp.int32, (c, 2 * c), 0)
            == lax.broadcasted_iota(jnp.int32, (c, 2 * c), 1) % c).astype(F32)
    mid = c // 2

    def by_head(x):
        return jnp.concatenate([jnp.where(head0, x, 0.0), jnp.where(head0, 0.0, x)], axis=0)

    def blockdiag(x):
        xb = x.astype(BF16)
        return jnp.where(same_blk, jnp.concatenate([xb, xb], axis=0), jnp.zeros((), BF16))

    units = []
    for d, y_ref in enumerate((yf, yb)):
        r_ref, lw_ref, kt_ref, v_ref, kk_ref, b_ref = ins[6 * d:6 * d + 6]
        rev = d == 1
        incl = _tri(c, rev, False)
        strict = _tri(c, rev, True)
        incl2 = _tri(c, rev, False, reps=2)
        strict2 = _tri(c, rev, True, reps=2)
        cum_m = incl.astype(BF16)
        last = 0 if rev else c - 1
        for b in range(nb):
            lw_all = lw_ref[b]
            g_all = _mm(cum_m, lw_all, "nn", 1, 3)
            for p in range(npair):
                sl = slice(p * pw, (p + 1) * pw)
                u = dict(y_ref=y_ref, b=b, sl=sl, idx=(d * nb + b) * npair + p, incl2=incl2, strict2=strict2)
                lw, g = lw_all[:, sl], g_all[:, sl]
                r, kt, v, kk, bb = (x[b, :, sl] for x in (r_ref, kt_ref, v_ref, kk_ref, b_ref))
                gh = g - g[mid:mid + 1, :]
                e_pos, e_neg = jnp.exp(gh), jnp.exp(-gh)
                u["lhs"] = jnp.concatenate([jnp.exp(gh - lw) * kk, r * e_pos], axis=0).astype(BF16)
                u["rhs"] = jnp.concatenate([by_head(bb * e_neg), by_head(kt * e_neg)], axis=0).astype(BF16)
                u["st_lhs"] = jnp.concatenate([jnp.exp(g - lw) * kk, jnp.exp(g) * r], axis=0).astype(BF16)
                g_last = g[last:last + 1, :]
                dl = jnp.exp(g_last - g)
                u["keep"] = jnp.exp(g_last)
                u["bk_d"] = jnp.concatenate([bb * dl, kt * dl], axis=0).astype(BF16)
                u["v"] = v
                u["v_blk"] = by_head(v).astype(BF16)
                units.append(u)

    for u in units:
        u["s_old"] = s_scr[u["idx"]]
        u["from_state"] = _mm(u["st_lhs"], u["s_old"], "nt")
    for u in units:
        gm = _mm(u["lhs"], u["rhs"], "nt")
        u["y"] = jnp.where(u["strict2"], -gm[:c, :2 * c], 0.0)
        u["a_ak"] = jnp.where(u["strict2"], gm[:c, 2 * c:], 0.0).astype(BF16)
        u["r_mat"] = jnp.concatenate([jnp.where(u["incl2"], -gm[c:, :2 * c], 0.0),
                                      jnp.where(u["incl2"], gm[c:, 2 * c:], 0.0)], axis=1).astype(BF16)
        u["inv"] = eye2 + u["y"]
        u["bd"] = blockdiag(u["y"])
    for u in units:
        u["rhs_e"] = u["from_state"][:c] + _mm(u["a_ak"], u["v_blk"])
    for _ in range(levels):
        for u in units:
            u["y"] = _mm(u["y"], u["bd"])
            u["bd"] = blockdiag(u["y"])
        for u in units:
            u["inv"] = u["inv"] + _mm(u["inv"], u["bd"])
    for u in units:
        u["e"] = _mm(u["inv"], by_head(u["rhs_e"]))
    for u in units:
        ev = jnp.concatenate([by_head(u["e"]), by_head(u["v"])], axis=0)
        u["y_ref"][u["b"], :, u["sl"]] = u["from_state"][c:] + _mm(u["r_mat"], ev)
    for u in units:
        upd = _mm(jnp.concatenate([-u["e"], u["v"]], axis=0), u["bk_d"], "tn")
        s_scr[u["idx"]] = u["s_old"] * u["keep"] + jnp.where(same_head, upd, 0.0)


def _rwkv_scan(r, v, kk, per_dir, ctx_len):
    nb, t, width = r.shape
    chunk = RW_CHUNK
    nsteps, nctx = t // chunk, ctx_len // chunk
    fwd, bwd = _chunk_maps(nsteps, nctx)
    spec = lambda cm: pl.BlockSpec((nb, chunk, width), lambda i: (0, cm(i), 0))
    args = []
    for d in range(2):
        lw, bb, kt = per_dir[d]
        args += [r, lw, kt, v, kk, bb]
    kern = functools.partial(_rwkv_kernel, nb=nb, npair=width // (2 * RW_HEAD), chunk=chunk)
    return pl.pallas_call(
        kern,
        grid=(nsteps,),
        in_specs=[spec(fwd)] * 6 + [spec(bwd)] * 6,
        out_specs=[spec(fwd), spec(bwd)],
        out_shape=[jax.ShapeDtypeStruct((nb, t, width), F32)] * 2,
        scratch_shapes=[pltpu.VMEM((2 * nb * (width // (2 * RW_HEAD)), 2 * RW_HEAD, 2 * RW_HEAD), F32)],
        compiler_params=_cparams(("arbitrary",)),
        name="rwkv_scan",
    )(*args)


def _rwkv_mixer(pa, p, ctx_len):
    width = p['rw_k_k'].shape[0]
    nh = width // RW_HEAD
    mu = p['rw_mu']
    pp = pa + mu[0] * (_seq_shift(pa, ctx_len, -1) - pa) + mu[1] * (_seq_shift(pa, ctx_len, 1) - pa)
    r, k, v = pp[..., :width], pp[..., width:2 * width], pp[..., 2 * width:3 * width]
    o = 3 * width
    wd = pp[..., o:o + RW_DECAY_RANK]
    ad = pp[..., o + RW_DECAY_RANK:o + RW_DECAY_RANK + RW_ICLR_RANK]
    gd = pp[..., o + RW_DECAY_RANK + RW_ICLR_RANK:]
    g = jax.nn.sigmoid(gd) @ p['rw_g_up']
    kk = _l2n(k * p['rw_k_k'], nh)
    per_dir, kts = [], []
    for d in range(2):
        logw = -RW_DECAY_SCALE * jax.nn.sigmoid(p['rw_w0'][d] + jnp.tanh(wd) @ p['rw_w_up'][d])
        a = jax.nn.sigmoid(p['rw_a0'][d] + ad @ p['rw_a_up'][d])
        kt = k * (1.0 + (a - 1.0) * p['rw_k_a'])
        per_dir.append((logw, kk * a, kt))
        kts.append(kt)
    yf, yb = _rwkv_scan(r, v, kk, per_dir, ctx_len)
    y = _head_layer_norm(yf + yb, nh, RW_GN_EPS) * p['rw_ln_w'] + p['rw_ln_b']
    b, t, _ = y.shape
    bh = lambda x: x.reshape(b, t, nh, RW_HEAD)
    bonus = sum(jnp.sum(bh(r) * bh(kt) * p['rw_r_k'], axis=-1, keepdims=True) * bh(v) for kt in kts)
    return (y + bonus.reshape(y.shape)) * g


def _attn_kernel(q_ref, k_ref, v_ref, o_ref, *, ctx_tiles, ctx_len, group):
    qi = pl.program_id(2)

    def run(keys, vals):
        for g in range(group):
            sl = slice(g * AT_HEAD, (g + 1) * AT_HEAD)
            s = lax.dot_general(q_ref[0, :, sl], keys, _DIMS["nt"], preferred_element_type=F32)
            p = jnp.exp(s - jnp.max(s, axis=-1, keepdims=True))
            o = jnp.dot(p.astype(BF16), vals, preferred_element_type=F32)
            o_ref[0, :, sl] = o / jnp.sum(p, axis=-1, keepdims=True)

    @pl.when(qi < ctx_tiles)
    def _():
        run(k_ref[0, :ctx_len, :], v_ref[0, :ctx_len, :])

    @pl.when(qi >= ctx_tiles)
    def _():
        run(k_ref[0], v_ref[0])


def _attention(q, k, v, ctx_len):
    nb, t, qw = q.shape
    hkv = k.shape[-1] // AT_HEAD
    group = qw // AT_HEAD // hkv
    tq = AT_TQ
    assert ctx_len % tq == 0 and t % tq == 0
    kern = functools.partial(_attn_kernel, ctx_tiles=ctx_len // tq, ctx_len=ctx_len, group=group)
    return pl.pallas_call(
        kern,
        grid=(nb, hkv, t // tq),
        in_specs=[pl.BlockSpec((1, tq, group * AT_HEAD), lambda b, h, i: (b, i, h)),
                  pl.BlockSpec((1, t, AT_HEAD), lambda b, h, i: (b, 0, h)),
                  pl.BlockSpec((1, t, AT_HEAD), lambda b, h, i: (b, 0, h))],
        out_specs=pl.BlockSpec((1, tq, group * AT_HEAD), lambda b, h, i: (b, i, h)),
        out_shape=jax.ShapeDtypeStruct((nb, t, qw), F32),
        compiler_params=_cparams(("parallel", "parallel", "arbitrary")),
        name="gqa",
    )(q, k, v)


def _rope_tables(t, ctx_len):
    n_lat = t - ctx_len
    pos = jnp.arange(n_lat)
    row = (pos // GRID_W).astype(F32)
    col = (pos % GRID_W).astype(F32)
    axis_dim = AT_HEAD // 2
    inv_freq = ROPE_THETA ** (-jnp.arange(0, axis_dim, 2, dtype=F32) / axis_dim)
    ang_r = row[:, None] * inv_freq[None, :]
    ang_c = col[:, None] * inv_freq[None, :]
    cos = jnp.concatenate([jnp.cos(ang_r)] * 2 + [jnp.cos(ang_c)] * 2, axis=-1)
    sin = jnp.concatenate([-jnp.sin(ang_r), jnp.sin(ang_r), -jnp.sin(ang_c), jnp.sin(ang_c)], axis=-1)
    cos = jnp.concatenate([jnp.ones((ctx_len, AT_HEAD), F32), cos], axis=0)
    sin = jnp.concatenate([jnp.zeros((ctx_len, AT_HEAD), F32), sin], axis=0)
    return cos, sin


def _attn_mixer(pd, at_q_norm, at_k_norm, ctx_len, rope):
    nb, t, w = pd.shape
    kvw = AT_KV_HEADS * AT_HEAD
    qw = w - 2 * kvw
    cos, sin = rope

    def prep(x, g):
        xh = x.reshape(nb, t, -1, AT_HEAD)
        xh = xh * lax.rsqrt(jnp.mean(xh * xh, axis=-1, keepdims=True) + NORM_EPS) * g
        sw = jnp.flip(xh.reshape(nb, t, -1, 2, 2, AT_HEAD // 4), axis=-2).reshape(xh.shape)
        return (xh * cos[:, None, :] + sw * sin[:, None, :]).reshape(x.shape)

    q = prep(pd[..., :qw], at_q_norm) * AT_HEAD ** -0.5
    k = prep(pd[..., qw:qw + kvw], at_k_norm)
    v = pd[..., qw + kvw:]
    return _attention(q.astype(BF16), k.astype(BF16), v.astype(BF16), ctx_len)


PREP_TM = 256
HALO = 8


def _halo_specs(tm, width, col, t):
    per = tm // HALO
    last = t // HALO - 1
    prev = pl.BlockSpec((1, HALO, width), lambda b, i: (b, jnp.maximum(i * per - 1, 0), col))
    nxt = pl.BlockSpec((1, HALO, width), lambda b, i: (b, jnp.minimum((i + 1) * per, last), col))
    return prev, nxt


def _neighbours(x, prev_ref, next_ref, start, ctx_len, t):
    tm = x.shape[0]
    row = lax.broadcasted_iota(jnp.int32, x.shape, 0)
    has_prev = jnp.logical_and(start != 0, start != ctx_len)
    has_next = jnp.logical_and(start + tm != ctx_len, start + tm != t)
    prev_row = jnp.where(has_prev, prev_ref[0, HALO - 1:HALO, :], 0.0)
    next_row = jnp.where(has_next, next_ref[0, 0:1, :], 0.0)
    x_prev = jnp.where(row == 0, prev_row, pltpu.roll(x, 1, 0))
    x_next = jnp.where(row == tm - 1, next_row, pltpu.roll(x, tm - 1, 0))
    return x_prev, x_next


def _rwkv_prep_kernel(x_ref, xp_ref, xn_ref, mu_ref, wa_ref, gup_ref, w0_ref, a0_ref, kk_ref, ka_ref, rk_ref, hs_ref,
                      r_o, v_o, kkn_o, lwf_o, bf_o, ktf_o, lwb_o, bb_o, ktb_o, g_o, bonus_o, *, ctx_len, t, width):
    start = pl.program_id(1) * x_ref.shape[1]
    x = x_ref[0]
    x_prev, x_next = _neighbours(x, xp_ref, xn_ref, start, ctx_len, t)
    pp = x + mu_ref[0:1, :] * (x_prev - x) + mu_ref[1:2, :] * (x_next - x)
    w = width
    r, k, v = pp[:, :w], pp[:, w:2 * w], pp[:, 2 * w:3 * w]
    low = pp[:, 3 * w:3 * w + LANES]
    lane = lax.broadcasted_iota(jnp.int32, low.shape, 1)
    low = jnp.where(lane < RW_DECAY_RANK, jnp.tanh(low), low)
    g_o[0] = _mm(jax.nn.sigmoid(pp[:, 3 * w + LANES:]), gup_ref[...])
    hs = hs_ref[...]
    kx = k * kk_ref[...]
    kk = kx * lax.rsqrt(_mm(kx * kx, hs, "nn", 3, 1) + 1e-6)
    r_o[0], v_o[0], kkn_o[0] = r, v, kk
    kt_sum = None
    for d, (lw_o, b_o, kt_o) in enumerate(((lwf_o, bf_o, ktf_o), (lwb_o, bb_o, ktb_o))):
        up = _mm(low, wa_ref[d])
        lw_o[0] = -RW_DECAY_SCALE * jax.nn.sigmoid(w0_ref[d:d + 1, :] + up[:, :w])
        a = jax.nn.sigmoid(a0_ref[d:d + 1, :] + up[:, w:])
        kt = k * (1.0 + (a - 1.0) * ka_ref[...])
        b_o[0] = kk * a
        kt_o[0] = kt
        kt_sum = kt if kt_sum is None else kt_sum + kt
    bonus_o[0] = _mm(r * kt_sum * rk_ref[...], hs, "nn", 3, 1) * v


def _rwkv_prep(proj, col, p, hs, ctx_len):
    nb, t, _ = proj.shape
    width = p['rw_k_k'].shape[0]
    cols = p['rw_mu'].shape[-1]
    tm = PREP_TM
    assert ctx_len % tm == 0 and t % tm == 0 and cols == 3 * width + LANES + RW_GATE_RANK
    rank = RW_DECAY_RANK
    wa = jnp.zeros((2, LANES, 2 * width), F32)
    wa = wa.at[:, :rank, :width].set(p['rw_w_up']).at[:, rank:, width:].set(p['rw_a_up'])
    params = [p['rw_mu'], wa, p['rw_g_up'], p['rw_w0'], p['rw_a0'], p['rw_k_k'].reshape(1, width),
              p['rw_k_a'].reshape(1, width), p['rw_r_k'].reshape(1, width), hs]
    full = lambda a: pl.BlockSpec(a.shape, lambda b, i: (0,) * a.ndim)
    prev, nxt = _halo_specs(tm, cols, col, t)
    out_spec = pl.BlockSpec((1, tm, width), lambda b, i: (b, i, 0))
    return pl.pallas_call(
        functools.partial(_rwkv_prep_kernel, ctx_len=ctx_len, t=t, width=width),
        grid=(nb, t // tm),
        in_specs=[pl.BlockSpec((1, tm, cols), lambda b, i: (b, i, col)), prev, nxt] + [full(a) for a in params],
        out_specs=[out_spec] * 11,
        out_shape=[jax.ShapeDtypeStruct((nb, t, width), F32)] * 11,
        compiler_params=_cparams(("parallel", "parallel")),
        name="rwkv_prep",
    )(proj, proj, proj, *params)


def _gdn_prep_kernel(*refs, ctx_len, t, nh):
    ins, conv_ref, outs = refs[:9], refs[9], refs[10:]
    start = pl.program_id(1) * ins[0].shape[1]
    width = nh * GD_HEAD
    for part in range(3):
        x_ref, xp_ref, xn_ref = ins[3 * part:3 * part + 3]
        x = x_ref[0]
        x_prev, x_next = _neighbours(x, xp_ref, xn_ref, start, ctx_len, t)
        w = conv_ref[:, part * width:(part + 1) * width]
        y = x_prev * w[0:1, :] + x * w[1:2, :] + x_next * w[2:3, :]
        y = y * jax.nn.sigmoid(y)
        if part == 2:
            outs[part][0] = y
            continue
        scale = GD_HEAD ** -0.5 if part == 0 else 1.0
        for h in range(nh):
            seg = y[:, h * GD_HEAD:(h + 1) * GD_HEAD]
            n = seg * lax.rsqrt(jnp.sum(seg * seg, axis=-1, keepdims=True) + 1e-6)
            outs[part][0, :, h * GD_HEAD:(h + 1) * GD_HEAD] = n * scale


def _gdn_prep(proj, col0, gd_conv, nh, ctx_len):
    nb, t, _ = proj.shape
    width = nh * GD_HEAD
    tm = PREP_TM
    assert gd_conv.shape[0] == 3 and ctx_len % tm == 0 and t % tm == 0
    in_specs, args = [], []
    for part in range(3):
        prev, nxt = _halo_specs(tm, width, col0 + part, t)
        in_specs += [pl.BlockSpec((1, tm, width), lambda b, i, c=col0 + part: (b, i, c)), prev, nxt]
        args += [proj, proj, proj]
    out_spec = pl.BlockSpec((1, tm, width), lambda b, i: (b, i, 0))
    return pl.pallas_call(
        functools.partial(_gdn_prep_kernel, ctx_len=ctx_len, t=t, nh=nh),
        grid=(nb, t // tm),
        in_specs=in_specs + [pl.BlockSpec(gd_conv.shape, lambda b, i: (0, 0))],
        out_specs=[out_spec] * 3,
        out_shape=[jax.ShapeDtypeStruct((nb, t, width), F32)] * 3,
        compiler_params=_cparams(("parallel", "parallel")),
        name="gdn_prep",
    )(*args, gd_conv)


def _attn_prep_kernel(q_ref, k_ref, v_ref, cos_ref, sin_ref, qg_ref, kg_ref, q_o, k_o, v_o):
    cos, sin = cos_ref[...], sin_ref[...]
    lane = lax.broadcasted_iota(jnp.int32, cos.shape, 1)
    low_half = (lane & (AT_HEAD // 2 - 1)) < AT_HEAD // 4

    def prep(seg, gain):
        n = seg * lax.rsqrt(jnp.mean(seg * seg, axis=-1, keepdims=True) + NORM_EPS) * gain
        partner = jnp.where(low_half, pltpu.roll(n, AT_HEAD - AT_HEAD // 4, 1), pltpu.roll(n, AT_HEAD // 4, 1))
        return n * cos + partner * sin

    for h in range(q_ref.shape[2] // AT_HEAD):
        sl = slice(h * AT_HEAD, (h + 1) * AT_HEAD)
        q_o[0, :, sl] = (prep(q_ref[0, :, sl], qg_ref[...]) * AT_HEAD ** -0.5).astype(BF16)
    for h in range(k_ref.shape[2] // AT_HEAD):
        sl = slice(h * AT_HEAD, (h + 1) * AT_HEAD)
        k_o[0, :, sl] = prep(k_ref[0, :, sl], kg_ref[...]).astype(BF16)
    v_o[0] = v_ref[0].astype(BF16)


def _attn_prep(proj, col_q, qw, kvw, at_q_norm, at_k_norm, rope):
    nb, t, _ = proj.shape
    tm = PREP_TM
    cos, sin = rope
    col_k = col_q * qw // kvw + qw // kvw
    blk = lambda w, c: pl.BlockSpec((1, tm, w), lambda b, i: (b, i, c))
    tab = pl.BlockSpec((tm, AT_HEAD), lambda b, i: (i, 0))
    vec = pl.BlockSpec((1, AT_HEAD), lambda b, i: (0, 0))
    out = lambda w: pl.BlockSpec((1, tm, w), lambda b, i: (b, i, 0))
    return pl.pallas_call(
        _attn_prep_kernel,
        grid=(nb, t // tm),
        in_specs=[blk(qw, col_q), blk(kvw, col_k), blk(kvw, col_k + 1), tab, tab, vec, vec],
        out_specs=[out(qw), out(kvw), out(kvw)],
        out_shape=[jax.ShapeDtypeStruct((nb, t, qw), BF16), jax.ShapeDtypeStruct((nb, t, kvw), BF16),
                   jax.ShapeDtypeStruct((nb, t, kvw), BF16)],
        compiler_params=_cparams(("parallel", "parallel")),
        name="attn_prep",
    )(proj, proj, proj, cos, sin, at_q_norm.reshape(1, AT_HEAD), at_k_norm.reshape(1, AT_HEAD))


def _mix_post_kernel(ryf, ryb, rbonus, rg, myf, myb, mo, gyf, gyb, gg, ay, lnw, lnb, mlg, gdg, hs_ref, o_ref, *, gw):
    y = ryf[0] + ryb[0]
    hs = hs_ref[...]
    inv_n = 1.0 / RW_HEAD
    mean = _mm(y, hs, "nn", 3, 1) * inv_n
    cen = y - mean
    var = _mm(cen * cen, hs, "nn", 3, 1) * inv_n
    o_ref[0, :, 0:gw] = ((cen * lax.rsqrt(var + RW_GN_EPS) * lnw[...] + lnb[...] + rbonus[0]) * rg[0]).astype(BF16)
    y = myf[0] + myb[0]
    for h in range(gw // ML_HEAD):
        sl = slice(h * ML_HEAD, (h + 1) * ML_HEAD)
        seg = y[:, sl]
        cen = seg - jnp.mean(seg, axis=-1, keepdims=True)
        var = jnp.mean(cen * cen, axis=-1, keepdims=True)
        out = cen * lax.rsqrt(var + NORM_EPS) * mlg[:, sl] * jax.nn.sigmoid(mo[0, :, sl])
        o_ref[0, :, gw + h * ML_HEAD:gw + (h + 1) * ML_HEAD] = out.astype(BF16)
    y = gyf[0] + gyb[0]
    for h in range(gw // GD_HEAD):
        sl = slice(h * GD_HEAD, (h + 1) * GD_HEAD)
        seg = y[:, sl]
        gate = gg[0, :, sl]
        out = seg * lax.rsqrt(jnp.mean(seg * seg, axis=-1, keepdims=True) + NORM_EPS) * gdg[...] * (gate * jax.nn.sigmoid(gate))
        o_ref[0, :, 2 * gw + h * GD_HEAD:2 * gw + (h + 1) * GD_HEAD] = out.astype(BF16)
    o_ref[0, :, 3 * gw:4 * gw] = ay[0].astype(BF16)


def _mix_post(rw, ml, gd, at, proj, col_o, col_g, params, hs):
    nb, t, gw = at.shape
    tm = PREP_TM
    row = lambda c=0: pl.BlockSpec((1, tm, gw), lambda b, i: (b, i, c))
    full = lambda a: pl.BlockSpec(a.shape, lambda b, i: (0,) * a.ndim)
    ln_w, ln_b, ml_g, gd_g = params
    consts = [ln_w.reshape(1, gw), ln_b.reshape(1, gw), ml_g.reshape(1, gw), gd_g.reshape(1, GD_HEAD), hs]
    return pl.pallas_call(
        functools.partial(_mix_post_kernel, gw=gw),
        grid=(nb, t // tm),
        in_specs=[row()] * 4 + [row(), row(), row(col_o)] + [row(), row(), row(col_g)] + [row()]
                 + [full(a) for a in consts],
        out_specs=pl.BlockSpec((1, tm, 4 * gw), lambda b, i: (b, i, 0)),
        out_shape=jax.ShapeDtypeStruct((nb, t, 4 * gw), BF16),
        compiler_params=_cparams(("parallel", "parallel")),
        name="mix_post",
    )(*rw, ml[0], ml[1], proj, gd[0], gd[1], proj, at, *consts)


MOD_TN = 1024
MOD_KC = 256


def _mod_kernel(s_ref, w_ref, b_ref, o_ref, *, ncond):
    d = w_ref.shape[1]
    acc = [jnp.zeros((1, w_ref.shape[2]), F32) for _ in range(ncond)]
    for kc in range(d // MOD_KC):
        rows = slice(kc * MOD_KC, (kc + 1) * MOD_KC)
        w = w_ref[0, rows, :]
        for j in range(ncond):
            acc[j] = acc[j] + jnp.sum(w * s_ref[rows, j:j + 1], axis=0, keepdims=True)
    bias = b_ref[0]
    pad = [jnp.zeros_like(bias)] * (o_ref.shape[1] - ncond)
    o_ref[0] = jnp.concatenate([a + bias for a in acc] + pad, axis=0)


def _modulation(conds, ada_w, ada_b):
    depth, d, n = ada_w.shape
    ncond = conds.shape[0]
    s_cols = jnp.pad(jax.nn.silu(conds).T, ((0, 0), (0, 8 - ncond)))
    return pl.pallas_call(
        functools.partial(_mod_kernel, ncond=ncond),
        grid=(depth, n // MOD_TN),
        in_specs=[pl.BlockSpec((d, 8), lambda l, j: (0, 0)),
                  pl.BlockSpec((1, d, MOD_TN), lambda l, j: (l, 0, j)),
                  pl.BlockSpec((1, 1, MOD_TN), lambda l, j: (l, 0, j))],
        out_specs=pl.BlockSpec((1, 8, MOD_TN), lambda l, j: (l, 0, j)),
        out_shape=jax.ShapeDtypeStruct((depth, 8, n), F32),
        compiler_params=_cparams(("parallel", "parallel")),
        name="adaln_modulation",
    )(s_cols, ada_w, ada_b.reshape(depth, 1, n))


PROJ_TM = 544
PROJ_TN = 1792


def _modulated_norm(x, gain, mod_ref, shift_row, row0, ctx_len):
    y = x * lax.rsqrt(jnp.mean(x * x, axis=-1, keepdims=True) + NORM_EPS) * gain
    rows = row0 + lax.broadcasted_iota(jnp.int32, (x.shape[0], 1), 0)
    is_ctx = rows < ctx_len
    shift = jnp.where(is_ctx, mod_ref[0, 0, shift_row:shift_row + 1, :], mod_ref[0, 1, shift_row:shift_row + 1, :])
    scale = jnp.where(is_ctx, mod_ref[0, 0, shift_row + 1:shift_row + 2, :],
                      mod_ref[0, 1, shift_row + 1:shift_row + 2, :])
    return y * (1.0 + scale) + shift


def _norm_proj_kernel(x_ref, g_ref, mod_ref, w_ref, o_ref, h_scr, *, tiles_per_seq, ctx_len):
    row0 = (pl.program_id(0) % tiles_per_seq) * x_ref.shape[0]

    @pl.when(pl.program_id(1) == 0)
    def _():
        h_scr[...] = _modulated_norm(x_ref[...], g_ref[...], mod_ref, 0, row0, ctx_len).astype(BF16)

    o_ref[...] = jnp.dot(h_scr[...], w_ref[...], preferred_element_type=F32)


def _norm_proj(x, gain, mod, w, ctx_len):
    nb, t, d = x.shape
    n = w.shape[1]
    tm, tn = _pick_tile(t, PROJ_TM), PROJ_TN
    assert n % tn == 0
    tps = t // tm
    out = pl.pallas_call(
        functools.partial(_norm_proj_kernel, tiles_per_seq=tps, ctx_len=ctx_len),
        grid=(nb * tps, n // tn),
        in_specs=[pl.BlockSpec((tm, d), lambda i, j: (i, 0)),
                  pl.BlockSpec((1, d), lambda i, j: (0, 0)),
                  pl.BlockSpec((1, 2, 8, d), lambda i, j: (i // tps, 0, 0, 0)),
                  pl.BlockSpec((d, tn), lambda i, j: (0, j))],
        out_specs=pl.BlockSpec((tm, tn), lambda i, j: (i, j)),
        out_shape=jax.ShapeDtypeStruct((nb * t, n), F32),
        scratch_shapes=[pltpu.VMEM((tm, d), BF16)],
        compiler_params=_cparams(("parallel", "arbitrary")),
        name="norm_in_proj",
    )(x.reshape(nb * t, d), gain.reshape(1, d), mod, w)
    return out.reshape(nb, t, n)


OUT_TM = 272


def _out_proj_kernel(y_ref, w_ref, x_ref, g_ref, mod_ref, wr_hi, wr_lo,
                     xo_ref, h_ref, aff_ref, *, tiles_per_seq, ctx_len, n_experts):
    acc = jnp.dot(y_ref[...], w_ref[...], preferred_element_type=F32)
    tm = x_ref.shape[0]
    row0 = (pl.program_id(0) % tiles_per_seq) * tm
    rows = row0 + lax.broadcasted_iota(jnp.int32, (tm, 1), 0)
    gate = jnp.where(rows < ctx_len, mod_ref[0, 0, 2:3, :], mod_ref[0, 1, 2:3, :])
    x_new = x_ref[...] + gate * acc
    xo_ref[...] = x_new
    h = _modulated_norm(x_new, g_ref[...], mod_ref, 3, row0, ctx_len)
    h_ref[...] = h.astype(BF16)
    h_hi = h.astype(BF16)
    h_lo = (h - h_hi.astype(F32)).astype(BF16)
    logits = (jnp.dot(h_hi, wr_hi[...], preferred_element_type=F32)
              + jnp.dot(h_lo, wr_hi[...], preferred_element_type=F32)
              + jnp.dot(h_hi, wr_lo[...], preferred_element_type=F32))
    lane = lax.broadcasted_iota(jnp.int32, logits.shape, 1)
    logits = jnp.where(lane < n_experts, logits, -jnp.inf)
    p = jnp.exp(logits - jnp.max(logits, axis=-1, keepdims=True))
    aff_ref[...] = p / jnp.sum(p, axis=-1, keepdims=True)


def _out_proj(ymix, w_out, x, gain, mod, w_router, ctx_len):
    nb, t, d = x.shape
    mw = ymix.shape[-1]
    ne = w_router.shape[1]
    tm = _pick_tile(t, OUT_TM)
    tps = t // tm
    wr =jnp.pad(w_router, ((0, 0), (0, LANES - ne)))
    wr_hi = wr.astype(BF16)
    wr_lo = (wr - wr_hi.astype(F32)).astype(BF16)
    row = lambda w: pl.BlockSpec((tm, w), lambda i: (i, 0))
    full = lambda a: pl.BlockSpec(a.shape, lambda i: (0,) * a.ndim)
    xo, h, aff = pl.pallas_call(
        functools.partial(_out_proj_kernel, tiles_per_seq=tps, ctx_len=ctx_len, n_experts=ne),
        grid=(nb * tps,),
        in_specs=[row(mw), full(w_out), row(d), pl.BlockSpec((1, d), lambda i: (0, 0)),
                  pl.BlockSpec((1, 2, 8, d), lambda i: (i // tps, 0, 0, 0)), full(wr_hi), full(wr_lo)],
        out_specs=[row(d), row(d), row(LANES)],
        out_shape=[jax.ShapeDtypeStruct((nb * t, d), F32), jax.ShapeDtypeStruct((nb * t, d), BF16),
                   jax.ShapeDtypeStruct((nb * t, LANES), F32)],
        compiler_params=_cparams(("parallel",)),
        name="out_proj_norm_router",
    )(ymix.reshape(nb * t, mw), w_out, x.reshape(nb * t, d), gain.reshape(1, d), mod, wr_hi, wr_lo)
    return xo.reshape(nb, t, d), h.reshape(nb, t, d), aff.reshape(nb, t, LANES)[..., :ne]


FFN_TF = 256


def _ffn_kernel(x_ref, w1_ref, w3_ref, w2_ref, o_ref):
    @pl.when(pl.program_id(1) == 0)
    def _():
        o_ref[...] = jnp.zeros_like(o_ref)

    xs = x_ref[0]
    a = jnp.dot(xs, w1_ref[0].astype(BF16), preferred_element_type=F32)
    b = jnp.dot(xs, w3_ref[0].astype(BF16), preferred_element_type=F32)
    hid = (a * jax.nn.sigmoid(a) * b).astype(BF16)
    o_ref[0] += jnp.dot(hid, w2_ref[0].astype(BF16), preferred_element_type=F32)


def _expert_ffn(xs, w1, w3, w2):
    ne, r, d = xs.shape
    f = w1.shape[2]
    tf = FFN_TF
    return pl.pallas_call(
        _ffn_kernel,
        grid=(ne, f // tf),
        in_specs=[pl.BlockSpec((1, r, d), lambda e, j: (e, 0, 0)),
                  pl.BlockSpec((1, d, tf), lambda e, j: (e, 0, j)),
                  pl.BlockSpec((1, d, tf), lambda e, j: (e, 0, j)),
                  pl.BlockSpec((1, tf, d), lambda e, j: (e, j, 0))],
        out_specs=pl.BlockSpec((1, r, d), lambda e, j: (e, 0, 0)),
        out_shape=jax.ShapeDtypeStruct((ne, r, d), F32),
        compiler_params=_cparams(("parallel", "arbitrary")),
        name="expert_ffn",
    )(xs, w1, w3, w2)


def _moe(h, aff, segments, w1, w3, w2):
    nb, t, d = h.shape
    ne = aff.shape[-1]
    picks, xs = [], []
    for start, length in segments:
        cap = EC_FACTOR * length // ne
        gate, idx = lax.top_k(jnp.swapaxes(aff[:, start:start + length], 1, 2), cap)
        idx = idx + start
        picks.append((gate, idx))
        rows = jnp.take_along_axis(h[:, None], idx[..., None], axis=2)
        xs.append(jnp.swapaxes(rows, 0, 1).reshape(ne, nb * cap, d))
    y = _expert_ffn(jnp.concatenate(xs, axis=1), w1, w3, w2)
    out = jnp.zeros((nb, t, d), F32)
    bidx = jnp.arange(nb)[:, None, None]
    off = 0
    for (start, length), (gate, idx) in zip(segments, picks):
        cap = gate.shape[-1]
        ys = jnp.swapaxes(y[:, off:off + nb * cap].reshape(ne, nb, cap, d), 0, 1)
        out = out.at[bidx, idx].add(ys * gate[..., None])
        off += nb * cap
    return out


def _relayout_w_in(w_in, gw, rw_cols, ngate):
    ml0 = rw_cols
    gd0 = ml0 + 4 * gw + ngate
    at0 = gd0 + 4 * gw + ngate
    assert rw_cols % LANES == 0 and rw_cols + 2 * ngate <= 4 * gw
    pad = jnp.zeros((w_in.shape[0], 4 * gw - rw_cols - 2 * ngate), w_in.dtype)
    parts = [w_in[:, :rw_cols], w_in[:, ml0 + 4 * gw:gd0], w_in[:, gd0 + 4 * gw:at0], pad,
             w_in[:, ml0:ml0 + 4 * gw], w_in[:, gd0:gd0 + 4 * gw], w_in[:, at0:]]
    return jnp.concatenate(parts, axis=1).astype(BF16)


def kernel(x, c, ctx, c_ctx, ada_w, ada_b, norm1_g, norm2_g, w_in, w_out, rw_mu, rw_w0, rw_w_up, rw_a0, rw_a_up,
           rw_g_up, rw_k_k, rw_k_a, rw_r_k, rw_ln_w, rw_ln_b, ml_ib, ml_fb, ml_norm_g, gd_conv, gd_a_log,
           gd_dt_bias, gd_norm_g, at_q_norm, at_k_norm, w_router, w_exp1, w_exp3, w_exp2, final_g):
    nb, n_lat, d = x.shape
    ctx_len = ctx.shape[1]
    t = ctx_len + n_lat
    depth = ada_w.shape[0]
    gw = d // N_GROUPS
    rw_cols = rw_mu.shape[-1]
    ngate = 4 * ml_ib.shape[-1]
    at_cols = w_in.shape[-1] - rw_cols - 2 * (4 * gw + ngate)

    xs = jnp.concatenate([ctx, x], axis=1)
    mods = _modulation(jnp.concatenate([c, c_ctx[None]], axis=0), ada_w, ada_b)
    rope = _rope_tables(t, ctx_len)
    nh = ml_ib.shape[-1]
    assert 4 * gw == 2048 and gw == nh * ML_HEAD == nh * GD_HEAD and at_cols == 2 * gw
    head_of = jnp.arange(gw) // RW_HEAD
    hs = (head_of[:, None] == head_of[None, :]).astype(BF16)

    for layer in range(depth):
        last = layer == depth - 1
        m = mods[layer, :nb + 1].reshape(nb + 1, 6, d)
        m = jnp.pad(m, ((0, 0), (0, 2), (0, 0)))
        mod = jnp.stack([jnp.broadcast_to(m[nb], (nb, 8, d)), m[:nb]], axis=1)
        proj = _norm_proj(xs, norm1_g[layer], mod, _relayout_w_in(w_in[layer], gw, rw_cols, ngate), ctx_len)
        p_rw = {k: v[layer] for k, v in dict(
            rw_mu=rw_mu, rw_w0=rw_w0, rw_w_up=rw_w_up, rw_a0=rw_a0, rw_a_up=rw_a_up, rw_g_up=rw_g_up,
            rw_k_k=rw_k_k, rw_k_a=rw_k_a, rw_r_k=rw_r_k).items()}
        graw = proj[..., rw_cols:rw_cols + 2 * ngate]
        r, v, kk, lwf, bf, ktf, lwb, bb, ktb, g_rw, bonus = _rwkv_prep(proj, 0, p_rw, hs, ctx_len)
        ryf, ryb = _rwkv_scan(r, v, kk, [(lwf, bf, ktf), (lwb, bb, ktb)], ctx_len)

        gm = graw[..., :ngate]
        li = gm[..., :2 * nh] + ml_ib[layer].reshape(-1)
        lf = jax.nn.log_sigmoid(gm[..., 2 * nh:] + ml_fb[layer].reshape(-1))
        g_ml = jnp.concatenate([li, lf], axis=-1)
        myf, myb = _mlstm_scan(proj, proj, proj, (4, 5, 6), g_ml, jnp.swapaxes(g_ml, 1, 2), ctx_len)

        gg = graw[..., ngate:]
        lg = -jnp.exp(gd_a_log[layer]).reshape(-1) * jax.nn.softplus(gg[..., :2 * nh] + gd_dt_bias[layer].reshape(-1))
        q_gd, k_gd, v_gd = _gdn_prep(proj, 8, gd_conv[layer], nh, ctx_len)
        gyf, gyb = _gdn_scan(q_gd, k_gd, v_gd, jnp.concatenate([lg, jax.nn.sigmoid(gg[..., 2 * nh:])], axis=-1), ctx_len)

        kvw = AT_KV_HEADS * AT_HEAD
        q_at, k_at, v_at = _attn_prep(proj, 12, at_cols - 2 * kvw, kvw, at_q_norm[layer], at_k_norm[layer], rope)
        ay = _attention(q_at, k_at, v_at, ctx_len)

        ymix = _mix_post((ryf, ryb, bonus, g_rw), (myf, myb), (gyf, gyb), ay, proj, 7, 11,
                         (rw_ln_w[layer], rw_ln_b[layer], ml_norm_g[layer], gd_norm_g[layer]), hs)
        xs, h2, aff = _out_proj(ymix, w_out[layer].astype(BF16), xs, norm2_g[layer], mod, w_router[layer], ctx_len)
        segments = [(ctx_len, n_lat)] if last else [(0, ctx_len), (ctx_len, n_lat)]
        moe = _moe(h2, aff, segments, w_exp1[layer], w_exp3[layer], w_exp2[layer])
        rows = jnp.arange(t)[None, :, None]
        gate2 = jnp.where(rows < ctx_len, mod[:, 0, 5][:, None, :], mod[:, 1, 5][:, None, :])
        xs = xs + gate2 * moe

    xl = xs[:, ctx_len:]
    return xl * lax.rsqrt(jnp.mean(xl * xl, axis=-1, keepdims=True) + NORM_EPS) * final_g
```

```python
import functools
import math

import jax
import jax.numpy as jnp
from jax import lax
from jax.experimental import pallas as pl
from jax.experimental.pallas import tpu as pltpu

F32 = jnp.float32
BF16 = jnp.bfloat16

NORM_EPS = 1e-6
GRID_W = 64
N_GROUPS = 4
RW_HEAD = 64
RW_DECAY_RANK = 64
RW_ICLR_RANK = 64
RW_GATE_RANK = 128
RW_DECAY_SCALE = math.exp(-0.5)
RW_GN_EPS = 64e-5
RW_CHUNK = 64
ML_HEAD = 128
ML_CHUNK = 256
GD_HEAD = 128
GD_CHUNK = 64
AT_HEAD = 128
AT_KV_HEADS = 2
ROPE_THETA = 10000.0
AT_TQ = 256
N_EXPERTS = 16
EC_FACTOR = 2

VMEM_LIMIT = 56 * 1024 * 1024
LANES = 128


def _cparams(sem):
    return pltpu.CompilerParams(dimension_semantics=sem, vmem_limit_bytes=VMEM_LIMIT)


def _pick_tile(n, limit, mult=16):
    best = None
    for cand in range(mult, min(n, limit) + 1, mult):
        if n % cand == 0:
            best = cand
    assert best is not None, (n, limit, mult)
    return best


_DIMS = {
    "nn": (((1,), (0,)), ((), ())),
    "nt": (((1,), (1,)), ((), ())),
    "tn": (((0,), (0,)), ((), ())),
}


def _split(a, n):
    if a.dtype == BF16:
        return [a]
    pieces, rest = [], a
    for i in range(n):
        p = rest.astype(BF16)
        pieces.append(p)
        if i + 1 < n:
            rest = rest - p.astype(F32)
    return pieces


def _mm(a, b, dims="nn", pa=1, pb=1):
    ap, bp = _split(a, pa), _split(b, pb)
    top = max(len(ap), len(bp))
    acc = None
    for i, x in enumerate(ap):
        for j, y in enumerate(bp):
            if i + j < top:
                t = lax.dot_general(x, y, _DIMS[dims], preferred_element_type=F32)
                acc = t if acc is None else acc + t
    return acc


def _tri(n, rev, strict, reps=1):
    t = lax.broadcasted_iota(jnp.int32, (n, reps * n), 0)
    s = lax.broadcasted_iota(jnp.int32, (n, reps * n), 1)
    if reps > 1:
        assert n & (n - 1) == 0
        s = s & (n - 1)
    if rev:
        return (s > t) if strict else (s >= t)
    return (s < t) if strict else (s <= t)


def _chunk_maps(nsteps, nctx):
    fwd = lambda i: i
    bwd = lambda i: jnp.where(i < nctx, nctx - 1 - i, nsteps - 1 - i + nctx)
    return fwd, bwd


def _unit_lower_inverse(a, levels, passes):
    n = a.shape[0]
    eye = (lax.broadcasted_iota(jnp.int32, (n, n), 0) == lax.broadcasted_iota(jnp.int32, (n, n), 1)).astype(F32)
    y = -a
    r = eye + y
    for _ in range(levels):
        y = _mm(y, y, "nn", passes, passes)
        r = r + _mm(r, y, "nn", passes, passes)
    return r


def _mlstm_kernel(qf, kf, vf, gf, gtf, qb, kb, vb, gb, gtb, yf, yb, c_scr, m_scr, *, nb, nh, chunk):
    step = pl.program_id(0)

    @pl.when(step == 0)
    def _():
        c_scr[...] = jnp.zeros_like(c_scr)
        m_scr[...] = jnp.zeros_like(m_scr)

    scale = ML_HEAD ** -0.5
    lane = lax.broadcasted_iota(jnp.int32, (chunk, ML_HEAD), 1)
    ones_col = (lane == 0).astype(F32)
    units = []
    for d, (q_ref, k_ref, v_ref, g_ref, gt_ref, y_ref) in enumerate(
            ((qf, kf, vf, gf, gtf, yf), (qb, kb, vb, gb, gtb, yb))):
        rev = d == 1
        incl = _tri(chunk, rev, False)
        cum_m = incl.astype(BF16)
        last = 0 if rev else chunk - 1
        for b in range(nb):
            g = g_ref[b]
            gt = gt_ref[b]
            cum_c = _mm(cum_m, g, "nn", 1, 3)
            cum_r = _mm(gt, cum_m, "nt", 3, 1)
            for h in range(nh):
                ci, cf = d * nh + h, (2 + d) * nh + h
                sl = slice(h * ML_HEAD, (h + 1) * ML_HEAD)
                u = dict(y_ref=y_ref, b=b, sl=sl, idx=(d * nb + b) * nh + h, last=last)
                u["q"] = q_ref[b, :, sl].astype(BF16)
                u["k"] = k_ref[b, :, sl] * scale
                u["v_aug"] = jnp.concatenate([v_ref[b, :, sl], ones_col], axis=1).astype(BF16)
                u["m_old"] = m_scr[u["idx"]][0:1, 0:1]
                u["bc"] = cum_c[:, cf:cf + 1]
                u["li_c"] = g[:, ci:ci + 1]
                dlog = jnp.where(incl, u["bc"] - cum_r[cf:cf + 1, :] + gt[ci:ci + 1, :], -jnp.inf)
                inter = u["bc"] + u["m_old"]
                u["mt"] = jnp.maximum(inter, jnp.max(dlog, axis=1, keepdims=True))
                u["p"] = jnp.exp(dlog - u["mt"])
                u["wi"] = jnp.exp(inter - u["mt"])
                units.append(u)
    for u in units:
        u["s"] = (_mm(u["q"], u["k"], "nt") * u["p"]).astype(BF16)
    for u in units:
        u["c_old"] = c_scr[u["idx"]]
        u["acc"] = _mm(u["s"], u["v_aug"]) + u["wi"] * _mm(u["q"], u["c_old"])
    for u in units:
        num = u["acc"][:, :ML_HEAD]
        den = u["acc"][:, ML_HEAD:ML_HEAD + 1]
        u["y_ref"][u["b"], :, u["sl"]] = num / jnp.maximum(jnp.abs(den), jnp.exp(-u["mt"]))
    for u in units:
        last = u["last"]
        m_new = u["mt"][last:last + 1, :]
        b_last = u["bc"][last:last + 1, :]
        wk = jnp.exp(b_last - u["bc"] + u["li_c"] - m_new)
        dc = jnp.exp(b_last + u["m_old"] - m_new)
        c_scr[u["idx"]] = dc * u["c_old"] + _mm(u["k"] * wk, u["v_aug"], "tn")
        m_scr[u["idx"]] = jnp.broadcast_to(m_new, (8, LANES))


def _mlstm_scan(q_src, k_src, v_src, cols, g, gt, ctx_len):
    nb, t, _ = g.shape
    nh = g.shape[-1] // 4
    width = nh * ML_HEAD
    chunk = ML_CHUNK
    nsteps, nctx = t // chunk, ctx_len // chunk
    fwd, bwd = _chunk_maps(nsteps, nctx)

    def specs(cm):
        col = lambda c: pl.BlockSpec((nb, chunk, width), lambda i, c=c: (0, cm(i), c))
        return [col(cols[0]), col(cols[1]), col(cols[2]),
                pl.BlockSpec((nb, chunk, 4 * nh), lambda i: (0, cm(i), 0)),
                pl.BlockSpec((nb, 4 * nh, chunk), lambda i: (0, 0, cm(i)))]

    out_spec = lambda cm: pl.BlockSpec((nb, chunk, width), lambda i: (0, cm(i), 0))
    kern = functools.partial(_mlstm_kernel, nb=nb, nh=nh, chunk=chunk)
    return pl.pallas_call(
        kern,
        grid=(nsteps,),
        in_specs=specs(fwd) + specs(bwd),
        out_specs=[out_spec(fwd), out_spec(bwd)],
        out_shape=[jax.ShapeDtypeStruct((nb, t, width), F32)] * 2,
        scratch_shapes=[pltpu.VMEM((2 * nb * nh, ML_HEAD, 2 * ML_HEAD), F32),
                        pltpu.VMEM((2 * nb * nh, 8, LANES), F32)],
        compiler_params=_cparams(("arbitrary",)),
        name="mlstm_scan",
    )(q_src, k_src, v_src, g, gt, q_src, k_src, v_src, g, gt)


def _head_layer_norm(y, nh, eps):
    b, t, w = y.shape
    yh = y.reshape(b, t, nh, w // nh)
    mean = jnp.mean(yh, axis=-1, keepdims=True)
    var = jnp.mean(jnp.square(yh - mean), axis=-1, keepdims=True)
    return ((yh - mean) * lax.rsqrt(var + eps)).reshape(b, t, w)


def _mlstm_mixer(proj, col0, graw, ml_ib, ml_fb, ml_norm_g, ctx_len):
    nh = graw.shape[-1] // 4
    width = nh * ML_HEAD
    li = graw[..., :2 * nh] + ml_ib.reshape(-1)
    lf = jax.nn.log_sigmoid(graw[..., 2 * nh:] + ml_fb.reshape(-1))
    g = jnp.concatenate([li, lf], axis=-1)
    yf, yb = _mlstm_scan(proj, proj, proj, (col0, col0 + 1, col0 + 2), g, jnp.swapaxes(g, 1, 2), ctx_len)
    y = _head_layer_norm(yf + yb, nh, NORM_EPS) * ml_norm_g
    o = proj[..., (col0 + 3) * width:(col0 + 4) * width]
    return y * jax.nn.sigmoid(o)


def _gdn_kernel(qf, kf, vf, gf, gtf, qb, kb, vb, gb, gtb, yf, yb, s_scr, *, nb, nh, chunk):
    step = pl.program_id(0)

    @pl.when(step == 0)
    def _():
        s_scr[...] = jnp.zeros_like(s_scr)

    levels = int(math.log2(chunk)) - 1
    c = chunk
    dh = GD_HEAD
    pw = 2 * dh
    first = lax.broadcasted_iota(jnp.int32, (c, pw), 1) < dh
    first2 = lax.broadcasted_iota(jnp.int32, (c, 2 * c), 1) < c
    first2_row = lax.broadcasted_iota(jnp.int32, (1, 2 * c), 1) < c
    first4 = (lax.broadcasted_iota(jnp.int32, (c, 2 * pw), 1) // dh) % 2 == 0
    same_head = (lax.broadcasted_iota(jnp.int32, (pw, pw), 0) // dh
                 == lax.broadcasted_iota(jnp.int32, (pw, pw), 1) // dh)
    same_blk = (lax.broadcasted_iota(jnp.int32, (2 * c, 2 * c), 0) // c
                == lax.broadcasted_iota(jnp.int32, (2 * c, 2 * c), 1) // c)
    eye2 = (lax.broadcasted_iota(jnp.int32, (c, 2 * c), 0)
            == lax.broadcasted_iota(jnp.int32, (c, 2 * c), 1) % c).astype(F32)

    def by_head(x, mask):
        return jnp.concatenate([jnp.where(mask, x, 0.0), jnp.where(mask, 0.0, x)], axis=0)

    def blockdiag(x):
        xb = x.astype(BF16)
        return jnp.where(same_blk, jnp.concatenate([xb, xb], axis=0), jnp.zeros((), BF16))

    units = []
    for d, (q_ref, k_ref, v_ref, g_ref, gt_ref, y_ref) in enumerate(
            ((qf, kf, vf, gf, gtf, yf), (qb, kb, vb, gb, gtb, yb))):
        rev = d == 1
        incl = _tri(c, rev, False)
        incl2 = _tri(c, rev, False, reps=2)
        strict2 = _tri(c, rev, True, reps=2)
        cum_m = incl.astype(BF16)
        cum_m2 = jnp.concatenate([cum_m, cum_m], axis=0)
        last = 0 if rev else c - 1
        for b in range(nb):
            g = g_ref[b]
            gt = gt_ref[b, 0]
            cum_c = _mm(cum_m, g, "nn", 1, 3)
            cum_r = _mm(gt, cum_m2, "nt", 3, 1)
            for hp in range(nh // 2):
                h0, h1 = 2 * hp, 2 * hp + 1
                sl = slice(hp * pw, (hp + 1) * pw)
                u = dict(y_ref=y_ref, b=b, sl=sl, idx=(d * nb + b) * (nh // 2) + hp, strict2=strict2)
                q, k, v = q_ref[b, :, sl], k_ref[b, :, sl], v_ref[b, :, sl]
                col = lambda arr, j: jnp.where(first, arr[:, j + h0:j + h0 + 1], arr[:, j + h1:j + h1 + 1])
                gc = col(cum_c, d * nh)
                beta = col(g, (2 + d) * nh)
                gc_c2 = jnp.where(first2, cum_c[:, d * nh + h0:d * nh + h0 + 1], cum_c[:, d * nh + h1:d * nh + h1 + 1])
                gc_r2 = jnp.where(first2_row, cum_r[d * nh + h0:d * nh + h0 + 1], cum_r[d * nh + h1:d * nh + h1 + 1])
                u["decay"] = jnp.where(incl2, jnp.exp(jnp.where(incl2, gc_c2 - gc_r2, 0.0)), 0.0)
                kb_ = k * beta
                eg = jnp.exp(gc)
                u["kq"] = jnp.concatenate([kb_, q], axis=0).astype(BF16)
                u["k_blk"] = by_head(k, first).astype(BF16)
                u["rhs"] = by_head(jnp.concatenate([v * beta, kb_ * eg], axis=1), first4).astype(BF16)
                u["qg"] = q * eg
                g_last = gc[last:last + 1, :]
                u["kd"] = (k * jnp.exp(g_last - gc)).astype(BF16)
                u["keep"] = jnp.exp(g_last)
                units.append(u)

    for u in units:
        kq = _mm(u["kq"], u["k_blk"], "nt")
        u["y"] = jnp.where(u["strict2"], -kq[:c] * u["decay"], 0.0)
        u["attn"] = (kq[c:] * u["decay"]).astype(BF16)
        u["inv"] = eye2 + u["y"]
        u["bd"] = blockdiag(u["y"])
    for _ in range(levels):
        for u in units:
            u["y"] = _mm(u["y"], u["bd"])
            u["bd"] = blockdiag(u["y"])
        for u in units:
            u["inv"] = u["inv"] + _mm(u["inv"], u["bd"])
    for u in units:
        u["sol"] = _mm(u["inv"], u["rhs"])
    for u in units:
        u["s_old"] = s_scr[u["idx"]]
        u["ws"] = _mm(jnp.concatenate([u["sol"][:, pw:], u["qg"]], axis=0), u["s_old"])
    for u in units:
        u["v_new"] = u["sol"][:, :pw] - u["ws"][:c]
        u["y_ref"][u["b"], :, u["sl"]] = u["ws"][c:] + _mm(u["attn"], by_head(u["v_new"], first))
    for u in units:
        upd = _mm(u["kd"], u["v_new"], "tn")
        s_scr[u["idx"]] = u["s_old"] * u["keep"] + jnp.where(same_head, upd, 0.0)


def _gdn_scan(q, k, v, g, ctx_len):
    nb, t, width = q.shape
    nh = width // GD_HEAD
    chunk = GD_CHUNK
    nsteps, nctx = t // chunk, ctx_len // chunk
    fwd, bwd = _chunk_maps(nsteps, nctx)
    gt = jnp.swapaxes(g.reshape(nb, nsteps, chunk, 4 * nh), 2, 3)

    def specs(cm):
        col = pl.BlockSpec((nb, chunk, width), lambda i: (0, cm(i), 0))
        return [col, col, col,
                pl.BlockSpec((nb, chunk, 4 * nh), lambda i: (0, cm(i), 0)),
                pl.BlockSpec((nb, 1, 4 * nh, chunk), lambda i: (0, cm(i), 0, 0))]

    out_spec = lambda cm: pl.BlockSpec((nb, chunk, width), lambda i: (0, cm(i), 0))
    kern = functools.partial(_gdn_kernel, nb=nb, nh=nh, chunk=chunk)
    return pl.pallas_call(
        kern,
        grid=(nsteps,),
        in_specs=specs(fwd) + specs(bwd),
        out_specs=[out_spec(fwd), out_spec(bwd)],
        out_shape=[jax.ShapeDtypeStruct((nb, t, width), F32)] * 2,
        scratch_shapes=[pltpu.VMEM((nb * nh, 2 * GD_HEAD, 2 * GD_HEAD), F32)],
        compiler_params=_cparams(("arbitrary",)),
        name="gdn_scan",
    )(q, k, v, g, gt, q, k, v, g, gt)


def _seq_shift(x, ctx_len, delta):
    t = x.shape[1]
    rows = jnp.arange(t)
    if delta < 0:
        sh = jnp.pad(x[:, :delta], ((0, 0), (-delta, 0), (0, 0)))
        ok = (rows + delta >= 0) & ((rows < ctx_len) | (rows + delta >= ctx_len))
    else:
        sh = jnp.pad(x[:, delta:], ((0, 0), (0, delta), (0, 0)))
        ok = (rows + delta < t) & ((rows >= ctx_len) | (rows + delta < ctx_len))
    return jnp.where(ok[None, :, None], sh, 0.0)


def _l2n(x, nh, eps=1e-6):
    b, t, w = x.shape
    xh = x.reshape(b, t, nh, w // nh)
    return (xh * lax.rsqrt(jnp.sum(xh * xh, axis=-1, keepdims=True) + eps)).reshape(b, t, w)


def _gdn_mixer(qkv, gate, graw, gd_conv, gd_a_log, gd_dt_bias, gd_norm_g, ctx_len):
    nh = graw.shape[-1] // 4
    width = nh * GD_HEAD
    pad = gd_conv.shape[0] // 2
    conv = sum(_seq_shift(qkv, ctx_len, j - pad) * gd_conv[j] for j in range(gd_conv.shape[0]))
    qkv = jax.nn.silu(conv)
    q = _l2n(qkv[..., :width], nh) * GD_HEAD ** -0.5
    k = _l2n(qkv[..., width:2 * width], nh)
    v = qkv[..., 2 * width:]
    lg = -jnp.exp(gd_a_log).reshape(-1) * jax.nn.softplus(graw[..., :2 * nh] + gd_dt_bias.reshape(-1))
    beta = jax.nn.sigmoid(graw[..., 2 * nh:])
    yf, yb = _gdn_scan(q, k, v, jnp.concatenate([lg, beta], axis=-1), ctx_len)
    o = yf + yb
    b, t, _ = o.shape
    oh = o.reshape(b, t, nh, GD_HEAD)
    oh = oh * lax.rsqrt(jnp.mean(oh * oh, axis=-1, keepdims=True) + NORM_EPS) * gd_norm_g
    return oh.reshape(b, t, width) * jax.nn.silu(gate)


RW_PASSES = 1


def _rwkv_kernel(*refs, nb, npair, chunk):
    ins, (yf, yb, s_scr) = refs[:12], refs[12:]
    step = pl.program_id(0)

    @pl.when(step == 0)
    def _():
        s_scr[...] = jnp.zeros_like(s_scr)

    levels = int(math.log2(chunk)) - 1
    pw = 2 * RW_HEAD
    c = chunk
    head0 = lax.broadcasted_iota(jnp.int32, (c, pw), 1) < RW_HEAD
    same_head = (lax.broadcasted_iota(jnp.int32, (pw, pw), 0) // RW_HEAD
                 == lax.broadcasted_iota(jnp.int32, (pw, pw), 1) // RW_HEAD)
    same_blk = (lax.broadcasted_iota(jnp.int32, (2 * c, 2 * c), 0) // c
                == lax.broadcasted_iota(jnp.int32, (2 * c, 2 * c), 1) // c)
    eye2 = (lax.broadcasted_iota(jnp.int32, (c, 2 * c), 0)
            == lax.broadcasted_iota(jnp.int32, (c, 2 * c), 1) % c).astype(F32)
    mid = c // 2

    def by_head(x):
        return jnp.concatenate([jnp.where(head0, x, 0.0), jnp.where(head0, 0.0, x)], axis=0)

    def blockdiag(x):
        xb = x.astype(BF16)
        return jnp.where(same_blk, jnp.concatenate([xb, xb], axis=0), jnp.zeros((), BF16))

    units = []
    for d, y_ref in enumerate((yf, yb)):
        r_ref, lw_ref, kt_ref, v_ref, kk_ref, b_ref = ins[6 * d:6 * d + 6]
        rev = d == 1
        incl = _tri(c, rev, False)
        strict = _tri(c, rev, True)
        incl2 = _tri(c, rev, False, reps=2)
        strict2 = _tri(c, rev, True, reps=2)
        cum_m = incl.astype(BF16)
        last = 0 if rev else c - 1
        for b in range(nb):
            lw_all = lw_ref[b]
            g_all = _mm(cum_m, lw_all, "nn", 1, 3)
            for p in range(npair):
                sl = slice(p * pw, (p + 1) * pw)
                u = dict(y_ref=y_ref, b=b, sl=sl, idx=(d * nb + b) * npair + p, incl2=incl2, strict2=strict2)
                lw, g = lw_all[:, sl], g_all[:, sl]
                r, kt, v, kk, bb = (x[b, :, sl] for x in (r_ref, kt_ref, v_ref, kk_ref, b_ref))
                gh = g - g[mid:mid + 1, :]
                e_pos, e_neg = jnp.exp(gh), jnp.exp(-gh)
                u["lhs"] = jnp.concatenate([jnp.exp(gh - lw) * kk, r * e_pos], axis=0).astype(BF16)
                u["rhs"] = jnp.concatenate([by_head(bb * e_neg), by_head(kt * e_neg)], axis=0).astype(BF16)
                u["st_lhs"] = jnp.concatenate([jnp.exp(g - lw) * kk, jnp.exp(g) * r], axis=0).astype(BF16)
                g_last = g[last:last + 1, :]
                dl = jnp.exp(g_last - g)
                u["keep"] = jnp.exp(g_last)
                u["bk_d"] = jnp.concatenate([bb * dl, kt * dl], axis=0).astype(BF16)
                u["v"] = v
                u["v_blk"] = by_head(v).astype(BF16)
                units.append(u)

    for u in units:
        u["s_old"] = s_scr[u["idx"]]
        u["from_state"] = _mm(u["st_lhs"], u["s_old"], "nt")
    for u in units:
        gm = _mm(u["lhs"], u["rhs"], "nt")
        u["y"] = jnp.where(u["strict2"], -gm[:c, :2 * c], 0.0)
        u["a_ak"] = jnp.where(u["strict2"], gm[:c, 2 * c:], 0.0).astype(BF16)
        u["r_mat"] = jnp.concatenate([jnp.where(u["incl2"], -gm[c:, :2 * c], 0.0),
                                      jnp.where(u["incl2"], gm[c:, 2 * c:], 0.0)], axis=1).astype(BF16)
        u["inv"] = eye2 + u["y"]
        u["bd"] = blockdiag(u["y"])
    for u in units:
        u["rhs_e"] = u["from_state"][:c] + _mm(u["a_ak"], u["v_blk"])
    for _ in range(levels):
        for u in units:
            u["y"] = _mm(u["y"], u["bd"])
            u["bd"] = blockdiag(u["y"])
        for u in units:
            u["inv"] = u["inv"] + _mm(u["inv"], u["bd"])
    for u in units:
        u["e"] = _mm(u["inv"], by_head(u["rhs_e"]))
    for u in units:
        ev = jnp.concatenate([by_head(u["e"]), by_head(u["v"])], axis=0)
        u["y_ref"][u["b"], :, u["sl"]] = u["from_state"][c:] + _mm(u["r_mat"], ev)
    for u in units:
        upd = _mm(jnp.concatenate([-u["e"], u["v"]], axis=0), u["bk_d"], "tn")
        s_scr[u["idx"]] = u["s_old"] * u["keep"] + jnp.where(same_head, upd, 0.0)


def _rwkv_scan(r, v, kk, per_dir, ctx_len):
    nb, t, width = r.shape
    chunk = RW_CHUNK
    nsteps, nctx = t // chunk, ctx_len // chunk
    fwd, bwd = _chunk_maps(nsteps, nctx)
    spec = lambda cm: pl.BlockSpec((nb, chunk, width), lambda i: (0, cm(i), 0))
    args = []
    for d in range(2):
        lw, bb, kt = per_dir[d]
        args += [r, lw, kt, v, kk, bb]
    kern = functools.partial(_rwkv_kernel, nb=nb, npair=width // (2 * RW_HEAD), chunk=chunk)
    return pl.pallas_call(
        kern,
        grid=(nsteps,),
        in_specs=[spec(fwd)] * 6 + [spec(bwd)] * 6,
        out_specs=[spec(fwd), spec(bwd)],
        out_shape=[jax.ShapeDtypeStruct((nb, t, width), F32)] * 2,
        scratch_shapes=[pltpu.VMEM((2 * nb * (width // (2 * RW_HEAD)), 2 * RW_HEAD, 2 * RW_HEAD), F32)],
        compiler_params=_cparams(("arbitrary",)),
        name="rwkv_scan",
    )(*args)


def _rwkv_mixer(pa, p, ctx_len):
    width = p['rw_k_k'].shape[0]
    nh = width // RW_HEAD
    mu = p['rw_mu']
    pp = pa + mu[0] * (_seq_shift(pa, ctx_len, -1) - pa) + mu[1] * (_seq_shift(pa, ctx_len, 1) - pa)
    r, k, v = pp[..., :width], pp[..., width:2 * width], pp[..., 2 * width:3 * width]
    o = 3 * width
    wd = pp[..., o:o + RW_DECAY_RANK]
    ad = pp[..., o + RW_DECAY_RANK:o + RW_DECAY_RANK + RW_ICLR_RANK]
    gd = pp[..., o + RW_DECAY_RANK + RW_ICLR_RANK:]
    g = jax.nn.sigmoid(gd) @ p['rw_g_up']
    kk = _l2n(k * p['rw_k_k'], nh)
    per_dir, kts = [], []
    for d in range(2):
        logw = -RW_DECAY_SCALE * jax.nn.sigmoid(p['rw_w0'][d] + jnp.tanh(wd) @ p['rw_w_up'][d])
        a = jax.nn.sigmoid(p['rw_a0'][d] + ad @ p['rw_a_up'][d])
        kt = k * (1.0 + (a - 1.0) * p['rw_k_a'])
        per_dir.append((logw, kk * a, kt))
        kts.append(kt)
    yf, yb = _rwkv_scan(r, v, kk, per_dir, ctx_len)
    y = _head_layer_norm(yf + yb, nh, RW_GN_EPS) * p['rw_ln_w'] + p['rw_ln_b']
    b, t, _ = y.shape
    bh = lambda x: x.reshape(b, t, nh, RW_HEAD)
    bonus = sum(jnp.sum(bh(r) * bh(kt) * p['rw_r_k'], axis=-1, keepdims=True) * bh(v) for kt in kts)
    return (y + bonus.reshape(y.shape)) * g


def _attn_kernel(q_ref, k_ref, v_ref, o_ref, *, ctx_tiles, ctx_len, group):
    qi = pl.program_id(2)

    def run(keys, vals):
        for g in range(group):
            sl = slice(g * AT_HEAD, (g + 1) * AT_HEAD)
            s = lax.dot_general(q_ref[0, :, sl], keys, _DIMS["nt"], preferred_element_type=F32)
            p = jnp.exp(s - jnp.max(s, axis=-1, keepdims=True))
            o = jnp.dot(p.astype(BF16), vals, preferred_element_type=F32)
            o_ref[0, :, sl] = o / jnp.sum(p, axis=-1, keepdims=True)

    @pl.when(qi < ctx_tiles)
    def _():
        run(k_ref[0, :ctx_len, :], v_ref[0, :ctx_len, :])

    @pl.when(qi >= ctx_tiles)
    def _():
        run(k_ref[0], v_ref[0])


def _attention(q, k, v, ctx_len):
    nb, t, qw = q.shape
    hkv = k.shape[-1] // AT_HEAD
    group = qw // AT_HEAD // hkv
    tq = AT_TQ
    assert ctx_len % tq == 0 and t % tq == 0
    kern = functools.partial(_attn_kernel, ctx_tiles=ctx_len // tq, ctx_len=ctx_len, group=group)
    return pl.pallas_call(
        kern,
        grid=(nb, hkv, t // tq),
        in_specs=[pl.BlockSpec((1, tq, group * AT_HEAD), lambda b, h, i: (b, i, h)),
                  pl.BlockSpec((1, t, AT_HEAD), lambda b, h, i: (b, 0, h)),
                  pl.BlockSpec((1, t, AT_HEAD), lambda b, h, i: (b, 0, h))],
        out_specs=pl.BlockSpec((1, tq, group * AT_HEAD), lambda b, h, i: (b, i, h)),
        out_shape=jax.ShapeDtypeStruct((nb, t, qw), F32),
        compiler_params=_cparams(("parallel", "parallel", "arbitrary")),
        name="gqa",
    )(q, k, v)


def _rope_tables(t, ctx_len):
    n_lat = t - ctx_len
    pos = jnp.arange(n_lat)
    row = (pos // GRID_W).astype(F32)
    col = (pos % GRID_W).astype(F32)
    axis_dim = AT_HEAD // 2
    inv_freq = ROPE_THETA ** (-jnp.arange(0, axis_dim, 2, dtype=F32) / axis_dim)
    ang_r = row[:, None] * inv_freq[None, :]
    ang_c = col[:, None] * inv_freq[None, :]
    cos = jnp.concatenate([jnp.cos(ang_r)] * 2 + [jnp.cos(ang_c)] * 2, axis=-1)
    sin = jnp.concatenate([-jnp.sin(ang_r), jnp.sin(ang_r), -jnp.sin(ang_c), jnp.sin(ang_c)], axis=-1)
    cos = jnp.concatenate([jnp.ones((ctx_len, AT_HEAD), F32), cos], axis=0)
    sin = jnp.concatenate([jnp.zeros((ctx_len, AT_HEAD), F32), sin], axis=0)
    return cos, sin


def _attn_mixer(pd, at_q_norm, at_k_norm, ctx_len, rope):
    nb, t, w = pd.shape
    kvw = AT_KV_HEADS * AT_HEAD
    qw = w - 2 * kvw
    cos, sin = rope

    def prep(x, g):
        xh = x.reshape(nb, t, -1, AT_HEAD)
        xh = xh * lax.rsqrt(jnp.mean(xh * xh, axis=-1, keepdims=True) + NORM_EPS) * g
        sw = jnp.flip(xh.reshape(nb, t, -1, 2, 2, AT_HEAD // 4), axis=-2).reshape(xh.shape)
        return (xh * cos[:, None, :] + sw * sin[:, None, :]).reshape(x.shape)

    q = prep(pd[..., :qw], at_q_norm) * AT_HEAD ** -0.5
    k = prep(pd[..., qw:qw + kvw], at_k_norm)
    v = pd[..., qw + kvw:]
    return _attention(q.astype(BF16), k.astype(BF16), v.astype(BF16), ctx_len)


PREP_TM = 256
HALO = 8


def _halo_specs(tm, width, col, t):
    per = tm // HALO
    last = t // HALO - 1
    prev = pl.BlockSpec((1, HALO, width), lambda b, i: (b, jnp.maximum(i * per - 1, 0), col))
    nxt = pl.BlockSpec((1, HALO, width), lambda b, i: (b, jnp.minimum((i + 1) * per, last), col))
    return prev, nxt


def _neighbours(x, prev_ref, next_ref, start, ctx_len, t):
    tm = x.shape[0]
    row = lax.broadcasted_iota(jnp.int32, x.shape, 0)
    has_prev = jnp.logical_and(start != 0, start != ctx_len)
    has_next = jnp.logical_and(start + tm != ctx_len, start + tm != t)
    prev_row = jnp.where(has_prev, prev_ref[0, HALO - 1:HALO, :], 0.0)
    next_row = jnp.where(has_next, next_ref[0, 0:1, :], 0.0)
    x_prev = jnp.where(row == 0, prev_row, pltpu.roll(x, 1, 0))
    x_next = jnp.where(row == tm - 1, next_row, pltpu.roll(x, tm - 1, 0))
    return x_prev, x_next


def _rwkv_prep_kernel(x_ref, xp_ref, xn_ref, mu_ref, wa_ref, gup_ref, w0_ref, a0_ref, kk_ref, ka_ref, rk_ref, hs_ref,
                      r_o, v_o, kkn_o, lwf_o, bf_o, ktf_o, lwb_o, bb_o, ktb_o, g_o, bonus_o, *, ctx_len, t, width):
    start = pl.program_id(1) * x_ref.shape[1]
    x = x_ref[0]
    x_prev, x_next = _neighbours(x, xp_ref, xn_ref, start, ctx_len, t)
    pp = x + mu_ref[0:1, :] * (x_prev - x) + mu_ref[1:2, :] * (x_next - x)
    w = width
    r, k, v = pp[:, :w], pp[:, w:2 * w], pp[:, 2 * w:3 * w]
    low = pp[:, 3 * w:3 * w + LANES]
    lane = lax.broadcasted_iota(jnp.int32, low.shape, 1)
    low = jnp.where(lane < RW_DECAY_RANK, jnp.tanh(low), low)
    g_o[0] = _mm(jax.nn.sigmoid(pp[:, 3 * w + LANES:]), gup_ref[...])
    hs = hs_ref[...]
    kx = k * kk_ref[...]
    kk = kx * lax.rsqrt(_mm(kx * kx, hs, "nn", 3, 1) + 1e-6)
    r_o[0], v_o[0], kkn_o[0] = r, v, kk
    kt_sum = None
    for d, (lw_o, b_o, kt_o) in enumerate(((lwf_o, bf_o, ktf_o), (lwb_o, bb_o, ktb_o))):
        up = _mm(low, wa_ref[d])
        lw_o[0] = -RW_DECAY_SCALE * jax.nn.sigmoid(w0_ref[d:d + 1, :] + up[:, :w])
        a = jax.nn.sigmoid(a0_ref[d:d + 1, :] + up[:, w:])
        kt = k * (1.0 + (a - 1.0) * ka_ref[...])
        b_o[0] = kk * a
        kt_o[0] = kt
        kt_sum = kt if kt_sum is None else kt_sum + kt
    bonus_o[0] = _mm(r * kt_sum * rk_ref[...], hs, "nn", 3, 1) * v


def _rwkv_prep(proj, col, p, hs, ctx_len):
    nb, t, _ = proj.shape
    width = p['rw_k_k'].shape[0]
    cols = p['rw_mu'].shape[-1]
    tm = PREP_TM
    assert ctx_len % tm == 0 and t % tm == 0 and cols == 3 * width + LANES + RW_GATE_RANK
    rank = RW_DECAY_RANK
    wa = jnp.zeros((2, LANES, 2 * width), F32)
    wa = wa.at[:, :rank, :width].set(p['rw_w_up']).at[:, rank:, width:].set(p['rw_a_up'])
    params = [p['rw_mu'], wa, p['rw_g_up'], p['rw_w0'], p['rw_a0'], p['rw_k_k'].reshape(1, width),
              p['rw_k_a'].reshape(1, width), p['rw_r_k'].reshape(1, width), hs]
    full = lambda a: pl.BlockSpec(a.shape, lambda b, i: (0,) * a.ndim)
    prev, nxt = _halo_specs(tm, cols, col, t)
    out_spec = pl.BlockSpec((1, tm, width), lambda b, i: (b, i, 0))
    return pl.pallas_call(
        functools.partial(_rwkv_prep_kernel, ctx_len=ctx_len, t=t, width=width),
        grid=(nb, t // tm),
        in_specs=[pl.BlockSpec((1, tm, cols), lambda b, i: (b, i, col)), prev, nxt] + [full(a) for a in params],
        out_specs=[out_spec] * 11,
        out_shape=[jax.ShapeDtypeStruct((nb, t, width), F32)] * 11,
        compiler_params=_cparams(("parallel", "parallel")),
        name="rwkv_prep",
    )(proj, proj, proj, *params)


def _gdn_prep_kernel(*refs, ctx_len, t, nh):
    ins, conv_ref, outs = refs[:9], refs[9], refs[10:]
    start = pl.program_id(1) * ins[0].shape[1]
    width = nh * GD_HEAD
    for part in range(3):
        x_ref, xp_ref, xn_ref = ins[3 * part:3 * part + 3]
        x = x_ref[0]
        x_prev, x_next = _neighbours(x, xp_ref, xn_ref, start, ctx_len, t)
        w = conv_ref[:, part * width:(part + 1) * width]
        y = x_prev * w[0:1, :] + x * w[1:2, :] + x_next * w[2:3, :]
        y = y * jax.nn.sigmoid(y)
        if part == 2:
            outs[part][0] = y
            continue
        scale = GD_HEAD ** -0.5 if part == 0 else 1.0
        for h in range(nh):
            seg = y[:, h * GD_HEAD:(h + 1) * GD_HEAD]
            n = seg * lax.rsqrt(jnp.sum(seg * seg, axis=-1, keepdims=True) + 1e-6)
            outs[part][0, :, h * GD_HEAD:(h + 1) * GD_HEAD] = n * scale


def _gdn_prep(proj, col0, gd_conv, nh, ctx_len):
    nb, t, _ = proj.shape
    width = nh * GD_HEAD
    tm = PREP_TM
    assert gd_conv.shape[0] == 3 and ctx_len % tm == 0 and t % tm == 0
    in_specs, args = [], []
    for part in range(3):
        prev, nxt = _halo_specs(tm, width, col0 + part, t)
        in_specs += [pl.BlockSpec((1, tm, width), lambda b, i, c=col0 + part: (b, i, c)), prev, nxt]
        args += [proj, proj, proj]
    out_spec = pl.BlockSpec((1, tm, width), lambda b, i: (b, i, 0))
    return pl.pallas_call(
        functools.partial(_gdn_prep_kernel, ctx_len=ctx_len, t=t, nh=nh),
        grid=(nb, t // tm),
        in_specs=in_specs + [pl.BlockSpec(gd_conv.shape, lambda b, i: (0, 0))],
        out_specs=[out_spec] * 3,
        out_shape=[jax.ShapeDtypeStruct((nb, t, width), F32)] * 3,
        compiler_params=_cparams(("parallel", "parallel")),
        name="gdn_prep",
    )(*args, gd_conv)


def _attn_prep_kernel(q_ref, k_ref, v_ref, cos_ref, sin_ref, qg_ref, kg_ref, q_o, k_o, v_o):
    cos, sin = cos_ref[...], sin_ref[...]
    lane = lax.broadcasted_iota(jnp.int32, cos.shape, 1)
    low_half = (lane & (AT_HEAD // 2 - 1)) < AT_HEAD // 4

    def prep(seg, gain):
        n = seg * lax.rsqrt(jnp.mean(seg * seg, axis=-1, keepdims=True) + NORM_EPS) * gain
        partner = jnp.where(low_half, pltpu.roll(n, AT_HEAD - AT_HEAD // 4, 1), pltpu.roll(n, AT_HEAD // 4, 1))
        return n * cos + partner * sin

    for h in range(q_ref.shape[2] // AT_HEAD):
        sl = slice(h * AT_HEAD, (h + 1) * AT_HEAD)
        q_o[0, :, sl] = (prep(q_ref[0, :, sl], qg_ref[...]) * AT_HEAD ** -0.5).astype(BF16)
    for h in range(k_ref.shape[2] // AT_HEAD):
        sl = slice(h * AT_HEAD, (h + 1) * AT_HEAD)
        k_o[0, :, sl] = prep(k_ref[0, :, sl], kg_ref[...]).astype(BF16)
    v_o[0] = v_ref[0].astype(BF16)


def _attn_prep(proj, col_q, qw, kvw, at_q_norm, at_k_norm, rope):
    nb, t, _ = proj.shape
    tm = PREP_TM
    cos, sin = rope
    col_k = col_q * qw // kvw + qw // kvw
    blk = lambda w, c: pl.BlockSpec((1, tm, w), lambda b, i: (b, i, c))
    tab = pl.BlockSpec((tm, AT_HEAD), lambda b, i: (i, 0))
    vec = pl.BlockSpec((1, AT_HEAD), lambda b, i: (0, 0))
    out = lambda w: pl.BlockSpec((1, tm, w), lambda b, i: (b, i, 0))
    return pl.pallas_call(
        _attn_prep_kernel,
        grid=(nb, t // tm),
        in_specs=[blk(qw, col_q), blk(kvw, col_k), blk(kvw, col_k + 1), tab, tab, vec, vec],
        out_specs=[out(qw), out(kvw), out(kvw)],
        out_shape=[jax.ShapeDtypeStruct((nb, t, qw), BF16), jax.ShapeDtypeStruct((nb, t, kvw), BF16),
                   jax.ShapeDtypeStruct((nb, t, kvw), BF16)],
        compiler_params=_cparams(("parallel", "parallel")),
        name="attn_prep",
    )(proj, proj, proj, cos, sin, at_q_norm.reshape(1, AT_HEAD), at_k_norm.reshape(1, AT_HEAD))


def _mix_post_kernel(ryf, ryb, rbonus, rg, myf, myb, mo, gyf, gyb, gg, ay, lnw, lnb, mlg, gdg, hs_ref, o_ref, *, gw):
    y = ryf[0] + ryb[0]
    hs = hs_ref[...]
    inv_n = 1.0 / RW_HEAD
    mean = _mm(y, hs, "nn", 3, 1) * inv_n
    cen = y - mean
    var = _mm(cen * cen, hs, "nn", 3, 1) * inv_n
    o_ref[0, :, 0:gw] = ((cen * lax.rsqrt(var + RW_GN_EPS) * lnw[...] + lnb[...] + rbonus[0]) * rg[0]).astype(BF16)
    y = myf[0] + myb[0]
    for h in range(gw // ML_HEAD):
        sl = slice(h * ML_HEAD, (h + 1) * ML_HEAD)
        seg = y[:, sl]
        cen = seg - jnp.mean(seg, axis=-1, keepdims=True)
        var = jnp.mean(cen * cen, axis=-1, keepdims=True)
        out = cen * lax.rsqrt(var + NORM_EPS) * mlg[:, sl] * jax.nn.sigmoid(mo[0, :, sl])
        o_ref[0, :, gw + h * ML_HEAD:gw + (h + 1) * ML_HEAD] = out.astype(BF16)
    y = gyf[0] + gyb[0]
    for h in range(gw // GD_HEAD):
        sl = slice(h * GD_HEAD, (h + 1) * GD_HEAD)
        seg = y[:, sl]
        gate = gg[0, :, sl]
        out = seg * lax.rsqrt(jnp.mean(seg * seg, axis=-1, keepdims=True) + NORM_EPS) * gdg[...] * (gate * jax.nn.sigmoid(gate))
        o_ref[0, :, 2 * gw + h * GD_HEAD:2 * gw + (h + 1) * GD_HEAD] = out.astype(BF16)
    o_ref[0, :, 3 * gw:4 * gw] = ay[0].astype(BF16)


def _mix_post(rw, ml, gd, at, proj, col_o, col_g, params, hs):
    nb, t, gw = at.shape
    tm = PREP_TM
    row = lambda c=0: pl.BlockSpec((1, tm, gw), lambda b, i: (b, i, c))
    full = lambda a: pl.BlockSpec(a.shape, lambda b, i: (0,) * a.ndim)
    ln_w, ln_b, ml_g, gd_g = params
    consts = [ln_w.reshape(1, gw), ln_b.reshape(1, gw), ml_g.reshape(1, gw), gd_g.reshape(1, GD_HEAD), hs]
    return pl.pallas_call(
        functools.partial(_mix_post_kernel, gw=gw),
        grid=(nb, t // tm),
        in_specs=[row()] * 4 + [row(), row(), row(col_o)] + [row(), row(), row(col_g)] + [row()]
                 + [full(a) for a in consts],
        out_specs=pl.BlockSpec((1, tm, 4 * gw), lambda b, i: (b, i, 0)),
        out_shape=jax.ShapeDtypeStruct((nb, t, 4 * gw), BF16),
        compiler_params=_cparams(("parallel", "parallel")),
        name="mix_post",
    )(*rw, ml[0], ml[1], proj, gd[0], gd[1], proj, at, *consts)


MOD_TN = 1024
MOD_KC = 256


def _mod_kernel(s_ref, w_ref, b_ref, o_ref, *, ncond):
    d = w_ref.shape[1]
    acc = [jnp.zeros((1, w_ref.shape[2]), F32) for _ in range(ncond)]
    for kc in range(d // MOD_KC):
        rows = slice(kc * MOD_KC, (kc + 1) * MOD_KC)
        w = w_ref[0, rows, :]
        for j in range(ncond):
            acc[j] = acc[j] + jnp.sum(w * s_ref[rows, j:j + 1], axis=0, keepdims=True)
    bias = b_ref[0]
    pad = [jnp.zeros_like(bias)] * (o_ref.shape[1] - ncond)
    o_ref[0] = jnp.concatenate([a + bias for a in acc] + pad, axis=0)


def _modulation(conds, ada_w, ada_b):
    depth, d, n = ada_w.shape
    ncond = conds.shape[0]
    s_cols = jnp.pad(jax.nn.silu(conds).T, ((0, 0), (0, 8 - ncond)))
    return pl.pallas_call(
        functools.partial(_mod_kernel, ncond=ncond),
        grid=(depth, n // MOD_TN),
        in_specs=[pl.BlockSpec((d, 8), lambda l, j: (0, 0)),
                  pl.BlockSpec((1, d, MOD_TN), lambda l, j: (l, 0, j)),
                  pl.BlockSpec((1, 1, MOD_TN), lambda l, j: (l, 0, j))],
        out_specs=pl.BlockSpec((1, 8, MOD_TN), lambda l, j: (l, 0, j)),
        out_shape=jax.ShapeDtypeStruct((depth, 8, n), F32),
        compiler_params=_cparams(("parallel", "parallel")),
        name="adaln_modulation",
    )(s_cols, ada_w, ada_b.reshape(depth, 1, n))


PROJ_TM = 544
PROJ_TN = 1792


def _modulated_norm(x, gain, mod_ref, shift_row, row0, ctx_len):
    y = x * lax.rsqrt(jnp.mean(x * x, axis=-1, keepdims=True) + NORM_EPS) * gain
    rows = row0 + lax.broadcasted_iota(jnp.int32, (x.shape[0], 1), 0)
    is_ctx = rows < ctx_len
    shift = jnp.where(is_ctx, mod_ref[0, 0, shift_row:shift_row + 1, :], mod_ref[0, 1, shift_row:shift_row + 1, :])
    scale = jnp.where(is_ctx, mod_ref[0, 0, shift_row + 1:shift_row + 2, :],
                      mod_ref[0, 1, shift_row + 1:shift_row + 2, :])
    return y * (1.0 + scale) + shift


def _norm_proj_kernel(x_ref, g_ref, mod_ref, w_ref, o_ref, h_scr, *, tiles_per_seq, ctx_len):
    row0 = (pl.program_id(0) % tiles_per_seq) * x_ref.shape[0]

    @pl.when(pl.program_id(1) == 0)
    def _():
        h_scr[...] = _modulated_norm(x_ref[...], g_ref[...], mod_ref, 0, row0, ctx_len).astype(BF16)

    o_ref[...] = jnp.dot(h_scr[...], w_ref[...], preferred_element_type=F32)


def _norm_proj(x, gain, mod, w, ctx_len):
    nb, t, d = x.shape
    n = w.shape[1]
    tm, tn = _pick_tile(t, PROJ_TM), PROJ_TN
    assert n % tn == 0
    tps = t // tm
    out = pl.pallas_call(
        functools.partial(_norm_proj_kernel, tiles_per_seq=tps, ctx_len=ctx_len),
        grid=(nb * tps, n // tn),
        in_specs=[pl.BlockSpec((tm, d), lambda i, j: (i, 0)),
                  pl.BlockSpec((1, d), lambda i, j: (0, 0)),
                  pl.BlockSpec((1, 2, 8, d), lambda i, j: (i // tps, 0, 0, 0)),
                  pl.BlockSpec((d, tn), lambda i, j: (0, j))],
        out_specs=pl.BlockSpec((tm, tn), lambda i, j: (i, j)),
        out_shape=jax.ShapeDtypeStruct((nb * t, n), F32),
        scratch_shapes=[pltpu.VMEM((tm, d), BF16)],
        compiler_params=_cparams(("parallel", "arbitrary")),
        name="norm_in_proj",
    )(x.reshape(nb * t, d), gain.reshape(1, d), mod, w)
    return out.reshape(nb, t, n)


OUT_TM = 272


def _out_proj_kernel(y_ref, w_ref, x_ref, g_ref, mod_ref, wr_hi, wr_lo,
                     xo_ref, h_ref, aff_ref, *, tiles_per_seq, ctx_len, n_experts):
    acc = jnp.dot(y_ref[...], w_ref[...], preferred_element_type=F32)
    tm = x_ref.shape[0]
    row0 = (pl.program_id(0) % tiles_per_seq) * tm
    rows = row0 + lax.broadcasted_iota(jnp.int32, (tm, 1), 0)
    gate = jnp.where(rows < ctx_len, mod_ref[0, 0, 2:3, :], mod_ref[0, 1, 2:3, :])
    x_new = x_ref[...] + gate * acc
    xo_ref[...] = x_new
    h = _modulated_norm(x_new, g_ref[...], mod_ref, 3, row0, ctx_len)
    h_ref[...] = h
    h_hi = h.astype(BF16)
    h_lo = (h - h_hi.astype(F32)).astype(BF16)
    logits = (jnp.dot(h_hi, wr_hi[...], preferred_element_type=F32)
              + jnp.dot(h_lo, wr_hi[...], preferred_element_type=F32)
              + jnp.dot(h_hi, wr_lo[...], preferred_element_type=F32))
    lane = lax.broadcasted_iota(jnp.int32, logits.shape, 1)
    logits = jnp.where(lane < n_experts, logits, -jnp.inf)
    p = jnp.exp(logits - jnp.max(logits, axis=-1, keepdims=True))
    aff_ref[...] = p / jnp.sum(p, axis=-1, keepdims=True)


def _out_proj(ymix, w_out, x, gain, mod, w_router, ctx_len):
    nb, t, d = x.shape
    mw = ymix.shape[-1]
    ne = w_router.shape[1]
    tm = _pick_tile(t, OUT_TM)
    tps = t // tm
    wr =jnp.pad(w_router, ((0, 0), (0, LANES - ne)))
    wr_hi = wr.astype(BF16)
    wr_lo = (wr - wr_hi.astype(F32)).astype(BF16)
    row = lambda w: pl.BlockSpec((tm, w), lambda i: (i, 0))
    full = lambda a: pl.BlockSpec(a.shape, lambda i: (0,) * a.ndim)
    xo, h, aff = pl.pallas_call(
        functools.partial(_out_proj_kernel, tiles_per_seq=tps, ctx_len=ctx_len, n_experts=ne),
        grid=(nb * tps,),
        in_specs=[row(mw), full(w_out), row(d), pl.BlockSpec((1, d), lambda i: (0, 0)),
                  pl.BlockSpec((1, 2, 8, d), lambda i: (i // tps, 0, 0, 0)), full(wr_hi), full(wr_lo)],
        out_specs=[row(d), row(d), row(LANES)],
        out_shape=[jax.ShapeDtypeStruct((nb * t, d), F32), jax.ShapeDtypeStruct((nb * t, d), F32),
                   jax.ShapeDtypeStruct((nb * t, LANES), F32)],
        compiler_params=_cparams(("parallel",)),
        name="out_proj_norm_router",
    )(ymix.reshape(nb * t, mw), w_out, x.reshape(nb * t, d), gain.reshape(1, d), mod, wr_hi, wr_lo)
    return xo, h, aff.reshape(nb, t, LANES)


ROUTE_BLK = 256
ROUTE_ROWS = 8


def _route_kernel(aff_ref, *out_refs, segments, n_experts):
    for (start, length), o_ref in zip(segments, out_refs):
        cap = EC_FACTOR * length // n_experts
        capp = o_ref.shape[2]
        x = aff_ref[0, start:start + length, :]

        def refine(i, thr):
            cand = thr | jnp.left_shift(jnp.int32(1), 30 - i)
            cnt = jnp.sum(jnp.where(x >= pltpu.bitcast(cand, F32), 1.0, 0.0), axis=0, keepdims=True)
            return jnp.where(cnt >= cap, cand, thr)

        thr = lax.fori_loop(0, 31, refine, jnp.zeros((1, LANES), jnp.int32))
        thr_f = pltpu.bitcast(thr, F32)
        need = cap - jnp.sum(jnp.where(x > thr_f, 1.0, 0.0), axis=0, keepdims=True)
        blk = min(ROUTE_BLK, length)
        before = (lax.broadcasted_iota(jnp.int32, (blk, blk), 1)
                  < lax.broadcasted_iota(jnp.int32, (blk, blk), 0)).astype(BF16)
        lane = lax.broadcasted_iota(jnp.int32, (blk, LANES), 1)
        slot_ids = lax.broadcasted_iota(jnp.int32, (blk, capp), 1).astype(F32)
        out_expert = lax.broadcasted_iota(jnp.int32, (LANES, capp), 0) >> 3
        src = lax.broadcasted_iota(jnp.int32, (LANES, LANES), 0)
        dst = lax.broadcasted_iota(jnp.int32, (LANES, LANES), 1)
        assert start % blk == 0 and length % blk == 0
        o_ref[0] = jnp.zeros((LANES, capp), F32)

        def block(kb, carry, start=start, blk=blk, thr_f=thr_f, need=need, before=before, lane=lane,
                  slot_ids=slot_ids, out_expert=out_expert, src=src, dst=dst, o_ref=o_ref):
            carry_tied, carry_sel = carry
            xb = aff_ref[0, pl.ds(pl.multiple_of(start + kb * blk, blk), blk), :]
            above_b = xb > thr_f
            tied_b = xb == thr_f
            tied_f = jnp.where(tied_b, 1.0, 0.0)
            tied_before = _mm(before, tied_f) + carry_tied
            sel_f = jnp.where(jnp.logical_or(above_b, jnp.logical_and(tied_b, tied_before < need)), 1.0, 0.0)
            slot = _mm(before, sel_f) + carry_sel
            tok = kb * blk + lax.broadcasted_iota(jnp.int32, (blk, LANES), 0)
            vals = jnp.where((lane & 7) == 0, (tok >> 6).astype(F32),
                             jnp.where((lane & 7) == 1, (tok & 63).astype(F32), 0.0))
            for r, piece in enumerate(_split(xb, 3)):
                to_lane = jnp.logical_and(dst == src * ROUTE_ROWS + 2 + r, src < n_experts)
                vals = vals + _mm(piece, to_lane.astype(BF16))
            vals = vals.astype(BF16)

            def expert(e, _):
                on_lane = lane == e
                sel_col = jnp.sum(jnp.where(on_lane, sel_f, 0.0), axis=1, keepdims=True)
                slot_col = jnp.sum(jnp.where(on_lane, slot, 0.0), axis=1, keepdims=True)
                hit = jnp.logical_and(sel_col > 0.5, slot_col == slot_ids)
                res = _mm(vals, jnp.where(hit, 1.0, 0.0).astype(BF16), "tn")
                o_ref[0] += jnp.where(out_expert == e, res, 0.0)
                return 0

            lax.fori_loop(0, n_experts, expert, 0)
            return (carry_tied + jnp.sum(tied_f, axis=0, keepdims=True),
                    carry_sel + jnp.sum(sel_f, axis=0, keepdims=True))

        zero = jnp.zeros((1, LANES), F32)
        lax.fori_loop(0, length // blk, block, (zero, zero))


def _route(aff, segments, n_experts):
    nb, t, _ = aff.shape
    caps = [EC_FACTOR * length // n_experts for _, length in segments]
    capps = [-(-c // LANES) * LANES for c in caps]
    outs = pl.pallas_call(
        functools.partial(_route_kernel, segments=tuple(segments), n_experts=n_experts),
        grid=(nb,),
        in_specs=[pl.BlockSpec((1, t, LANES), lambda b: (b, 0, 0))],
        out_specs=[pl.BlockSpec((1, LANES, cp), lambda b: (b, 0, 0)) for cp in capps],
        out_shape=[jax.ShapeDtypeStruct((nb, LANES, cp), F32) for cp in capps],
        compiler_params=_cparams(("parallel",)),
        name="expert_choice_route",
    )(aff)
    picks = []
    for o, cap in zip(outs, caps):
        v = o.reshape(nb, LANES // ROUTE_ROWS, ROUTE_ROWS, -1)[:, :n_experts, :, :cap]
        tok = jnp.round(v[:, :, 0] * 64.0 + v[:, :, 1]).astype(jnp.int32)
        picks.append((tok, v[:, :, 2] + v[:, :, 3] + v[:, :, 4]))
    return picks


FFN_TF = 256


def _ffn_kernel(x_ref, w1_ref, w3_ref, w2_ref, o_ref):
    @pl.when(pl.program_id(1) == 0)
    def _():
        o_ref[...] = jnp.zeros_like(o_ref)

    xs = x_ref[0]
    a = jnp.dot(xs, w1_ref[0].astype(BF16), preferred_element_type=F32)
    b = jnp.dot(xs, w3_ref[0].astype(BF16), preferred_element_type=F32)
    hid = (a * jax.nn.sigmoid(a) * b).astype(BF16)
    o_ref[0] += jnp.dot(hid, w2_ref[0].astype(BF16), preferred_element_type=F32)


def _expert_ffn(xs, w1, w3, w2):
    ne, r, d = xs.shape
    f = w1.shape[2]
    tf = FFN_TF
    return pl.pallas_call(
        _ffn_kernel,
        grid=(ne, f // tf),
        in_specs=[pl.BlockSpec((1, r, d), lambda e, j: (e, 0, 0)),
                  pl.BlockSpec((1, d, tf), lambda e, j: (e, 0, j)),
                  pl.BlockSpec((1, d, tf), lambda e, j: (e, 0, j)),
                  pl.BlockSpec((1, tf, d), lambda e, j: (e, j, 0))],
        out_specs=pl.BlockSpec((1, r, d), lambda e, j: (e, 0, 0)),
        out_shape=jax.ShapeDtypeStruct((ne, r, d), F32),
        compiler_params=_cparams(("parallel", "arbitrary")),
        name="expert_ffn",
    )(xs, w1, w3, w2)


def _moe_ffn_kernel(idx_ref, gate_ref, g2_ref, w1_ref, w3_ref, w2_ref, h_hbm, x_hbm, xo_hbm,
                    xg, xs_bf, acc, rows, gsem, rsem, wsem, *, groups):
    del x_hbm
    e, j = pl.program_id(0), pl.program_id(1)
    ne, nj = pl.num_programs(0), pl.num_programs(1)
    r_tot = acc.shape[0]
    per = r_tot // nj

    def row_copy(src, dst, s, d_, sem):
        return pltpu.make_async_copy(src.at[pl.ds(s, 1), :], dst.at[pl.ds(d_, 1), :], sem)

    def for_rows(lo, n, fn):
        def body(i, c):
            fn(lo + i)
            return c
        lax.fori_loop(0, n, body, 0)

    def gather_start(expert, lo, n):
        for_rows(lo, n, lambda r: row_copy(h_hbm, xg, idx_ref[expert, r], r, gsem).start())

    def gather_wait(lo, n):
        for_rows(lo, n, lambda r: row_copy(h_hbm, xg, 0, r, gsem).wait())

    @pl.when(jnp.logical_and(e == 0, j == 0))
    def _():
        gather_start(0, 0, r_tot)
        gather_wait(0, r_tot)

    @pl.when(j == 0)
    def _():
        xs_bf[...] = xg[...].astype(BF16)

    @pl.when(jnp.logical_and(j == 0, e > 0))
    def _():
        for_rows(0, r_tot, lambda r: row_copy(rows, xo_hbm, r, idx_ref[e - 1, r], wsem).start())

    @pl.when(e + 1 < ne)
    def _():
        gather_start(e + 1, j * per, per)

    @pl.when(j == nj - 1)
    def _():
        @pl.when(e > 0)
        def _():
            for_rows(0, r_tot, lambda r: row_copy(rows, xo_hbm, r, 0, wsem).wait())
        for_rows(0, r_tot, lambda r: row_copy(xo_hbm, rows, idx_ref[e, r], r, rsem).start())

    xs = xs_bf[...]
    a = jnp.dot(xs, w1_ref[0, 0].astype(BF16), preferred_element_type=F32)
    b = jnp.dot(xs, w3_ref[0, 0].astype(BF16), preferred_element_type=F32)
    hid = (a * jax.nn.sigmoid(a) * b).astype(BF16)
    part = jnp.dot(hid, w2_ref[0, 0].astype(BF16), preferred_element_type=F32)

    @pl.when(j == 0)
    def _():
        acc[...] = part

    @pl.when(j > 0)
    def _():
        acc[...] += part

    @pl.when(j == nj - 1)
    def _():
        for_rows(0, r_tot, lambda r: row_copy(xo_hbm, rows, 0, r, rsem).wait())
        for g, (lo, n) in enumerate(groups):
            rows[lo:lo + n, :] += (gate_ref[0, lo:lo + n, :] * acc[lo:lo + n, :]) * g2_ref[g:g + 1, :]

        @pl.when(e + 1 < ne)
        def _():
            gather_wait(0, r_tot)

        @pl.when(e == ne - 1)
        def _():
            for_rows(0, r_tot, lambda r: row_copy(rows, xo_hbm, r, idx_ref[e, r], wsem).start())
            for_rows(0, r_tot, lambda r: row_copy(rows, xo_hbm, r, 0, wsem).wait())


def _moe_ffn(x, h, idx, gate, g2, groups, w1, w3, w2, layer):
    n, d = x.shape
    ne, r = idx.shape
    f = w1.shape[3]
    tf = FFN_TF
    nj = f // tf
    assert r % nj == 0
    grid_spec = pltpu.PrefetchScalarGridSpec(
        num_scalar_prefetch=1,
        grid=(ne, nj),
        in_specs=[pl.BlockSpec((1, r, 1), lambda e, j, idx: (e, 0, 0)),
                  pl.BlockSpec(g2.shape, lambda e, j, idx: (0, 0)),
                  pl.BlockSpec((1, 1, d, tf), lambda e, j, idx: (layer, e, 0, j)),
                  pl.BlockSpec((1, 1, d, tf), lambda e, j, idx: (layer, e, 0, j)),
                  pl.BlockSpec((1, 1, tf, d), lambda e, j, idx: (layer, e, j, 0)),
                  pl.BlockSpec(memory_space=pl.ANY),
                  pl.BlockSpec(memory_space=pl.ANY)],
        out_specs=pl.BlockSpec(memory_space=pl.ANY),
        scratch_shapes=[pltpu.VMEM((r, d), F32), pltpu.VMEM((r, d), BF16), pltpu.VMEM((r, d), F32),
                        pltpu.VMEM((r, d), F32), pltpu.SemaphoreType.DMA, pltpu.SemaphoreType.DMA,
                        pltpu.SemaphoreType.DMA])
    return pl.pallas_call(
        functools.partial(_moe_ffn_kernel, groups=tuple(groups)),
        grid_spec=grid_spec,
        out_shape=jax.ShapeDtypeStruct((n, d), F32),
        input_output_aliases={7: 0},
        compiler_params=_cparams(("arbitrary", "arbitrary")),
        name="moe_ffn",
    )(idx, gate.reshape(ne, r, 1), g2, w1, w3, w2, h, x)


def _moe_glue_reference(h, aff, segments, w1, w3, w2):
    nb, t, d = h.shape
    ne = aff.shape[-1]
    picks, xs = [], []
    for start, length in segments:
        cap = EC_FACTOR * length // ne
        gate, idx = lax.top_k(jnp.swapaxes(aff[:, start:start + length], 1, 2), cap)
        idx = idx + start
        picks.append((gate, idx))
        rows = jnp.take_along_axis(h[:, None], idx[..., None], axis=2)
        xs.append(jnp.swapaxes(rows, 0, 1).reshape(ne, nb * cap, d))
    y = _expert_ffn(jnp.concatenate(xs, axis=1), w1, w3, w2)
    out = jnp.zeros((nb, t, d), F32)
    bidx = jnp.arange(nb)[:, None, None]
    off = 0
    for (start, length), (gate, idx) in zip(segments, picks):
        cap = gate.shape[-1]
        ys = jnp.swapaxes(y[:, off:off + nb * cap].reshape(ne, nb, cap, d), 0, 1)
        out = out.at[bidx, idx].add(ys * gate[..., None])
        off += nb * cap
    return out


RELAYOUT_TM = 256


def _relayout_kernel(w_ref, o_ref, *, gw, rw_cols, ngate):
    ml0 = rw_cols
    gd0 = ml0 + 4 * gw + ngate
    at0 = gd0 + 4 * gw + ngate
    o_ref[:, 0:rw_cols] = w_ref[:, 0:rw_cols].astype(BF16)
    o_ref[:, rw_cols:rw_cols + ngate] = w_ref[:, ml0 + 4 * gw:gd0].astype(BF16)
    o_ref[:, rw_cols + ngate:rw_cols + 2 * ngate] = w_ref[:, gd0 + 4 * gw:at0].astype(BF16)
    o_ref[:, rw_cols + 2 * ngate:4 * gw] = jnp.zeros((o_ref.shape[0], 4 * gw - rw_cols - 2 * ngate), BF16)
    o_ref[:, 4 * gw:8 * gw] = w_ref[:, ml0:ml0 + 4 * gw].astype(BF16)
    o_ref[:, 8 * gw:12 * gw] = w_ref[:, gd0:gd0 + 4 * gw].astype(BF16)
    o_ref[:, 12 * gw:] = w_ref[:, at0:].astype(BF16)


def _relayout_w_in(w_in, layer, gw, rw_cols, ngate):
    _, d, n = w_in.shape
    assert rw_cols % LANES == 0 and rw_cols + 2 * ngate <= 4 * gw
    n_out = n - rw_cols - 2 * ngate + 4 * gw
    tm = RELAYOUT_TM
    return pl.pallas_call(
        functools.partial(_relayout_kernel, gw=gw, rw_cols=rw_cols, ngate=ngate),
        grid=(d // tm,),
        in_specs=[pl.BlockSpec((None, tm, n), lambda i: (layer, i, 0))],
        out_specs=pl.BlockSpec((tm, n_out), lambda i: (i, 0)),
        out_shape=jax.ShapeDtypeStruct((d, n_out), BF16),
        compiler_params=_cparams(("parallel",)),
        name="w_in_relayout",
    )(w_in)


def kernel(x, c, ctx, c_ctx, ada_w, ada_b, norm1_g, norm2_g, w_in, w_out, rw_mu, rw_w0, rw_w_up, rw_a0, rw_a_up,
           rw_g_up, rw_k_k, rw_k_a, rw_r_k, rw_ln_w, rw_ln_b, ml_ib, ml_fb, ml_norm_g, gd_conv, gd_a_log,
           gd_dt_bias, gd_norm_g, at_q_norm, at_k_norm, w_router, w_exp1, w_exp3, w_exp2, final_g):
    nb, n_lat, d = x.shape
    ctx_len = ctx.shape[1]
    t = ctx_len + n_lat
    depth = ada_w.shape[0]
    gw = d // N_GROUPS
    rw_cols = rw_mu.shape[-1]
    ngate = 4 * ml_ib.shape[-1]
    at_cols = w_in.shape[-1] - rw_cols - 2 * (4 * gw + ngate)

    xs = jnp.concatenate([ctx, x], axis=1)
    mods = _modulation(jnp.concatenate([c, c_ctx[None]], axis=0), ada_w, ada_b)
    rope = _rope_tables(t, ctx_len)
    nh = ml_ib.shape[-1]
    assert 4 * gw == 2048 and gw == nh * ML_HEAD == nh * GD_HEAD and at_cols == 2 * gw
    head_of = jnp.arange(gw) // RW_HEAD
    hs = (head_of[:, None] == head_of[None, :]).astype(BF16)

    for layer in range(depth):
        last = layer == depth - 1
        m = mods[layer, :nb + 1].reshape(nb + 1, 6, d)
        m = jnp.pad(m, ((0, 0), (0, 2), (0, 0)))
        mod = jnp.stack([jnp.broadcast_to(m[nb], (nb, 8, d)), m[:nb]], axis=1)
        proj = _norm_proj(xs, norm1_g[layer], mod, _relayout_w_in(w_in, layer, gw, rw_cols, ngate), ctx_len)
        p_rw = {k: v[layer] for k, v in dict(
            rw_mu=rw_mu, rw_w0=rw_w0, rw_w_up=rw_w_up, rw_a0=rw_a0, rw_a_up=rw_a_up, rw_g_up=rw_g_up,
            rw_k_k=rw_k_k, rw_k_a=rw_k_a, rw_r_k=rw_r_k).items()}
        graw = proj[..., rw_cols:rw_cols + 2 * ngate]
        r, v, kk, lwf, bf, ktf, lwb, bb, ktb, g_rw, bonus = _rwkv_prep(proj, 0, p_rw, hs, ctx_len)
        ryf, ryb = _rwkv_scan(r, v, kk, [(lwf, bf, ktf), (lwb, bb, ktb)], ctx_len)

        gm = graw[..., :ngate]
        li = gm[..., :2 * nh] + ml_ib[layer].reshape(-1)
        lf = jax.nn.log_sigmoid(gm[..., 2 * nh:] + ml_fb[layer].reshape(-1))
        g_ml = jnp.concatenate([li, lf], axis=-1)
        myf, myb = _mlstm_scan(proj, proj, proj, (4, 5, 6), g_ml, jnp.swapaxes(g_ml, 1, 2), ctx_len)

        gg = graw[..., ngate:]
        lg = -jnp.exp(gd_a_log[layer]).reshape(-1) * jax.nn.softplus(gg[..., :2 * nh] + gd_dt_bias[layer].reshape(-1))
        q_gd, k_gd, v_gd = _gdn_prep(proj, 8, gd_conv[layer], nh, ctx_len)
        gyf, gyb = _gdn_scan(q_gd, k_gd, v_gd, jnp.concatenate([lg, jax.nn.sigmoid(gg[..., 2 * nh:])], axis=-1), ctx_len)

        kvw = AT_KV_HEADS * AT_HEAD
        q_at, k_at, v_at = _attn_prep(proj, 12, at_cols - 2 * kvw, kvw, at_q_norm[layer], at_k_norm[layer], rope)
        ay = _attention(q_at, k_at, v_at, ctx_len)

        ymix = _mix_post((ryf, ryb, bonus, g_rw), (myf, myb), (gyf, gyb), ay, proj, 7, 11,
                         (rw_ln_w[layer], rw_ln_b[layer], ml_norm_g[layer], gd_norm_g[layer]), hs)
        xs_flat, h2, aff = _out_proj(ymix, w_out[layer].astype(BF16), xs, norm2_g[layer], mod, w_router[layer],
                                     ctx_len)
        segments = [(ctx_len, n_lat)] if last else [(0, ctx_len), (ctx_len, n_lat)]
        ne = w_router.shape[-1]
        idx_parts, gate_parts, groups, g2_rows, off = [], [], [], [], 0
        for (start, length), (tok, gate) in zip(segments, _route(aff, segments, ne)):
            for b in range(nb):
                idx_parts.append(tok[b] + (b * t + start))
                gate_parts.append(gate[b])
                groups.append((off, tok.shape[-1]))
                off += tok.shape[-1]
                g2_rows.append(mod[b, 0 if start < ctx_len else 1, 5])
        xs = _moe_ffn(xs_flat, h2, jnp.concatenate(idx_parts, axis=1), jnp.concatenate(gate_parts, axis=1),
                      jnp.stack(g2_rows), groups, w_exp1, w_exp3, w_exp2, layer).reshape(nb, t, d)

    xl = xs[:, ctx_len:]
    return xl * lax.rsqrt(jnp.mean(xl * xl, axis=-1, keepdims=True) + NORM_EPS) * final_g
```

```python
import functools
import math

import jax
import jax.numpy as jnp
from jax import lax
from jax.experimental import pallas as pl
from jax.experimental.pallas import tpu as pltpu

F32 = jnp.float32
BF16 = jnp.bfloat16

NORM_EPS = 1e-6
GRID_W = 64
N_GROUPS = 4
RW_HEAD = 64
RW_DECAY_RANK = 64
RW_ICLR_RANK = 64
RW_GATE_RANK = 128
RW_DECAY_SCALE = math.exp(-0.5)
RW_GN_EPS = 64e-5
RW_CHUNK = 64
ML_HEAD = 128
ML_CHUNK = 256
GD_HEAD = 128
GD_CHUNK = 64
AT_HEAD = 128
AT_KV_HEADS = 2
ROPE_THETA = 10000.0
AT_TQ = 256
N_EXPERTS = 16
EC_FACTOR = 2

VMEM_LIMIT = 56 * 1024 * 1024
LANES = 128


def _cparams(sem):
    return pltpu.CompilerParams(dimension_semantics=sem, vmem_limit_bytes=VMEM_LIMIT)


def _pick_tile(n, limit, mult=16):
    best = None
    for cand in range(mult, min(n, limit) + 1, mult):
        if n % cand == 0:
            best = cand
    assert best is not None, (n, limit, mult)
    return best


_DIMS = {
    "nn": (((1,), (0,)), ((), ())),
    "nt": (((1,), (1,)), ((), ())),
    "tn": (((0,), (0,)), ((), ())),
}


def _split(a, n):
    if a.dtype == BF16:
        return [a]
    pieces, rest = [], a
    for i in range(n):
        p = rest.astype(BF16)
        pieces.append(p)
        if i + 1 < n:
            rest = rest - p.astype(F32)
    return pieces


def _mm(a, b, dims="nn", pa=1, pb=1):
    ap, bp = _split(a, pa), _split(b, pb)
    top = max(len(ap), len(bp))
    acc = None
    for i, x in enumerate(ap):
        for j, y in enumerate(bp):
            if i + j < top:
                t = lax.dot_general(x, y, _DIMS[dims], preferred_element_type=F32)
                acc = t if acc is None else acc + t
    return acc


def _tri(n, rev, strict, reps=1):
    t = lax.broadcasted_iota(jnp.int32, (n, reps * n), 0)
    s = lax.broadcasted_iota(jnp.int32, (n, reps * n), 1)
    if reps > 1:
        assert n & (n - 1) == 0
        s = s & (n - 1)
    if rev:
        return (s > t) if strict else (s >= t)
    return (s < t) if strict else (s <= t)


def _chunk_maps(nsteps, nctx):
    fwd = lambda i: i
    bwd = lambda i: jnp.where(i < nctx, nctx - 1 - i, nsteps - 1 - i + nctx)
    return fwd, bwd


def _unit_lower_inverse(a, levels, passes):
    n = a.shape[0]
    eye = (lax.broadcasted_iota(jnp.int32, (n, n), 0) == lax.broadcasted_iota(jnp.int32, (n, n), 1)).astype(F32)
    y = -a
    r = eye + y
    for _ in range(levels):
        y = _mm(y, y, "nn", passes, passes)
        r = r + _mm(r, y, "nn", passes, passes)
    return r


def _mlstm_kernel(qf, kf, vf, gf, gtf, qb, kb, vb, gb, gtb, yf, yb, c_scr, m_scr, *, nb, nh, chunk):
    step = pl.program_id(0)

    @pl.when(step == 0)
    def _():
        c_scr[...] = jnp.zeros_like(c_scr)
        m_scr[...] = jnp.zeros_like(m_scr)

    scale = ML_HEAD ** -0.5
    lane = lax.broadcasted_iota(jnp.int32, (chunk, ML_HEAD), 1)
    ones_col = (lane == 0).astype(F32)
    units = []
    for d, (q_ref, k_ref, v_ref, g_ref, gt_ref, y_ref) in enumerate(
            ((qf, kf, vf, gf, gtf, yf), (qb, kb, vb, gb, gtb, yb))):
        rev = d == 1
        incl = _tri(chunk, rev, False)
        cum_m = incl.astype(BF16)
        last = 0 if rev else chunk - 1
        for b in range(nb):
            g = g_ref[b]
            gt = gt_ref[b]
            cum_c = _mm(cum_m, g, "nn", 1, 3)
            cum_r = _mm(gt, cum_m, "nt", 3, 1)
            for h in range(nh):
                ci, cf = d * nh + h, (2 + d) * nh + h
                sl = slice(h * ML_HEAD, (h + 1) * ML_HEAD)
                u = dict(y_ref=y_ref, b=b, sl=sl, idx=(d * nb + b) * nh + h, last=last)
                u["q"] = q_ref[b, :, sl].astype(BF16)
                u["k"] = k_ref[b, :, sl] * scale
                u["v_aug"] = jnp.concatenate([v_ref[b, :, sl], ones_col], axis=1).astype(BF16)
                u["m_old"] = m_scr[u["idx"]][0:1, 0:1]
                u["bc"] = cum_c[:, cf:cf + 1]
                u["li_c"] = g[:, ci:ci + 1]
                dlog = jnp.where(incl, u["bc"] - cum_r[cf:cf + 1, :] + gt[ci:ci + 1, :], -jnp.inf)
                inter = u["bc"] + u["m_old"]
                u["mt"] = jnp.maximum(inter, jnp.max(dlog, axis=1, keepdims=True))
                u["p"] = jnp.exp(dlog - u["mt"])
                u["wi"] = jnp.exp(inter - u["mt"])
                units.append(u)
    for u in units:
        u["s"] = (_mm(u["q"], u["k"], "nt") * u["p"]).astype(BF16)
    for u in units:
        u["c_old"] = c_scr[u["idx"]]
        u["acc"] = _mm(u["s"], u["v_aug"]) + u["wi"] * _mm(u["q"], u["c_old"])
    for u in units:
        num = u["acc"][:, :ML_HEAD]
        den = u["acc"][:, ML_HEAD:ML_HEAD + 1]
        u["y_ref"][u["b"], :, u["sl"]] = num / jnp.maximum(jnp.abs(den), jnp.exp(-u["mt"]))
    for u in units:
        last = u["last"]
        m_new = u["mt"][last:last + 1, :]
        b_last = u["bc"][last:last + 1, :]
        wk = jnp.exp(b_last - u["bc"] + u["li_c"] - m_new)
        dc = jnp.exp(b_last + u["m_old"] - m_new)
        c_scr[u["idx"]] = dc * u["c_old"] + _mm(u["k"] * wk, u["v_aug"], "tn")
        m_scr[u["idx"]] = jnp.broadcast_to(m_new, (8, LANES))


def _mlstm_scan(q_src, k_src, v_src, cols, g, gt, ctx_len):
    nb, t, _ = g.shape
    nh = g.shape[-1] // 4
    width = nh * ML_HEAD
    chunk = ML_CHUNK
    nsteps, nctx = t // chunk, ctx_len // chunk
    fwd, bwd = _chunk_maps(nsteps, nctx)

    def specs(cm):
        col = lambda c: pl.BlockSpec((nb, chunk, width), lambda i, c=c: (0, cm(i), c))
        return [col(cols[0]), col(cols[1]), col(cols[2]),
                pl.BlockSpec((nb, chunk, 4 * nh), lambda i: (0, cm(i), 0)),
                pl.BlockSpec((nb, 4 * nh, chunk), lambda i: (0, 0, cm(i)))]

    out_spec = lambda cm: pl.BlockSpec((nb, chunk, width), lambda i: (0, cm(i), 0))
    kern = functools.partial(_mlstm_kernel, nb=nb, nh=nh, chunk=chunk)
    return pl.pallas_call(
        kern,
        grid=(nsteps,),
        in_specs=specs(fwd) + specs(bwd),
        out_specs=[out_spec(fwd), out_spec(bwd)],
        out_shape=[jax.ShapeDtypeStruct((nb, t, width), F32)] * 2,
        scratch_shapes=[pltpu.VMEM((2 * nb * nh, ML_HEAD, 2 * ML_HEAD), F32),
                        pltpu.VMEM((2 * nb * nh, 8, LANES), F32)],
        compiler_params=_cparams(("arbitrary",)),
        name="mlstm_scan",
    )(q_src, k_src, v_src, g, gt, q_src, k_src, v_src, g, gt)


def _head_layer_norm(y, nh, eps):
    b, t, w = y.shape
    yh = y.reshape(b, t, nh, w // nh)
    mean = jnp.mean(yh, axis=-1, keepdims=True)
    var = jnp.mean(jnp.square(yh - mean), axis=-1, keepdims=True)
    return ((yh - mean) * lax.rsqrt(var + eps)).reshape(b, t, w)


def _mlstm_mixer(proj, col0, graw, ml_ib, ml_fb, ml_norm_g, ctx_len):
    nh = graw.shape[-1] // 4
    width = nh * ML_HEAD
    li = graw[..., :2 * nh] + ml_ib.reshape(-1)
    lf = jax.nn.log_sigmoid(graw[..., 2 * nh:] + ml_fb.reshape(-1))
    g = jnp.concatenate([li, lf], axis=-1)
    yf, yb = _mlstm_scan(proj, proj, proj, (col0, col0 + 1, col0 + 2), g, jnp.swapaxes(g, 1, 2), ctx_len)
    y = _head_layer_norm(yf + yb, nh, NORM_EPS) * ml_norm_g
    o = proj[..., (col0 + 3) * width:(col0 + 4) * width]
    return y * jax.nn.sigmoid(o)


def _gdn_kernel(qf, kf, vf, gf, gtf, qb, kb, vb, gb, gtb, yf, yb, s_scr, *, nb, nh, chunk):
    step = pl.program_id(0)

    @pl.when(step == 0)
    def _():
        s_scr[...] = jnp.zeros_like(s_scr)

    levels = int(math.log2(chunk)) - 1
    c = chunk
    dh = GD_HEAD
    pw = 2 * dh
    first = lax.broadcasted_iota(jnp.int32, (c, pw), 1) < dh
    first2 = lax.broadcasted_iota(jnp.int32, (c, 2 * c), 1) < c
    first2_row = lax.broadcasted_iota(jnp.int32, (1, 2 * c), 1) < c
    first4 = (lax.broadcasted_iota(jnp.int32, (c, 2 * pw), 1) // dh) % 2 == 0
    same_head = (lax.broadcasted_iota(jnp.int32, (pw, pw), 0) // dh
                 == lax.broadcasted_iota(jnp.int32, (pw, pw), 1) // dh)
    same_blk = (lax.broadcasted_iota(jnp.int32, (2 * c, 2 * c), 0) // c
                == lax.broadcasted_iota(jnp.int32, (2 * c, 2 * c), 1) // c)
    eye2 = (lax.broadcasted_iota(jnp.int32, (c, 2 * c), 0)
            == lax.broadcasted_iota(jnp.int32, (c, 2 * c), 1) % c).astype(F32)

    def by_head(x, mask):
        return jnp.concatenate([jnp.where(mask, x, 0.0), jnp.where(mask, 0.0, x)], axis=0)

    def blockdiag(x):
        xb = x.astype(BF16)
        return jnp.where(same_blk, jnp.concatenate([xb, xb], axis=0), jnp.zeros((), BF16))

    units = []
    for d, (q_ref, k_ref, v_ref, g_ref, gt_ref, y_ref) in enumerate(
            ((qf, kf, vf, gf, gtf, yf), (qb, kb, vb, gb, gtb, yb))):
        rev = d == 1
        incl = _tri(c, rev, False)
        incl2 = _tri(c, rev, False, reps=2)
        strict2 = _tri(c, rev, True, reps=2)
        cum_m = incl.astype(BF16)
        cum_m2 = jnp.concatenate([cum_m, cum_m], axis=0)
        last = 0 if rev else c - 1
        for b in range(nb):
            g = g_ref[b]
            gt = gt_ref[b, 0]
            cum_c = _mm(cum_m, g, "nn", 1, 3)
            cum_r = _mm(gt, cum_m2, "nt", 3, 1)
            for hp in range(nh // 2):
                h0, h1 = 2 * hp, 2 * hp + 1
                sl = slice(hp * pw, (hp + 1) * pw)
                u = dict(y_ref=y_ref, b=b, sl=sl, idx=(d * nb + b) * (nh // 2) + hp, strict2=strict2)
                q, k, v = q_ref[b, :, sl], k_ref[b, :, sl], v_ref[b, :, sl]
                col = lambda arr, j: jnp.where(first, arr[:, j + h0:j + h0 + 1], arr[:, j + h1:j + h1 + 1])
                gc = col(cum_c, d * nh)
                beta = col(g, (2 + d) * nh)
                gc_c2 = jnp.where(first2, cum_c[:, d * nh + h0:d * nh + h0 + 1], cum_c[:, d * nh + h1:d * nh + h1 + 1])
                gc_r2 = jnp.where(first2_row, cum_r[d * nh + h0:d * nh + h0 + 1], cum_r[d * nh + h1:d * nh + h1 + 1])
                u["decay"] = jnp.where(incl2, jnp.exp(jnp.where(incl2, gc_c2 - gc_r2, 0.0)), 0.0)
                kb_ = k * beta
                eg = jnp.exp(gc)
                u["kq"] = jnp.concatenate([kb_, q], axis=0).astype(BF16)
                u["k_blk"] = by_head(k, first).astype(BF16)
                u["rhs"] = by_head(jnp.concatenate([v * beta, kb_ * eg], axis=1), first4).astype(BF16)
                u["qg"] = q * eg
                g_last = gc[last:last + 1, :]
                u["kd"] = (k * jnp.exp(g_last - gc)).astype(BF16)
                u["keep"] = jnp.exp(g_last)
                units.append(u)

    for u in units:
        kq = _mm(u["kq"], u["k_blk"], "nt")
        u["y"] = jnp.where(u["strict2"], -kq[:c] * u["decay"], 0.0)
        u["attn"] = (kq[c:] * u["decay"]).astype(BF16)
        u["inv"] = eye2 + u["y"]
        u["bd"] = blockdiag(u["y"])
    for _ in range(levels):
        for u in units:
            u["y"] = _mm(u["y"], u["bd"])
            u["bd"] = blockdiag(u["y"])
        for u in units:
            u["inv"] = u["inv"] + _mm(u["inv"], u["bd"])
    for u in units:
        u["sol"] = _mm(u["inv"], u["rhs"])
    for u in units:
        u["s_old"] = s_scr[u["idx"]]
        u["ws"] = _mm(jnp.concatenate([u["sol"][:, pw:], u["qg"]], axis=0), u["s_old"])
    for u in units:
        u["v_new"] = u["sol"][:, :pw] - u["ws"][:c]
        u["y_ref"][u["b"], :, u["sl"]] = u["ws"][c:] + _mm(u["attn"], by_head(u["v_new"], first))
    for u in units:
        upd = _mm(u["kd"], u["v_new"], "tn")
        s_scr[u["idx"]] = u["s_old"] * u["keep"] + jnp.where(same_head, upd, 0.0)


def _gdn_scan(q, k, v, g, ctx_len):
    nb, t, width = q.shape
    nh = width // GD_HEAD
    chunk = GD_CHUNK
    nsteps, nctx = t // chunk, ctx_len // chunk
    fwd, bwd = _chunk_maps(nsteps, nctx)
    gt = jnp.swapaxes(g.reshape(nb, nsteps, chunk, 4 * nh), 2, 3)

    def specs(cm):
        col = pl.BlockSpec((nb, chunk, width), lambda i: (0, cm(i), 0))
        return [col, col, col,
                pl.BlockSpec((nb, chunk, 4 * nh), lambda i: (0, cm(i), 0)),
                pl.BlockSpec((nb, 1, 4 * nh, chunk), lambda i: (0, cm(i), 0, 0))]

    out_spec = lambda cm: pl.BlockSpec((nb, chunk, width), lambda i: (0, cm(i), 0))
    kern = functools.partial(_gdn_kernel, nb=nb, nh=nh, chunk=chunk)
    return pl.pallas_call(
        kern,
        grid=(nsteps,),
        in_specs=specs(fwd) + specs(bwd),
        out_specs=[out_spec(fwd), out_spec(bwd)],
        out_shape=[jax.ShapeDtypeStruct((nb, t, width), F32)] * 2,
        scratch_shapes=[pltpu.VMEM((nb * nh, 2 * GD_HEAD, 2 * GD_HEAD), F32)],
        compiler_params=_cparams(("arbitrary",)),
        name="gdn_scan",
    )(q, k, v, g, gt, q, k, v, g, gt)


def _seq_shift(x, ctx_len, delta):
    t = x.shape[1]
    rows = jnp.arange(t)
    if delta < 0:
        sh = jnp.pad(x[:, :delta], ((0, 0), (-delta, 0), (0, 0)))
        ok = (rows + delta >= 0) & ((rows < ctx_len) | (rows + delta >= ctx_len))
    else:
        sh = jnp.pad(x[:, delta:], ((0, 0), (0, delta), (0, 0)))
        ok = (rows + delta < t) & ((rows >= ctx_len) | (rows + delta < ctx_len))
    return jnp.where(ok[None, :, None], sh, 0.0)


def _l2n(x, nh, eps=1e-6):
    b, t, w = x.shape
    xh = x.reshape(b, t, nh, w // nh)
    return (xh * lax.rsqrt(jnp.sum(xh * xh, axis=-1, keepdims=True) + eps)).reshape(b, t, w)


def _gdn_mixer(qkv, gate, graw, gd_conv, gd_a_log, gd_dt_bias, gd_norm_g, ctx_len):
    nh = graw.shape[-1] // 4
    width = nh * GD_HEAD
    pad = gd_conv.shape[0] // 2
    conv = sum(_seq_shift(qkv, ctx_len, j - pad) * gd_conv[j] for j in range(gd_conv.shape[0]))
    qkv = jax.nn.silu(conv)
    q = _l2n(qkv[..., :width], nh) * GD_HEAD ** -0.5
    k = _l2n(qkv[..., width:2 * width], nh)
    v = qkv[..., 2 * width:]
    lg = -jnp.exp(gd_a_log).reshape(-1) * jax.nn.softplus(graw[..., :2 * nh] + gd_dt_bias.reshape(-1))
    beta = jax.nn.sigmoid(graw[..., 2 * nh:])
    yf, yb = _gdn_scan(q, k, v, jnp.concatenate([lg, beta], axis=-1), ctx_len)
    o = yf + yb
    b, t, _ = o.shape
    oh = o.reshape(b, t, nh, GD_HEAD)
    oh = oh * lax.rsqrt(jnp.mean(oh * oh, axis=-1, keepdims=True) + NORM_EPS) * gd_norm_g
    return oh.reshape(b, t, width) * jax.nn.silu(gate)


RW_PASSES = 1


def _rwkv_kernel(*refs, nb, npair, chunk):
    ins, (yf, yb, s_scr) = refs[:12], refs[12:]
    step = pl.program_id(0)

    @pl.when(step == 0)
    def _():
        s_scr[...] = jnp.zeros_like(s_scr)

    levels = int(math.log2(chunk)) - 1
    pw = 2 * RW_HEAD
    c = chunk
    head0 = lax.broadcasted_iota(jnp.int32, (c, pw), 1) < RW_HEAD
    same_head = (lax.broadcasted_iota(jnp.int32, (pw, pw), 0) // RW_HEAD
                 == lax.broadcasted_iota(jnp.int32, (pw, pw), 1) // RW_HEAD)
    same_blk = (lax.broadcasted_iota(jnp.int32, (2 * c, 2 * c), 0) // c
                == lax.broadcasted_iota(jnp.int32, (2 * c, 2 * c), 1) // c)
    eye2 = (lax.broadcasted_iota(jnp.int32, (c, 2 * c), 0)
            == lax.broadcasted_iota(jnp.int32, (c, 2 * c), 1) % c).astype(F32)
    mid = c // 2

    def by_head(x):
        return jnp.concatenate([jnp.where(head0, x, 0.0), jnp.where(head0, 0.0, x)], axis=0)

    def blockdiag(x):
        xb = x.astype(BF16)
        return jnp.where(same_blk, jnp.concatenate([xb, xb], axis=0), jnp.zeros((), BF16))

    units = []
    for d, y_ref in enumerate((yf, yb)):
        r_ref, lw_ref, kt_ref, v_ref, kk_ref, b_ref = ins[6 * d:6 * d + 6]
        rev = d == 1
        incl = _tri(c, rev, False)
        strict = _tri(c, rev, True)
        incl2 = _tri(c, rev, False, reps=2)
        strict2 = _tri(c, rev, True, reps=2)
        cum_m = incl.astype(BF16)
        last = 0 if rev else c - 1
        for b in range(nb):
            lw_all = lw_ref[b]
            g_all = _mm(cum_m, lw_all, "nn", 1, 3)
            for p in range(npair):
                sl = slice(p * pw, (p + 1) * pw)
                u = dict(y_ref=y_ref, b=b, sl=sl, idx=(d * nb + b) * npair + p, incl2=incl2, strict2=strict2)
                lw, g = lw_all[:, sl], g_all[:, sl]
                r, kt, v, kk, bb = (x[b, :, sl] for x in (r_ref, kt_ref, v_ref, kk_ref, b_ref))
                gh = g - g[mid:mid + 1, :]
                e_pos, e_neg = jnp.exp(gh), jnp.exp(-gh)
                u["lhs"] = jnp.concatenate([jnp.exp(gh - lw) * kk, r * e_pos], axis=0).astype(BF16)
                u["rhs"] = jnp.concatenate([by_head(bb * e_neg), by_head(kt * e_neg)], axis=0).astype(BF16)
                u["st_lhs"] = jnp.concatenate([jnp.exp(g - lw) * kk, jnp.exp(g) * r], axis=0).astype(BF16)
                g_last = g[last:last + 1, :]
                dl = jnp.exp(g_last - g)
                u["keep"] = jnp.exp(g_last)
                u["bk_d"] = jnp.concatenate([bb * dl, kt * dl], axis=0).astype(BF16)
                u["v"] = v
                u["v_blk"] = by_head(v).astype(BF16)
                units.append(u)

    for u in units:
        u["s_old"] = s_scr[u["idx"]]
        u["from_state"] = _mm(u["st_lhs"], u["s_old"], "nt")
    for u in units:
        gm = _mm(u["lhs"], u["rhs"], "nt")
        u["y"] = jnp.where(u["strict2"], -gm[:c, :2 * c], 0.0)
        u["a_ak"] = jnp.where(u["strict2"], gm[:c, 2 * c:], 0.0).astype(BF16)
        u["r_mat"] = jnp.concatenate([jnp.where(u["incl2"], -gm[c:, :2 * c], 0.0),
                                      jnp.where(u["incl2"], gm[c:, 2 * c:], 0.0)], axis=1).astype(BF16)
        u["inv"] = eye2 + u["y"]
        u["bd"] = blockdiag(u["y"])
    for u in units:
        u["rhs_e"] = u["from_state"][:c] + _mm(u["a_ak"], u["v_blk"])
    for _ in range(levels):
        for u in units:
            u["y"] = _mm(u["y"], u["bd"])
            u["bd"] = blockdiag(u["y"])
        for u in units:
            u["inv"] = u["inv"] + _mm(u["inv"], u["bd"])
    for u in units:
        u["e"] = _mm(u["inv"], by_head(u["rhs_e"]))
    for u in units:
        ev = jnp.concatenate([by_head(u["e"]), by_head(u["v"])], axis=0)
        u["y_ref"][u["b"], :, u["sl"]] = u["from_state"][c:] + _mm(u["r_mat"], ev)
    for u in units:
        upd = _mm(jnp.concatenate([-u["e"], u["v"]], axis=0), u["bk_d"], "tn")
        s_scr[u["idx"]] = u["s_old"] * u["keep"] + jnp.where(same_head, upd, 0.0)


def _rwkv_scan(r, v, kk, per_dir, ctx_len):
    nb, t, width = r.shape
    chunk = RW_CHUNK
    nsteps, nctx = t // chunk, ctx_len // chunk
    fwd, bwd = _chunk_maps(nsteps, nctx)
    spec = lambda cm: pl.BlockSpec((nb, chunk, width), lambda i: (0, cm(i), 0))
    args = []
    for d in range(2):
        lw, bb, kt = per_dir[d]
        args += [r, lw, kt, v, kk, bb]
    kern = functools.partial(_rwkv_kernel, nb=nb, npair=width // (2 * RW_HEAD), chunk=chunk)
    return pl.pallas_call(
        kern,
        grid=(nsteps,),
        in_specs=[spec(fwd)] * 6 + [spec(bwd)] * 6,
        out_specs=[spec(fwd), spec(bwd)],
        out_shape=[jax.ShapeDtypeStruct((nb, t, width), F32)] * 2,
        scratch_shapes=[pltpu.VMEM((2 * nb * (width // (2 * RW_HEAD)), 2 * RW_HEAD, 2 * RW_HEAD), F32)],
        compiler_params=_cparams(("arbitrary",)),
        name="rwkv_scan",
    )(*args)


def _rwkv_mixer(pa, p, ctx_len):
    width = p['rw_k_k'].shape[0]
    nh = width // RW_HEAD
    mu = p['rw_mu']
    pp = pa + mu[0] * (_seq_shift(pa, ctx_len, -1) - pa) + mu[1] * (_seq_shift(pa, ctx_len, 1) - pa)
    r, k, v = pp[..., :width], pp[..., width:2 * width], pp[..., 2 * width:3 * width]
    o = 3 * width
    wd = pp[..., o:o + RW_DECAY_RANK]
    ad = pp[..., o + RW_DECAY_RANK:o + RW_DECAY_RANK + RW_ICLR_RANK]
    gd = pp[..., o + RW_DECAY_RANK + RW_ICLR_RANK:]
    g = jax.nn.sigmoid(gd) @ p['rw_g_up']
    kk = _l2n(k * p['rw_k_k'], nh)
    per_dir, kts = [], []
    for d in range(2):
        logw = -RW_DECAY_SCALE * jax.nn.sigmoid(p['rw_w0'][d] + jnp.tanh(wd) @ p['rw_w_up'][d])
        a = jax.nn.sigmoid(p['rw_a0'][d] + ad @ p['rw_a_up'][d])
        kt = k * (1.0 + (a - 1.0) * p['rw_k_a'])
        per_dir.append((logw, kk * a, kt))
        kts.append(kt)
    yf, yb = _rwkv_scan(r, v, kk, per_dir, ctx_len)
    y = _head_layer_norm(yf + yb, nh, RW_GN_EPS) * p['rw_ln_w'] + p['rw_ln_b']
    b, t, _ = y.shape
    bh = lambda x: x.reshape(b, t, nh, RW_HEAD)
    bonus = sum(jnp.sum(bh(r) * bh(kt) * p['rw_r_k'], axis=-1, keepdims=True) * bh(v) for kt in kts)
    return (y + bonus.reshape(y.shape)) * g


def _attn_kernel(q_ref, k_ref, v_ref, o_ref, *, ctx_tiles, ctx_len, group):
    qi = pl.program_id(2)

    def run(keys, vals):
        for g in range(group):
            sl = slice(g * AT_HEAD, (g + 1) * AT_HEAD)
            s = lax.dot_general(q_ref[0, :, sl], keys, _DIMS["nt"], preferred_element_type=F32)
            p = jnp.exp(s - jnp.max(s, axis=-1, keepdims=True))
            o = jnp.dot(p.astype(BF16), vals, preferred_element_type=F32)
            o_ref[0, :, sl] = o / jnp.sum(p, axis=-1, keepdims=True)

    @pl.when(qi < ctx_tiles)
    def _():
        run(k_ref[0, :ctx_len, :], v_ref[0, :ctx_len, :])

    @pl.when(qi >= ctx_tiles)
    def _():
        run(k_ref[0], v_ref[0])


def _attention(q, k, v, ctx_len):
    nb, t, qw = q.shape
    hkv = k.shape[-1] // AT_HEAD
    group = qw // AT_HEAD // hkv
    tq = AT_TQ
    assert ctx_len % tq == 0 and t % tq == 0
    kern = functools.partial(_attn_kernel, ctx_tiles=ctx_len // tq, ctx_len=ctx_len, group=group)
    return pl.pallas_call(
        kern,
        grid=(nb, hkv, t // tq),
        in_specs=[pl.BlockSpec((1, tq, group * AT_HEAD), lambda b, h, i: (b, i, h)),
                  pl.BlockSpec((1, t, AT_HEAD), lambda b, h, i: (b, 0, h)),
                  pl.BlockSpec((1, t, AT_HEAD), lambda b, h, i: (b, 0, h))],
        out_specs=pl.BlockSpec((1, tq, group * AT_HEAD), lambda b, h, i: (b, i, h)),
        out_shape=jax.ShapeDtypeStruct((nb, t, qw), F32),
        compiler_params=_cparams(("parallel", "parallel", "arbitrary")),
        name="gqa",
    )(q, k, v)


def _rope_tables(t, ctx_len):
    n_lat = t - ctx_len
    pos = jnp.arange(n_lat)
    row = (pos // GRID_W).astype(F32)
    col = (pos % GRID_W).astype(F32)
    axis_dim = AT_HEAD // 2
    inv_freq = ROPE_THETA ** (-jnp.arange(0, axis_dim, 2, dtype=F32) / axis_dim)
    ang_r = row[:, None] * inv_freq[None, :]
    ang_c = col[:, None] * inv_freq[None, :]
    cos = jnp.concatenate([jnp.cos(ang_r)] * 2 + [jnp.cos(ang_c)] * 2, axis=-1)
    sin = jnp.concatenate([-jnp.sin(ang_r), jnp.sin(ang_r), -jnp.sin(ang_c), jnp.sin(ang_c)], axis=-1)
    cos = jnp.concatenate([jnp.ones((ctx_len, AT_HEAD), F32), cos], axis=0)
    sin = jnp.concatenate([jnp.zeros((ctx_len, AT_HEAD), F32), sin], axis=0)
    return cos, sin


def _attn_mixer(pd, at_q_norm, at_k_norm, ctx_len, rope):
    nb, t, w = pd.shape
    kvw = AT_KV_HEADS * AT_HEAD
    qw = w - 2 * kvw
    cos, sin = rope

    def prep(x, g):
        xh = x.reshape(nb, t, -1, AT_HEAD)
        xh = xh * lax.rsqrt(jnp.mean(xh * xh, axis=-1, keepdims=True) + NORM_EPS) * g
        sw = jnp.flip(xh.reshape(nb, t, -1, 2, 2, AT_HEAD // 4), axis=-2).reshape(xh.shape)
        return (xh * cos[:, None, :] + sw * sin[:, None, :]).reshape(x.shape)

    q = prep(pd[..., :qw], at_q_norm) * AT_HEAD ** -0.5
    k = prep(pd[..., qw:qw + kvw], at_k_norm)
    v = pd[..., qw + kvw:]
    return _attention(q.astype(BF16), k.astype(BF16), v.astype(BF16), ctx_len)


PREP_TM = 256
HALO = 8


def _halo_specs(tm, width, col, t):
    per = tm // HALO
    last = t // HALO - 1
    prev = pl.BlockSpec((1, HALO, width), lambda b, i: (b, jnp.maximum(i * per - 1, 0), col))
    nxt = pl.BlockSpec((1, HALO, width), lambda b, i: (b, jnp.minimum((i + 1) * per, last), col))
    return prev, nxt


def _neighbours(x, prev_ref, next_ref, start, ctx_len, t):
    tm = x.shape[0]
    row = lax.broadcasted_iota(jnp.int32, x.shape, 0)
    has_prev = jnp.logical_and(start != 0, start != ctx_len)
    has_next = jnp.logical_and(start + tm != ctx_len, start + tm != t)
    prev_row = jnp.where(has_prev, prev_ref[0, HALO - 1:HALO, :], 0.0)
    next_row = jnp.where(has_next, next_ref[0, 0:1, :], 0.0)
    x_prev = jnp.where(row == 0, prev_row, pltpu.roll(x, 1, 0))
    x_next = jnp.where(row == tm - 1, next_row, pltpu.roll(x, tm - 1, 0))
    return x_prev, x_next


def _rwkv_prep_kernel(x_ref, xp_ref, xn_ref, mu_ref, wa_ref, gup_ref, w0_ref, a0_ref, kk_ref, ka_ref, rk_ref, hs_ref,
                      r_o, v_o, kkn_o, lwf_o, bf_o, ktf_o, lwb_o, bb_o, ktb_o, g_o, bonus_o, *, ctx_len, t, width):
    start = pl.program_id(1) * x_ref.shape[1]
    x = x_ref[0]
    x_prev, x_next = _neighbours(x, xp_ref, xn_ref, start, ctx_len, t)
    pp = x + mu_ref[0:1, :] * (x_prev - x) + mu_ref[1:2, :] * (x_next - x)
    w = width
    r, k, v = pp[:, :w], pp[:, w:2 * w], pp[:, 2 * w:3 * w]
    low = pp[:, 3 * w:3 * w + LANES]
    lane = lax.broadcasted_iota(jnp.int32, low.shape, 1)
    low = jnp.where(lane < RW_DECAY_RANK, jnp.tanh(low), low)
    g_o[0] = _mm(jax.nn.sigmoid(pp[:, 3 * w + LANES:]), gup_ref[...])
    hs = hs_ref[...]
    kx = k * kk_ref[...]
    kk = kx * lax.rsqrt(_mm(kx * kx, hs, "nn", 3, 1) + 1e-6)
    r_o[0], v_o[0], kkn_o[0] = r, v, kk
    kt_sum = None
    for d, (lw_o, b_o, kt_o) in enumerate(((lwf_o, bf_o, ktf_o), (lwb_o, bb_o, ktb_o))):
        up = _mm(low, wa_ref[d])
        lw_o[0] = -RW_DECAY_SCALE * jax.nn.sigmoid(w0_ref[d:d + 1, :] + up[:, :w])
        a = jax.nn.sigmoid(a0_ref[d:d + 1, :] + up[:, w:])
        kt = k * (1.0 + (a - 1.0) * ka_ref[...])
        b_o[0] = kk * a
        kt_o[0] = kt
        kt_sum = kt if kt_sum is None else kt_sum + kt
    bonus_o[0] = _mm(r * kt_sum * rk_ref[...], hs, "nn", 3, 1) * v


def _rwkv_prep(proj, col, p, hs, ctx_len):
    nb, t, _ = proj.shape
    width = p['rw_k_k'].shape[0]
    cols = p['rw_mu'].shape[-1]
    tm = PREP_TM
    assert ctx_len % tm == 0 and t % tm == 0 and cols == 3 * width + LANES + RW_GATE_RANK
    rank = RW_DECAY_RANK
    wa = jnp.zeros((2, LANES, 2 * width), F32)
    wa = wa.at[:, :rank, :width].set(p['rw_w_up']).at[:, rank:, width:].set(p['rw_a_up'])
    params = [p['rw_mu'], wa, p['rw_g_up'], p['rw_w0'], p['rw_a0'], p['rw_k_k'].reshape(1, width),
              p['rw_k_a'].reshape(1, width), p['rw_r_k'].reshape(1, width), hs]
    full = lambda a: pl.BlockSpec(a.shape, lambda b, i: (0,) * a.ndim)
    prev, nxt = _halo_specs(tm, cols, col, t)
    out_spec = pl.BlockSpec((1, tm, width), lambda b, i: (b, i, 0))
    return pl.pallas_call(
        functools.partial(_rwkv_prep_kernel, ctx_len=ctx_len, t=t, width=width),
        grid=(nb, t // tm),
        in_specs=[pl.BlockSpec((1, tm, cols), lambda b, i: (b, i, col)), prev, nxt] + [full(a) for a in params],
        out_specs=[out_spec] * 11,
        out_shape=[jax.ShapeDtypeStruct((nb, t, width), F32)] * 11,
        compiler_params=_cparams(("parallel", "parallel")),
        name="rwkv_prep",
    )(proj, proj, proj, *params)


def _gdn_prep_kernel(*refs, ctx_len, t, nh):
    ins, conv_ref, outs = refs[:9], refs[9], refs[10:]
    start = pl.program_id(1) * ins[0].shape[1]
    width = nh * GD_HEAD
    for part in range(3):
        x_ref, xp_ref, xn_ref = ins[3 * part:3 * part + 3]
        x = x_ref[0]
        x_prev, x_next = _neighbours(x, xp_ref, xn_ref, start, ctx_len, t)
        w = conv_ref[:, part * width:(part + 1) * width]
        y = x_prev * w[0:1, :] + x * w[1:2, :] + x_next * w[2:3, :]
        y = y * jax.nn.sigmoid(y)
        if part == 2:
            outs[part][0] = y
            continue
        scale = GD_HEAD ** -0.5 if part == 0 else 1.0
        for h in range(nh):
            seg = y[:, h * GD_HEAD:(h + 1) * GD_HEAD]
            n = seg * lax.rsqrt(jnp.sum(seg * seg, axis=-1, keepdims=True) + 1e-6)
            outs[part][0, :, h * GD_HEAD:(h + 1) * GD_HEAD] = n * scale


def _gdn_prep(proj, col0, gd_conv, nh, ctx_len):
    nb, t, _ = proj.shape
    width = nh * GD_HEAD
    tm = PREP_TM
    assert gd_conv.shape[0] == 3 and ctx_len % tm == 0 and t % tm == 0
    in_specs, args = [], []
    for part in range(3):
        prev, nxt = _halo_specs(tm, width, col0 + part, t)
        in_specs += [pl.BlockSpec((1, tm, width), lambda b, i, c=col0 + part: (b, i, c)), prev, nxt]
        args += [proj, proj, proj]
    out_spec = pl.BlockSpec((1, tm, width), lambda b, i: (b, i, 0))
    return pl.pallas_call(
        functools.partial(_gdn_prep_kernel, ctx_len=ctx_len, t=t, nh=nh),
        grid=(nb, t // tm),
        in_specs=in_specs + [pl.BlockSpec(gd_conv.shape, lambda b, i: (0, 0))],
        out_specs=[out_spec] * 3,
        out_shape=[jax.ShapeDtypeStruct((nb, t, width), F32)] * 3,
        compiler_params=_cparams(("parallel", "parallel")),
        name="gdn_prep",
    )(*args, gd_conv)


def _attn_prep_kernel(q_ref, k_ref, v_ref, cos_ref, sin_ref, qg_ref, kg_ref, q_o, k_o, v_o):
    cos, sin = cos_ref[...], sin_ref[...]
    lane = lax.broadcasted_iota(jnp.int32, cos.shape, 1)
    low_half = (lane & (AT_HEAD // 2 - 1)) < AT_HEAD // 4

    def prep(seg, gain):
        n = seg * lax.rsqrt(jnp.mean(seg * seg, axis=-1, keepdims=True) + NORM_EPS) * gain
        partner = jnp.where(low_half, pltpu.roll(n, AT_HEAD - AT_HEAD // 4, 1), pltpu.roll(n, AT_HEAD // 4, 1))
        return n * cos + partner * sin

    for h in range(q_ref.shape[2] // AT_HEAD):
        sl = slice(h * AT_HEAD, (h + 1) * AT_HEAD)
        q_o[0, :, sl] = (prep(q_ref[0, :, sl], qg_ref[...]) * AT_HEAD ** -0.5).astype(BF16)
    for h in range(k_ref.shape[2] // AT_HEAD):
        sl = slice(h * AT_HEAD, (h + 1) * AT_HEAD)
        k_o[0, :, sl] = prep(k_ref[0, :, sl], kg_ref[...]).astype(BF16)
    v_o[0] = v_ref[0].astype(BF16)


def _attn_prep(proj, col_q, qw, kvw, at_q_norm, at_k_norm, rope):
    nb, t, _ = proj.shape
    tm = PREP_TM
    cos, sin = rope
    col_k = col_q * qw // kvw + qw // kvw
    blk = lambda w, c: pl.BlockSpec((1, tm, w), lambda b, i: (b, i, c))
    tab = pl.BlockSpec((tm, AT_HEAD), lambda b, i: (i, 0))
    vec = pl.BlockSpec((1, AT_HEAD), lambda b, i: (0, 0))
    out = lambda w: pl.BlockSpec((1, tm, w), lambda b, i: (b, i, 0))
    return pl.pallas_call(
        _attn_prep_kernel,
        grid=(nb, t // tm),
        in_specs=[blk(qw, col_q), blk(kvw, col_k), blk(kvw, col_k + 1), tab, tab, vec, vec],
        out_specs=[out(qw), out(kvw), out(kvw)],
        out_shape=[jax.ShapeDtypeStruct((nb, t, qw), BF16), jax.ShapeDtypeStruct((nb, t, kvw), BF16),
                   jax.ShapeDtypeStruct((nb, t, kvw), BF16)],
        compiler_params=_cparams(("parallel", "parallel")),
        name="attn_prep",
    )(proj, proj, proj, cos, sin, at_q_norm.reshape(1, AT_HEAD), at_k_norm.reshape(1, AT_HEAD))


def _mix_post_kernel(ryf, ryb, rbonus, rg, myf, myb, mo, gyf, gyb, gg, ay, lnw, lnb, mlg, gdg, hs_ref, o_ref, *, gw):
    y = ryf[0] + ryb[0]
    hs = hs_ref[...]
    inv_n = 1.0 / RW_HEAD
    mean = _mm(y, hs, "nn", 3, 1) * inv_n
    cen = y - mean
    var = _mm(cen * cen, hs, "nn", 3, 1) * inv_n
    o_ref[0, :, 0:gw] = ((cen * lax.rsqrt(var + RW_GN_EPS) * lnw[...] + lnb[...] + rbonus[0]) * rg[0]).astype(BF16)
    y = myf[0] + myb[0]
    for h in range(gw // ML_HEAD):
        sl = slice(h * ML_HEAD, (h + 1) * ML_HEAD)
        seg = y[:, sl]
        cen = seg - jnp.mean(seg, axis=-1, keepdims=True)
        var = jnp.mean(cen * cen, axis=-1, keepdims=True)
        out = cen * lax.rsqrt(var + NORM_EPS) * mlg[:, sl] * jax.nn.sigmoid(mo[0, :, sl])
        o_ref[0, :, gw + h * ML_HEAD:gw + (h + 1) * ML_HEAD] = out.astype(BF16)
    y = gyf[0] + gyb[0]
    for h in range(gw // GD_HEAD):
        sl = slice(h * GD_HEAD, (h + 1) * GD_HEAD)
        seg = y[:, sl]
        gate = gg[0, :, sl]
        out = seg * lax.rsqrt(jnp.mean(seg * seg, axis=-1, keepdims=True) + NORM_EPS) * gdg[...] * (gate * jax.nn.sigmoid(gate))
        o_ref[0, :, 2 * gw + h * GD_HEAD:2 * gw + (h + 1) * GD_HEAD] = out.astype(BF16)
    o_ref[0, :, 3 * gw:4 * gw] = ay[0].astype(BF16)


def _mix_post(rw, ml, gd, at, proj, col_o, col_g, params, hs):
    nb, t, gw = at.shape
    tm = PREP_TM
    row = lambda c=0: pl.BlockSpec((1, tm, gw), lambda b, i: (b, i, c))
    full = lambda a: pl.BlockSpec(a.shape, lambda b, i: (0,) * a.ndim)
    ln_w, ln_b, ml_g, gd_g = params
    consts = [ln_w.reshape(1, gw), ln_b.reshape(1, gw), ml_g.reshape(1, gw), gd_g.reshape(1, GD_HEAD), hs]
    return pl.pallas_call(
        functools.partial(_mix_post_kernel, gw=gw),
        grid=(nb, t // tm),
        in_specs=[row()] * 4 + [row(), row(), row(col_o)] + [row(), row(), row(col_g)] + [row()]
                 + [full(a) for a in consts],
        out_specs=pl.BlockSpec((1, tm, 4 * gw), lambda b, i: (b, i, 0)),
        out_shape=jax.ShapeDtypeStruct((nb, t, 4 * gw), BF16),
        compiler_params=_cparams(("parallel", "parallel")),
        name="mix_post",
    )(*rw, ml[0], ml[1], proj, gd[0], gd[1], proj, at, *consts)


MOD_TN = 1024
MOD_KC = 256


def _mod_kernel(s_ref, w_ref, b_ref, o_ref, *, ncond):
    d = w_ref.shape[1]
    acc = [jnp.zeros((1, w_ref.shape[2]), F32) for _ in range(ncond)]
    for kc in range(d // MOD_KC):
        rows = slice(kc * MOD_KC, (kc + 1) * MOD_KC)
        w = w_ref[0, rows, :]
        for j in range(ncond):
            acc[j] = acc[j] + jnp.sum(w * s_ref[rows, j:j + 1], axis=0, keepdims=True)
    bias = b_ref[0]
    pad = [jnp.zeros_like(bias)] * (o_ref.shape[1] - ncond)
    o_ref[0] = jnp.concatenate([a + bias for a in acc] + pad, axis=0)


def _modulation(conds, ada_w, ada_b):
    depth, d, n = ada_w.shape
    ncond = conds.shape[0]
    s_cols = jnp.pad(jax.nn.silu(conds).T, ((0, 0), (0, 8 - ncond)))
    return pl.pallas_call(
        functools.partial(_mod_kernel, ncond=ncond),
        grid=(depth, n // MOD_TN),
        in_specs=[pl.BlockSpec((d, 8), lambda l, j: (0, 0)),
                  pl.BlockSpec((1, d, MOD_TN), lambda l, j: (l, 0, j)),
                  pl.BlockSpec((1, 1, MOD_TN), lambda l, j: (l, 0, j))],
        out_specs=pl.BlockSpec((1, 8, MOD_TN), lambda l, j: (l, 0, j)),
        out_shape=jax.ShapeDtypeStruct((depth, 8, n), F32),
        compiler_params=_cparams(("parallel", "parallel")),
        name="adaln_modulation",
    )(s_cols, ada_w, ada_b.reshape(depth, 1, n))


PROJ_TM = 544
PROJ_TN = 1792


def _modulated_norm(x, gain, mod_ref, shift_row, row0, ctx_len):
    y = x * lax.rsqrt(jnp.mean(x * x, axis=-1, keepdims=True) + NORM_EPS) * gain
    rows = row0 + lax.broadcasted_iota(jnp.int32, (x.shape[0], 1), 0)
    is_ctx = rows < ctx_len
    shift = jnp.where(is_ctx, mod_ref[0, 0, shift_row:shift_row + 1, :], mod_ref[0, 1, shift_row:shift_row + 1, :])
    scale = jnp.where(is_ctx, mod_ref[0, 0, shift_row + 1:shift_row + 2, :],
                      mod_ref[0, 1, shift_row + 1:shift_row + 2, :])
    return y * (1.0 + scale) + shift


def _norm_proj_kernel(x_ref, g_ref, mod_ref, w_ref, o_ref, h_scr, *, tiles_per_seq, ctx_len):
    row0 = (pl.program_id(0) % tiles_per_seq) * x_ref.shape[0]

    @pl.when(pl.program_id(1) == 0)
    def _():
        h_scr[...] = _modulated_norm(x_ref[...], g_ref[...], mod_ref, 0, row0, ctx_len).astype(BF16)

    o_ref[...] = jnp.dot(h_scr[...], w_ref[...], preferred_element_type=F32)


def _norm_proj(x, gain, mod, w, ctx_len):
    nb, t, d = x.shape
    n = w.shape[1]
    tm, tn = _pick_tile(t, PROJ_TM), PROJ_TN
    assert n % tn == 0
    tps = t // tm
    out = pl.pallas_call(
        functools.partial(_norm_proj_kernel, tiles_per_seq=tps, ctx_len=ctx_len),
        grid=(nb * tps, n // tn),
        in_specs=[pl.BlockSpec((tm, d), lambda i, j: (i, 0)),
                  pl.BlockSpec((1, d), lambda i, j: (0, 0)),
                  pl.BlockSpec((1, 2, 8, d), lambda i, j: (i // tps, 0, 0, 0)),
                  pl.BlockSpec((d, tn), lambda i, j: (0, j))],
        out_specs=pl.BlockSpec((tm, tn), lambda i, j: (i, j)),
        out_shape=jax.ShapeDtypeStruct((nb * t, n), F32),
        scratch_shapes=[pltpu.VMEM((tm, d), BF16)],
        compiler_params=_cparams(("parallel", "arbitrary")),
        name="norm_in_proj",
    )(x.reshape(nb * t, d), gain.reshape(1, d), mod, w)
    return out.reshape(nb, t, n)


OUT_TM = 272


def _out_proj_kernel(y_ref, w_ref, x_ref, g_ref, mod_ref, wr_hi, wr_lo,
                     xo_ref, h_ref, aff_ref, *, tiles_per_seq, ctx_len, n_experts):
    acc = jnp.dot(y_ref[...], w_ref[...], preferred_element_type=F32)
    tm = x_ref.shape[0]
    row0 = (pl.program_id(0) % tiles_per_seq) * tm
    rows = row0 + lax.broadcasted_iota(jnp.int32, (tm, 1), 0)
    gate = jnp.where(rows < ctx_len, mod_ref[0, 0, 2:3, :], mod_ref[0, 1, 2:3, :])
    x_new = x_ref[...] + gate * acc
    xo_ref[...] = x_new
    h = _modulated_norm(x_new, g_ref[...], mod_ref, 3, row0, ctx_len)
    h_ref[...] = h
    h_hi = h.astype(BF16)
    h_lo = (h - h_hi.astype(F32)).astype(BF16)
    logits = (jnp.dot(h_hi, wr_hi[...], preferred_element_type=F32)
              + jnp.dot(h_lo, wr_hi[...], preferred_element_type=F32)
              + jnp.dot(h_hi, wr_lo[...], preferred_element_type=F32))
    lane = lax.broadcasted_iota(jnp.int32, logits.shape, 1)
    logits = jnp.where(lane < n_experts, logits, -jnp.inf)
    p = jnp.exp(logits - jnp.max(logits, axis=-1, keepdims=True))
    aff_ref[...] = p / jnp.sum(p, axis=-1, keepdims=True)


def _out_proj(ymix, w_out, x, gain, mod, w_router, ctx_len):
    nb, t, d = x.shape
    mw = ymix.shape[-1]
    ne = w_router.shape[1]
    tm = _pick_tile(t, OUT_TM)
    tps = t // tm
    wr =jnp.pad(w_router, ((0, 0), (0, LANES - ne)))
    wr_hi = wr.astype(BF16)
    wr_lo = (wr - wr_hi.astype(F32)).astype(BF16)
    row = lambda w: pl.BlockSpec((tm, w), lambda i: (i, 0))
    full = lambda a: pl.BlockSpec(a.shape, lambda i: (0,) * a.ndim)
    xo, h, aff = pl.pallas_call(
        functools.partial(_out_proj_kernel, tiles_per_seq=tps, ctx_len=ctx_len, n_experts=ne),
        grid=(nb * tps,),
        in_specs=[row(mw), full(w_out), row(d), pl.BlockSpec((1, d), lambda i: (0, 0)),
                  pl.BlockSpec((1, 2, 8, d), lambda i: (i // tps, 0, 0, 0)), full(wr_hi), full(wr_lo)],
        out_specs=[row(d), row(d), row(LANES)],
        out_shape=[jax.ShapeDtypeStruct((nb * t, d), F32), jax.ShapeDtypeStruct((nb * t, d), F32),
                   jax.ShapeDtypeStruct((nb * t, LANES), F32)],
        compiler_params=_cparams(("parallel",)),
        name="out_proj_norm_router",
    )(ymix.reshape(nb * t, mw), w_out, x.reshape(nb * t, d), gain.reshape(1, d), mod, wr_hi, wr_lo)
    return xo, h, aff.reshape(nb, t, LANES)


ROUTE_BLK = 256
ROUTE_ROWS = 8


def _route_kernel(aff_ref, *refs, segments, n_experts):
    out_refs, t_scr = refs[:-1], refs[-1]
    for (start, length), o_ref in zip(segments, out_refs):
        cap = EC_FACTOR * length // n_experts
        capp = o_ref.shape[2]
        x = aff_ref[0, start:start + length, :]

        def refine(i, thr):
            cand = thr | jnp.left_shift(jnp.int32(1), 30 - i)
            cnt = jnp.sum(jnp.where(x >= pltpu.bitcast(cand, F32), 1.0, 0.0), axis=0, keepdims=True)
            return jnp.where(cnt >= cap, cand, thr)

        thr = lax.fori_loop(0, 31, refine, jnp.zeros((1, LANES), jnp.int32))
        thr_f = pltpu.bitcast(thr, F32)
        need = cap - jnp.sum(jnp.where(x > thr_f, 1.0, 0.0), axis=0, keepdims=True)
        blk = min(ROUTE_BLK, length)
        before = (lax.broadcasted_iota(jnp.int32, (blk, blk), 1)
                  < lax.broadcasted_iota(jnp.int32, (blk, blk), 0)).astype(BF16)
        lane = lax.broadcasted_iota(jnp.int32, (blk, LANES), 1)
        slot_ids = lax.broadcasted_iota(jnp.int32, (capp, blk), 0).astype(F32)
        src = lax.broadcasted_iota(jnp.int32, (LANES, LANES), 0)
        dst = lax.broadcasted_iota(jnp.int32, (LANES, LANES), 1)
        assert start % blk == 0 and length % blk == 0 and blk == t_scr.shape[2]
        o_ref[0] = jnp.zeros((LANES, capp), F32)

        def block(kb, carry, start=start, blk=blk, thr_f=thr_f, need=need, before=before, lane=lane,
                  slot_ids=slot_ids, src=src, dst=dst, o_ref=o_ref):
            carry_tied, carry_sel = carry
            xb = aff_ref[0, pl.ds(pl.multiple_of(start + kb * blk, blk), blk), :]
            above_b = xb > thr_f
            tied_b = xb == thr_f
            tied_f = jnp.where(tied_b, 1.0, 0.0)
            tied_before = _mm(before, tied_f) + carry_tied
            sel_f = jnp.where(jnp.logical_or(above_b, jnp.logical_and(tied_b, tied_before < need)), 1.0, 0.0)
            slot = _mm(before, sel_f) + carry_sel
            tok = kb * blk + lax.broadcasted_iota(jnp.int32, (blk, LANES), 0)
            vals = jnp.where((lane & 7) == 0, (tok >> 6).astype(F32),
                             jnp.where((lane & 7) == 1, (tok & 63).astype(F32), 0.0))
            for r, piece in enumerate(_split(xb, 3)):
                to_lane = jnp.logical_and(dst == src * ROUTE_ROWS + 2 + r, src < n_experts)
                vals = vals + _mm(piece, to_lane.astype(BF16))
            t_scr[0] = slot.T
            t_scr[1] = sel_f.T
            t_scr[2] = vals.T

            def expert(e, _):
                row_slot = t_scr[0, pl.ds(e, 1), :]
                row_sel = t_scr[1, pl.ds(e, 1), :]
                rows = pl.ds(pl.multiple_of(e * ROUTE_ROWS, ROUTE_ROWS), ROUTE_ROWS)
                hit = jnp.logical_and(row_sel > 0.5, row_slot == slot_ids)
                o_ref[0, rows, :] += lax.dot_general(t_scr[2, rows, :], jnp.where(hit, 1.0, 0.0), _DIMS["nt"],
                                                     preferred_element_type=F32)
                return 0

            lax.fori_loop(0, n_experts, expert, 0, unroll=2)
            return (carry_tied + jnp.sum(tied_f, axis=0, keepdims=True),
                    carry_sel + jnp.sum(sel_f, axis=0, keepdims=True))

        zero = jnp.zeros((1, LANES), F32)
        lax.fori_loop(0, length // blk, block, (zero, zero))


def _route(aff, segments, n_experts):
    nb, t, _ = aff.shape
    caps = [EC_FACTOR * length // n_experts for _, length in segments]
    capps = [-(-c // LANES) * LANES for c in caps]
    outs = pl.pallas_call(
        functools.partial(_route_kernel, segments=tuple(segments), n_experts=n_experts),
        grid=(nb,),
        in_specs=[pl.BlockSpec((1, t, LANES), lambda b: (b, 0, 0))],
        out_specs=[pl.BlockSpec((1, LANES, cp), lambda b: (b, 0, 0)) for cp in capps],
        out_shape=[jax.ShapeDtypeStruct((nb, LANES, cp), F32) for cp in capps],
        scratch_shapes=[pltpu.VMEM((3, LANES, ROUTE_BLK), F32)],
        compiler_params=_cparams(("parallel",)),
        name="expert_choice_route",
    )(aff)
    picks = []
    for o, cap in zip(outs, caps):
        v = o.reshape(nb, LANES // ROUTE_ROWS, ROUTE_ROWS, -1)[:, :n_experts, :, :cap]
        tok = jnp.round(v[:, :, 0] * 64.0 + v[:, :, 1]).astype(jnp.int32)
        picks.append((tok, v[:, :, 2] + v[:, :, 3] + v[:, :, 4]))
    return picks


FFN_TF = 256


def _ffn_kernel(x_ref, w1_ref, w3_ref, w2_ref, o_ref):
    @pl.when(pl.program_id(1) == 0)
    def _():
        o_ref[...] = jnp.zeros_like(o_ref)

    xs = x_ref[0]
    a = jnp.dot(xs, w1_ref[0].astype(BF16), preferred_element_type=F32)
    b = jnp.dot(xs, w3_ref[0].astype(BF16), preferred_element_type=F32)
    hid = (a * jax.nn.sigmoid(a) * b).astype(BF16)
    o_ref[0] += jnp.dot(hid, w2_ref[0].astype(BF16), preferred_element_type=F32)


def _expert_ffn(xs, w1, w3, w2):
    ne, r, d = xs.shape
    f = w1.shape[2]
    tf = FFN_TF
    return pl.pallas_call(
        _ffn_kernel,
        grid=(ne, f // tf),
        in_specs=[pl.BlockSpec((1, r, d), lambda e, j: (e, 0, 0)),
                  pl.BlockSpec((1, d, tf), lambda e, j: (e, 0, j)),
                  pl.BlockSpec((1, d, tf), lambda e, j: (e, 0, j)),
                  pl.BlockSpec((1, tf, d), lambda e, j: (e, j, 0))],
        out_specs=pl.BlockSpec((1, r, d), lambda e, j: (e, 0, 0)),
        out_shape=jax.ShapeDtypeStruct((ne, r, d), F32),
        compiler_params=_cparams(("parallel", "arbitrary")),
        name="expert_ffn",
    )(xs, w1, w3, w2)


def _moe_ffn_kernel(idx_ref, gate_ref, g2_ref, w1_ref, w3_ref, w2_ref, h_hbm, x_hbm, xo_hbm,
                    xg, xs_bf, acc, rows, gsem, rsem, wsem, *, groups):
    del x_hbm
    e, j = pl.program_id(0), pl.program_id(1)
    ne, nj = pl.num_programs(0), pl.num_programs(1)
    r_tot = acc.shape[0]
    per = r_tot // nj

    def row_copy(src, dst, s, d_, sem):
        return pltpu.make_async_copy(src.at[pl.ds(s, 1), :], dst.at[pl.ds(d_, 1), :], sem)

    def for_rows(lo, n, fn):
        def body(i, c):
            fn(lo + i)
            return c
        lax.fori_loop(0, n, body, 0, unroll=8)

    def gather_start(expert, lo, n):
        for_rows(lo, n, lambda r: row_copy(h_hbm, xg, idx_ref[expert, r], r, gsem).start())

    def gather_wait(lo, n):
        for_rows(lo, n, lambda r: row_copy(h_hbm, xg, 0, r, gsem).wait())

    @pl.when(jnp.logical_and(e == 0, j == 0))
    def _():
        gather_start(0, 0, r_tot)
        gather_wait(0, r_tot)

    @pl.when(j == 0)
    def _():
        xs_bf[...] = xg[...].astype(BF16)

    @pl.when(jnp.logical_and(j == 0, e > 0))
    def _():
        for_rows(0, r_tot, lambda r: row_copy(rows, xo_hbm, r, idx_ref[e - 1, r], wsem).start())

    @pl.when(e + 1 < ne)
    def _():
        gather_start(e + 1, j * per, per)

    @pl.when(j == nj - 1)
    def _():
        @pl.when(e > 0)
        def _():
            for_rows(0, r_tot, lambda r: row_copy(rows, xo_hbm, r, 0, wsem).wait())
        for_rows(0, r_tot, lambda r: row_copy(xo_hbm, rows, idx_ref[e, r], r, rsem).start())

    xs = xs_bf[...]
    a = jnp.dot(xs, w1_ref[0, 0].astype(BF16), preferred_element_type=F32)
    b = jnp.dot(xs, w3_ref[0, 0].astype(BF16), preferred_element_type=F32)
    hid = (a * jax.nn.sigmoid(a) * b).astype(BF16)
    part = jnp.dot(hid, w2_ref[0, 0].astype(BF16), preferred_element_type=F32)

    @pl.when(j == 0)
    def _():
        acc[...] = part

    @pl.when(j > 0)
    def _():
        acc[...] += part

    @pl.when(j == nj - 1)
    def _():
        for_rows(0, r_tot, lambda r: row_copy(xo_hbm, rows, 0, r, rsem).wait())
        for g, (lo, n) in enumerate(groups):
            rows[lo:lo + n, :] += (gate_ref[0, lo:lo + n, :] * acc[lo:lo + n, :]) * g2_ref[g:g + 1, :]

        @pl.when(e + 1 < ne)
        def _():
            gather_wait(0, r_tot)

        @pl.when(e == ne - 1)
        def _():
            for_rows(0, r_tot, lambda r: row_copy(rows, xo_hbm, r, idx_ref[e, r], wsem).start())
            for_rows(0, r_tot, lambda r: row_copy(rows, xo_hbm, r, 0, wsem).wait())


def _moe_ffn(x, h, idx, gate, g2, groups, w1, w3, w2, layer):
    n, d = x.shape
    ne, r = idx.shape
    f = w1.shape[3]
    tf = FFN_TF
    nj = f // tf
    assert r % nj == 0
    grid_spec = pltpu.PrefetchScalarGridSpec(
        num_scalar_prefetch=1,
        grid=(ne, nj),
        in_specs=[pl.BlockSpec((1, r, 1), lambda e, j, idx: (e, 0, 0)),
                  pl.BlockSpec(g2.shape, lambda e, j, idx: (0, 0)),
                  pl.BlockSpec((1, 1, d, tf), lambda e, j, idx: (layer, e, 0, j)),
                  pl.BlockSpec((1, 1, d, tf), lambda e, j, idx: (layer, e, 0, j)),
                  pl.BlockSpec((1, 1, tf, d), lambda e, j, idx: (layer, e, j, 0)),
                  pl.BlockSpec(memory_space=pl.ANY),
                  pl.BlockSpec(memory_space=pl.ANY)],
        out_specs=pl.BlockSpec(memory_space=pl.ANY),
        scratch_shapes=[pltpu.VMEM((r, d), F32), pltpu.VMEM((r, d), BF16), pltpu.VMEM((r, d), F32),
                        pltpu.VMEM((r, d), F32), pltpu.SemaphoreType.DMA, pltpu.SemaphoreType.DMA,
                        pltpu.SemaphoreType.DMA])
    return pl.pallas_call(
        functools.partial(_moe_ffn_kernel, groups=tuple(groups)),
        grid_spec=grid_spec,
        out_shape=jax.ShapeDtypeStruct((n, d), F32),
        input_output_aliases={7: 0},
        compiler_params=_cparams(("arbitrary", "arbitrary")),
        name="moe_ffn",
    )(idx, gate.reshape(ne, r, 1), g2, w1, w3, w2, h, x)


def _moe_glue_reference(h, aff, segments, w1, w3, w2):
    nb, t, d = h.shape
    ne = aff.shape[-1]
    picks, xs = [], []
    for start, length in segments:
        cap = EC_FACTOR * length // ne
        gate, idx = lax.top_k(jnp.swapaxes(aff[:, start:start + length], 1, 2), cap)
        idx = idx + start
        picks.append((gate, idx))
        rows = jnp.take_along_axis(h[:, None], idx[..., None], axis=2)
        xs.append(jnp.swapaxes(rows, 0, 1).reshape(ne, nb * cap, d))
    y = _expert_ffn(jnp.concatenate(xs, axis=1), w1, w3, w2)
    out = jnp.zeros((nb, t, d), F32)
    bidx = jnp.arange(nb)[:, None, None]
    off = 0
    for (start, length), (gate, idx) in zip(segments, picks):
        cap = gate.shape[-1]
        ys = jnp.swapaxes(y[:, off:off + nb * cap].reshape(ne, nb, cap, d), 0, 1)
        out = out.at[bidx, idx].add(ys * gate[..., None])
        off += nb * cap
    return out


def _final_norm_kernel(x_ref, g_ref, o_ref):
    x = x_ref[0]
    o_ref[0] = x * lax.rsqrt(jnp.mean(x * x, axis=-1, keepdims=True) + NORM_EPS) * g_ref[...]


def _final_norm(xs, gain, ctx_len):
    nb, t, d = xs.shape
    tm = PREP_TM
    assert ctx_len % tm == 0 and t % tm == 0
    skip = ctx_len // tm
    return pl.pallas_call(
        _final_norm_kernel,
        grid=(nb, (t - ctx_len) // tm),
        in_specs=[pl.BlockSpec((1, tm, d), lambda b, i: (b, i + skip, 0)), pl.BlockSpec((1, d), lambda b, i: (0, 0))],
        out_specs=pl.BlockSpec((1, tm, d), lambda b, i: (b, i, 0)),
        out_shape=jax.ShapeDtypeStruct((nb, t - ctx_len, d), F32),
        compiler_params=_cparams(("parallel", "parallel")),
        name="final_norm",
    )(xs, gain.reshape(1, d))


RELAYOUT_TM = 256


def _relayout_kernel(w_ref, o_ref, *, gw, rw_cols, ngate):
    ml0 = rw_cols
    gd0 = ml0 + 4 * gw + ngate
    at0 = gd0 + 4 * gw + ngate
    o_ref[:, 0:rw_cols] = w_ref[:, 0:rw_cols].astype(BF16)
    o_ref[:, rw_cols:rw_cols + ngate] = w_ref[:, ml0 + 4 * gw:gd0].astype(BF16)
    o_ref[:, rw_cols + ngate:rw_cols + 2 * ngate] = w_ref[:, gd0 + 4 * gw:at0].astype(BF16)
    o_ref[:, rw_cols + 2 * ngate:4 * gw] = jnp.zeros((o_ref.shape[0], 4 * gw - rw_cols - 2 * ngate), BF16)
    o_ref[:, 4 * gw:8 * gw] = w_ref[:, ml0:ml0 + 4 * gw].astype(BF16)
    o_ref[:, 8 * gw:12 * gw] = w_ref[:, gd0:gd0 + 4 * gw].astype(BF16)
    o_ref[:, 12 * gw:] = w_ref[:, at0:].astype(BF16)


def _relayout_w_in(w_in, layer, gw, rw_cols, ngate):
    _, d, n = w_in.shape
    assert rw_cols % LANES == 0 and rw_cols + 2 * ngate <= 4 * gw
    n_out = n - rw_cols - 2 * ngate + 4 * gw
    tm = RELAYOUT_TM
    return pl.pallas_call(
        functools.partial(_relayout_kernel, gw=gw, rw_cols=rw_cols, ngate=ngate),
        grid=(d // tm,),
        in_specs=[pl.BlockSpec((None, tm, n), lambda i: (layer, i, 0))],
        out_specs=pl.BlockSpec((tm, n_out), lambda i: (i, 0)),
        out_shape=jax.ShapeDtypeStruct((d, n_out), BF16),
        compiler_params=_cparams(("parallel",)),
        name="w_in_relayout",
    )(w_in)


def kernel(x, c, ctx, c_ctx, ada_w, ada_b, norm1_g, norm2_g, w_in, w_out, rw_mu, rw_w0, rw_w_up, rw_a0, rw_a_up,
           rw_g_up, rw_k_k, rw_k_a, rw_r_k, rw_ln_w, rw_ln_b, ml_ib, ml_fb, ml_norm_g, gd_conv, gd_a_log,
           gd_dt_bias, gd_norm_g, at_q_norm, at_k_norm, w_router, w_exp1, w_exp3, w_exp2, final_g):
    nb, n_lat, d = x.shape
    ctx_len = ctx.shape[1]
    t = ctx_len + n_lat
    depth = ada_w.shape[0]
    gw = d // N_GROUPS
    rw_cols = rw_mu.shape[-1]
    ngate = 4 * ml_ib.shape[-1]
    at_cols = w_in.shape[-1] - rw_cols - 2 * (4 * gw + ngate)

    xs = jnp.concatenate([ctx, x], axis=1)
    mods = _modulation(jnp.concatenate([c, c_ctx[None]], axis=0), ada_w, ada_b)
    rope = _rope_tables(t, ctx_len)
    nh = ml_ib.shape[-1]
    assert 4 * gw == 2048 and gw == nh * ML_HEAD == nh * GD_HEAD and at_cols == 2 * gw
    head_of = jnp.arange(gw) // RW_HEAD
    hs = (head_of[:, None] == head_of[None, :]).astype(BF16)

    for layer in range(depth):
        last = layer == depth - 1
        m = mods[layer, :nb + 1].reshape(nb + 1, 6, d)
        m = jnp.pad(m, ((0, 0), (0, 2), (0, 0)))
        mod = jnp.stack([jnp.broadcast_to(m[nb], (nb, 8, d)), m[:nb]], axis=1)
        proj = _norm_proj(xs, norm1_g[layer], mod, _relayout_w_in(w_in, layer, gw, rw_cols, ngate), ctx_len)
        p_rw = {k: v[layer] for k, v in dict(
            rw_mu=rw_mu, rw_w0=rw_w0, rw_w_up=rw_w_up, rw_a0=rw_a0, rw_a_up=rw_a_up, rw_g_up=rw_g_up,
            rw_k_k=rw_k_k, rw_k_a=rw_k_a, rw_r_k=rw_r_k).items()}
        graw = proj[..., rw_cols:rw_cols + 2 * ngate]
        r, v, kk, lwf, bf, ktf, lwb, bb, ktb, g_rw, bonus = _rwkv_prep(proj, 0, p_rw, hs, ctx_len)
        ryf, ryb = _rwkv_scan(r, v, kk, [(lwf, bf, ktf), (lwb, bb, ktb)], ctx_len)

        gm = graw[..., :ngate]
        li = gm[..., :2 * nh] + ml_ib[layer].reshape(-1)
        lf = jax.nn.log_sigmoid(gm[..., 2 * nh:] + ml_fb[layer].reshape(-1))
        g_ml = jnp.concatenate([li, lf], axis=-1)
        myf, myb = _mlstm_scan(proj, proj, proj, (4, 5, 6), g_ml, jnp.swapaxes(g_ml, 1, 2), ctx_len)

        gg = graw[..., ngate:]
        lg = -jnp.exp(gd_a_log[layer]).reshape(-1) * jax.nn.softplus(gg[..., :2 * nh] + gd_dt_bias[layer].reshape(-1))
        q_gd, k_gd, v_gd = _gdn_prep(proj, 8, gd_conv[layer], nh, ctx_len)
        gyf, gyb = _gdn_scan(q_gd, k_gd, v_gd, jnp.concatenate([lg, jax.nn.sigmoid(gg[..., 2 * nh:])], axis=-1), ctx_len)

        kvw = AT_KV_HEADS * AT_HEAD
        q_at, k_at, v_at = _attn_prep(proj, 12, at_cols - 2 * kvw, kvw, at_q_norm[layer], at_k_norm[layer], rope)
        ay = _attention(q_at, k_at, v_at, ctx_len)

        ymix = _mix_post((ryf, ryb, bonus, g_rw), (myf, myb), (gyf, gyb), ay, proj, 7, 11,
                         (rw_ln_w[layer], rw_ln_b[layer], ml_norm_g[layer], gd_norm_g[layer]), hs)
        xs_flat, h2, aff = _out_proj(ymix, w_out[layer].astype(BF16), xs, norm2_g[layer], mod, w_router[layer],
                                     ctx_len)
        segments = [(ctx_len, n_lat)] if last else [(0, ctx_len), (ctx_len, n_lat)]
        ne = w_router.shape[-1]
        idx_parts, gate_parts, groups, g2_rows, off = [], [], [], [], 0
        for (start, length), (tok, gate) in zip(segments, _route(aff, segments, ne)):
            for b in range(nb):
                idx_parts.append(tok[b] + (b * t + start))
                gate_parts.append(gate[b])
                groups.append((off, tok.shape[-1]))
                off += tok.shape[-1]
                g2_rows.append(mod[b, 0 if start < ctx_len else 1, 5])
        xs = _moe_ffn(xs_flat, h2, jnp.concatenate(idx_parts, axis=1), jnp.concatenate(gate_parts, axis=1),
                      jnp.stack(g2_rows), groups, w_exp1, w_exp3, w_exp2, layer).reshape(nb, t, d)

    return _final_norm(xs, final_g, ctx_len)
```

```python
import functools
import math

import jax
import jax.numpy as jnp
from jax import lax
from jax.experimental import pallas as pl
from jax.experimental.pallas import tpu as pltpu

F32 = jnp.float32
BF16 = jnp.bfloat16

NORM_EPS = 1e-6
GRID_W = 64
N_GROUPS = 4
RW_HEAD = 64
RW_DECAY_RANK = 64
RW_ICLR_RANK = 64
RW_GATE_RANK = 128
RW_DECAY_SCALE = math.exp(-0.5)
RW_GN_EPS = 64e-5
RW_CHUNK = 64
ML_HEAD = 128
ML_CHUNK = 256
GD_HEAD = 128
GD_CHUNK = 64
AT_HEAD = 128
AT_KV_HEADS = 2
ROPE_THETA = 10000.0
AT_TQ = 256
N_EXPERTS = 16
EC_FACTOR = 2

VMEM_LIMIT = 56 * 1024 * 1024
LANES = 128


def _cparams(sem):
    return pltpu.CompilerParams(dimension_semantics=sem, vmem_limit_bytes=VMEM_LIMIT)


def _pick_tile(n, limit, mult=16):
    best = None
    for cand in range(mult, min(n, limit) + 1, mult):
        if n % cand == 0:
            best = cand
    assert best is not None, (n, limit, mult)
    return best


_DIMS = {
    "nn": (((1,), (0,)), ((), ())),
    "nt": (((1,), (1,)), ((), ())),
    "tn": (((0,), (0,)), ((), ())),
}


def _split(a, n):
    if a.dtype == BF16:
        return [a]
    pieces, rest = [], a
    for i in range(n):
        p = rest.astype(BF16)
        pieces.append(p)
        if i + 1 < n:
            rest = rest - p.astype(F32)
    return pieces


def _mm(a, b, dims="nn", pa=1, pb=1):
    ap, bp = _split(a, pa), _split(b, pb)
    top = max(len(ap), len(bp))
    acc = None
    for i, x in enumerate(ap):
        for j, y in enumerate(bp):
            if i + j < top:
                t = lax.dot_general(x, y, _DIMS[dims], preferred_element_type=F32)
                acc = t if acc is None else acc + t
    return acc


def _tri(n, rev, strict, reps=1):
    t = lax.broadcasted_iota(jnp.int32, (n, reps * n), 0)
    s = lax.broadcasted_iota(jnp.int32, (n, reps * n), 1)
    if reps > 1:
        assert n & (n - 1) == 0
        s = s & (n - 1)
    if rev:
        return (s > t) if strict else (s >= t)
    return (s < t) if strict else (s <= t)


def _chunk_maps(nsteps, nctx):
    fwd = lambda i: i
    bwd = lambda i: jnp.where(i < nctx, nctx - 1 - i, nsteps - 1 - i + nctx)
    return fwd, bwd


def _unit_lower_inverse(a, levels, passes):
    n = a.shape[0]
    eye = (lax.broadcasted_iota(jnp.int32, (n, n), 0) == lax.broadcasted_iota(jnp.int32, (n, n), 1)).astype(F32)
    y = -a
    r = eye + y
    for _ in range(levels):
        y = _mm(y, y, "nn", passes, passes)
        r = r + _mm(r, y, "nn", passes, passes)
    return r


def _mlstm_kernel(qf, kf, vf, gf, gtf, qb, kb, vb, gb, gtb, yf, yb, c_scr, m_scr, *, nb, nh, chunk):
    step = pl.program_id(0)

    @pl.when(step == 0)
    def _():
        c_scr[...] = jnp.zeros_like(c_scr)
        m_scr[...] = jnp.zeros_like(m_scr)

    scale = ML_HEAD ** -0.5
    lane = lax.broadcasted_iota(jnp.int32, (chunk, ML_HEAD), 1)
    ones_col = (lane == 0).astype(F32)
    units = []
    for d, (q_ref, k_ref, v_ref, g_ref, gt_ref, y_ref) in enumerate(
            ((qf, kf, vf, gf, gtf, yf), (qb, kb, vb, gb, gtb, yb))):
        rev = d == 1
        incl = _tri(chunk, rev, False)
        cum_m = incl.astype(BF16)
        last = 0 if rev else chunk - 1
        for b in range(nb):
            g = g_ref[b]
            gt = gt_ref[b]
            cum_c = _mm(cum_m, g, "nn", 1, 3)
            cum_r = _mm(gt, cum_m, "nt", 3, 1)
            for h in range(nh):
                ci, cf = d * nh + h, (2 + d) * nh + h
                sl = slice(h * ML_HEAD, (h + 1) * ML_HEAD)
                u = dict(y_ref=y_ref, b=b, sl=sl, idx=(d * nb + b) * nh + h, last=last)
                u["q"] = q_ref[b, :, sl].astype(BF16)
                u["k"] = k_ref[b, :, sl] * scale
                u["v_aug"] = jnp.concatenate([v_ref[b, :, sl], ones_col], axis=1).astype(BF16)
                u["m_old"] = m_scr[u["idx"]][0:1, 0:1]
                u["bc"] = cum_c[:, cf:cf + 1]
                u["li_c"] = g[:, ci:ci + 1]
                dlog = jnp.where(incl, u["bc"] - cum_r[cf:cf + 1, :] + gt[ci:ci + 1, :], -jnp.inf)
                inter = u["bc"] + u["m_old"]
                u["mt"] = jnp.maximum(inter, jnp.max(dlog, axis=1, keepdims=True))
                u["p"] = jnp.exp(dlog - u["mt"])
                u["wi"] = jnp.exp(inter - u["mt"])
                units.append(u)
    for u in units:
        u["s"] = (_mm(u["q"], u["k"], "nt") * u["p"]).astype(BF16)
    for u in units:
        u["c_old"] = c_scr[u["idx"]]
        u["acc"] = _mm(u["s"], u["v_aug"]) + u["wi"] * _mm(u["q"], u["c_old"])
    for u in units:
        num = u["acc"][:, :ML_HEAD]
        den = u["acc"][:, ML_HEAD:ML_HEAD + 1]
        u["y_ref"][u["b"], :, u["sl"]] = num / jnp.maximum(jnp.abs(den), jnp.exp(-u["mt"]))
    for u in units:
        last = u["last"]
        m_new = u["mt"][last:last + 1, :]
        b_last = u["bc"][last:last + 1, :]
        wk = jnp.exp(b_last - u["bc"] + u["li_c"] - m_new)
        dc = jnp.exp(b_last + u["m_old"] - m_new)
        c_scr[u["idx"]] = dc * u["c_old"] + _mm(u["k"] * wk, u["v_aug"], "tn")
        m_scr[u["idx"]] = jnp.broadcast_to(m_new, (8, LANES))


def _mlstm_scan(q_src, k_src, v_src, cols, g, gt, ctx_len):
    nb, t, _ = g.shape
    nh = g.shape[-1] // 4
    width = nh * ML_HEAD
    chunk = ML_CHUNK
    nsteps, nctx = t // chunk, ctx_len // chunk
    fwd, bwd = _chunk_maps(nsteps, nctx)

    def specs(cm):
        col = lambda c: pl.BlockSpec((nb, chunk, width), lambda i, c=c: (0, cm(i), c))
        return [col(cols[0]), col(cols[1]), col(cols[2]),
                pl.BlockSpec((nb, chunk, 4 * nh), lambda i: (0, cm(i), 0)),
                pl.BlockSpec((nb, 4 * nh, chunk), lambda i: (0, 0, cm(i)))]

    out_spec = lambda cm: pl.BlockSpec((nb, chunk, width), lambda i: (0, cm(i), 0))
    kern = functools.partial(_mlstm_kernel, nb=nb, nh=nh, chunk=chunk)
    return pl.pallas_call(
        kern,
        grid=(nsteps,),
        in_specs=specs(fwd) + specs(bwd),
        out_specs=[out_spec(fwd), out_spec(bwd)],
        out_shape=[jax.ShapeDtypeStruct((nb, t, width), F32)] * 2,
        scratch_shapes=[pltpu.VMEM((2 * nb * nh, ML_HEAD, 2 * ML_HEAD), F32),
                        pltpu.VMEM((2 * nb * nh, 8, LANES), F32)],
        compiler_params=_cparams(("arbitrary",)),
        name="mlstm_scan",
    )(q_src, k_src, v_src, g, gt, q_src, k_src, v_src, g, gt)


def _head_layer_norm(y, nh, eps):
    b, t, w = y.shape
    yh = y.reshape(b, t, nh, w // nh)
    mean = jnp.mean(yh, axis=-1, keepdims=True)
    var = jnp.mean(jnp.square(yh - mean), axis=-1, keepdims=True)
    return ((yh - mean) * lax.rsqrt(var + eps)).reshape(b, t, w)


def _mlstm_mixer(proj, col0, graw, ml_ib, ml_fb, ml_norm_g, ctx_len):
    nh = graw.shape[-1] // 4
    width = nh * ML_HEAD
    li = graw[..., :2 * nh] + ml_ib.reshape(-1)
    lf = jax.nn.log_sigmoid(graw[..., 2 * nh:] + ml_fb.reshape(-1))
    g = jnp.concatenate([li, lf], axis=-1)
    yf, yb = _mlstm_scan(proj, proj, proj, (col0, col0 + 1, col0 + 2), g, jnp.swapaxes(g, 1, 2), ctx_len)
    y = _head_layer_norm(yf + yb, nh, NORM_EPS) * ml_norm_g
    o = proj[..., (col0 + 3) * width:(col0 + 4) * width]
    return y * jax.nn.sigmoid(o)


def _gdn_kernel(qf, kf, vf, gf, gtf, qb, kb, vb, gb, gtb, yf, yb, s_scr, *, nb, nh, chunk):
    step = pl.program_id(0)

    @pl.when(step == 0)
    def _():
        s_scr[...] = jnp.zeros_like(s_scr)

    levels = int(math.log2(chunk)) - 1
    c = chunk
    dh = GD_HEAD
    pw = 2 * dh
    first = lax.broadcasted_iota(jnp.int32, (c, pw), 1) < dh
    first2 = lax.broadcasted_iota(jnp.int32, (c, 2 * c), 1) < c
    first2_row = lax.broadcasted_iota(jnp.int32, (1, 2 * c), 1) < c
    first4 = (lax.broadcasted_iota(jnp.int32, (c, 2 * pw), 1) // dh) % 2 == 0
    same_head = (lax.broadcasted_iota(jnp.int32, (pw, pw), 0) // dh
                 == lax.broadcasted_iota(jnp.int32, (pw, pw), 1) // dh)
    same_blk = (lax.broadcasted_iota(jnp.int32, (2 * c, 2 * c), 0) // c
                == lax.broadcasted_iota(jnp.int32, (2 * c, 2 * c), 1) // c)
    eye2 = (lax.broadcasted_iota(jnp.int32, (c, 2 * c), 0)
            == lax.broadcasted_iota(jnp.int32, (c, 2 * c), 1) % c).astype(F32)

    def by_head(x, mask):
        return jnp.concatenate([jnp.where(mask, x, 0.0), jnp.where(mask, 0.0, x)], axis=0)

    def blockdiag(x):
        xb = x.astype(BF16)
        return jnp.where(same_blk, jnp.concatenate([xb, xb], axis=0), jnp.zeros((), BF16))

    units = []
    for d, (q_ref, k_ref, v_ref, g_ref, gt_ref, y_ref) in enumerate(
            ((qf, kf, vf, gf, gtf, yf), (qb, kb, vb, gb, gtb, yb))):
        rev = d == 1
        incl = _tri(c, rev, False)
        incl2 = _tri(c, rev, False, reps=2)
        strict2 = _tri(c, rev, True, reps=2)
        cum_m = incl.astype(BF16)
        cum_m2 = jnp.concatenate([cum_m, cum_m], axis=0)
        last = 0 if rev else c - 1
        for b in range(nb):
            g = g_ref[b]
            gt = gt_ref[b, 0]
            cum_c = _mm(cum_m, g, "nn", 1, 3)
            cum_r = _mm(gt, cum_m2, "nt", 3, 1)
            for hp in range(nh // 2):
                h0, h1 = 2 * hp, 2 * hp + 1
                sl = slice(hp * pw, (hp + 1) * pw)
                u = dict(y_ref=y_ref, b=b, sl=sl, idx=(d * nb + b) * (nh // 2) + hp, strict2=strict2)
                q, k, v = q_ref[b, :, sl], k_ref[b, :, sl], v_ref[b, :, sl]
                col = lambda arr, j: jnp.where(first, arr[:, j + h0:j + h0 + 1], arr[:, j + h1:j + h1 + 1])
                gc = col(cum_c, d * nh)
                beta = col(g, (2 + d) * nh)
                gc_c2 = jnp.where(first2, cum_c[:, d * nh + h0:d * nh + h0 + 1], cum_c[:, d * nh + h1:d * nh + h1 + 1])
                gc_r2 = jnp.where(first2_row, cum_r[d * nh + h0:d * nh + h0 + 1], cum_r[d * nh + h1:d * nh + h1 + 1])
                u["decay"] = jnp.where(incl2, jnp.exp(jnp.where(incl2, gc_c2 - gc_r2, 0.0)), 0.0)
                kb_ = k * beta
                eg = jnp.exp(gc)
                u["kq"] = jnp.concatenate([kb_, q], axis=0).astype(BF16)
                u["k_blk"] = by_head(k, first).astype(BF16)
                u["rhs"] = by_head(jnp.concatenate([v * beta, kb_ * eg], axis=1), first4).astype(BF16)
                u["qg"] = q * eg
                g_last = gc[last:last + 1, :]
                u["kd"] = (k * jnp.exp(g_last - gc)).astype(BF16)
                u["keep"] = jnp.exp(g_last)
                units.append(u)

    for u in units:
        kq = _mm(u["kq"], u["k_blk"], "nt")
        u["y"] = jnp.where(u["strict2"], -kq[:c] * u["decay"], 0.0)
        u["attn"] = (kq[c:] * u["decay"]).astype(BF16)
        u["inv"] = eye2 + u["y"]
        u["bd"] = blockdiag(u["y"])
    for _ in range(levels):
        for u in units:
            u["y"] = _mm(u["y"], u["bd"])
            u["bd"] = blockdiag(u["y"])
        for u in units:
            u["inv"] = u["inv"] + _mm(u["inv"], u["bd"])
    for u in units:
        u["sol"] = _mm(u["inv"], u["rhs"])
    for u in units:
        u["s_old"] = s_scr[u["idx"]]
        u["ws"] = _mm(jnp.concatenate([u["sol"][:, pw:], u["qg"]], axis=0), u["s_old"])
    for u in units:
        u["v_new"] = u["sol"][:, :pw] - u["ws"][:c]
        u["y_ref"][u["b"], :, u["sl"]] = u["ws"][c:] + _mm(u["attn"], by_head(u["v_new"], first))
    for u in units:
        upd = _mm(u["kd"], u["v_new"], "tn")
        s_scr[u["idx"]] = u["s_old"] * u["keep"] + jnp.where(same_head, upd, 0.0)


def _gdn_scan(q, k, v, g, ctx_len):
    nb, t, width = q.shape
    nh = width // GD_HEAD
    chunk = GD_CHUNK
    nsteps, nctx = t // chunk, ctx_len // chunk
    fwd, bwd = _chunk_maps(nsteps, nctx)
    gt = jnp.swapaxes(g.reshape(nb, nsteps, chunk, 4 * nh), 2, 3)

    def specs(cm):
        col = pl.BlockSpec((nb, chunk, width), lambda i: (0, cm(i), 0))
        return [col, col, col,
                pl.BlockSpec((nb, chunk, 4 * nh), lambda i: (0, cm(i), 0)),
                pl.BlockSpec((nb, 1, 4 * nh, chunk), lambda i: (0, cm(i), 0, 0))]

    out_spec = lambda cm: pl.BlockSpec((nb, chunk, width), lambda i: (0, cm(i), 0))
    kern = functools.partial(_gdn_kernel, nb=nb, nh=nh, chunk=chunk)
    return pl.pallas_call(
        kern,
        grid=(nsteps,),
        in_specs=specs(fwd) + specs(bwd),
        out_specs=[out_spec(fwd), out_spec(bwd)],
        out_shape=[jax.ShapeDtypeStruct((nb, t, width), F32)] * 2,
        scratch_shapes=[pltpu.VMEM((nb * nh, 2 * GD_HEAD, 2 * GD_HEAD), F32)],
        compiler_params=_cparams(("arbitrary",)),
        name="gdn_scan",
    )(q, k, v, g, gt, q, k, v, g, gt)


def _seq_shift(x, ctx_len, delta):
    t = x.shape[1]
    rows = jnp.arange(t)
    if delta < 0:
        sh = jnp.pad(x[:, :delta], ((0, 0), (-delta, 0), (0, 0)))
        ok = (rows + delta >= 0) & ((rows < ctx_len) | (rows + delta >= ctx_len))
    else:
        sh = jnp.pad(x[:, delta:], ((0, 0), (0, delta), (0, 0)))
        ok = (rows + delta < t) & ((rows >= ctx_len) | (rows + delta < ctx_len))
    return jnp.where(ok[None, :, None], sh, 0.0)


def _l2n(x, nh, eps=1e-6):
    b, t, w = x.shape
    xh = x.reshape(b, t, nh, w // nh)
    return (xh * lax.rsqrt(jnp.sum(xh * xh, axis=-1, keepdims=True) + eps)).reshape(b, t, w)


def _gdn_mixer(qkv, gate, graw, gd_conv, gd_a_log, gd_dt_bias, gd_norm_g, ctx_len):
    nh = graw.shape[-1] // 4
    width = nh * GD_HEAD
    pad = gd_conv.shape[0] // 2
    conv = sum(_seq_shift(qkv, ctx_len, j - pad) * gd_conv[j] for j in range(gd_conv.shape[0]))
    qkv = jax.nn.silu(conv)
    q = _l2n(qkv[..., :width], nh) * GD_HEAD ** -0.5
    k = _l2n(qkv[..., width:2 * width], nh)
    v = qkv[..., 2 * width:]
    lg = -jnp.exp(gd_a_log).reshape(-1) * jax.nn.softplus(graw[..., :2 * nh] + gd_dt_bias.reshape(-1))
    beta = jax.nn.sigmoid(graw[..., 2 * nh:])
    yf, yb = _gdn_scan(q, k, v, jnp.concatenate([lg, beta], axis=-1), ctx_len)
    o = yf + yb
    b, t, _ = o.shape
    oh = o.reshape(b, t, nh, GD_HEAD)
    oh = oh * lax.rsqrt(jnp.mean(oh * oh, axis=-1, keepdims=True) + NORM_EPS) * gd_norm_g
    return oh.reshape(b, t, width) * jax.nn.silu(gate)


RW_PASSES = 1


def _rwkv_kernel(*refs, nb, npair, chunk):
    ins, (yf, yb, s_scr) = refs[:12], refs[12:]
    step = pl.program_id(0)

    @pl.when(step == 0)
    def _():
        s_scr[...] = jnp.zeros_like(s_scr)

    levels = int(math.log2(chunk)) - 1
    pw = 2 * RW_HEAD
    c = chunk
    head0 = lax.broadcasted_iota(jnp.int32, (c, pw), 1) < RW_HEAD
    same_head = (lax.broadcasted_iota(jnp.int32, (pw, pw), 0) // RW_HEAD
                 == lax.broadcasted_iota(jnp.int32, (pw, pw), 1) // RW_HEAD)
    same_blk = (lax.broadcasted_iota(jnp.int32, (2 * c, 2 * c), 0) // c
                == lax.broadcasted_iota(jnp.int32, (2 * c, 2 * c), 1) // c)
    eye2 = (lax.broadcasted_iota(jnp.int32, (c, 2 * c), 0)
            == lax.broadcasted_iota(jnp.int32, (c, 2 * c), 1) % c).astype(F32)
    mid = c // 2

    def by_head(x):
        return jnp.concatenate([jnp.where(head0, x, 0.0), jnp.where(head0, 0.0, x)], axis=0)

    def blockdiag(x):
        xb = x.astype(BF16)
        return jnp.where(same_blk, jnp.concatenate([xb, xb], axis=0), jnp.zeros((), BF16))

    units = []
    for d, y_ref in enumerate((yf, yb)):
        r_ref, lw_ref, kt_ref, v_ref, kk_ref, b_ref = ins[6 * d:6 * d + 6]
        rev = d == 1
        incl = _tri(c, rev, False)
        strict = _tri(c, rev, True)
        incl2 = _tri(c, rev, False, reps=2)
        strict2 = _tri(c, rev, True, reps=2)
        cum_m = incl.astype(BF16)
        last = 0 if rev else c - 1
        for b in range(nb):
            lw_all = lw_ref[b]
            g_all = _mm(cum_m, lw_all, "nn", 1, 3)
            for p in range(npair):
                sl = slice(p * pw, (p + 1) * pw)
                u = dict(y_ref=y_ref, b=b, sl=sl, idx=(d * nb + b) * npair + p, incl2=incl2, strict2=strict2)
                lw, g = lw_all[:, sl], g_all[:, sl]
                r, kt, v, kk, bb = (x[b, :, sl] for x in (r_ref, kt_ref, v_ref, kk_ref, b_ref))
                gh = g - g[mid:mid + 1, :]
                e_pos, e_neg = jnp.exp(gh), jnp.exp(-gh)
                u["lhs"] = jnp.concatenate([jnp.exp(gh - lw) * kk, r * e_pos], axis=0).astype(BF16)
                u["rhs"] = jnp.concatenate([by_head(bb * e_neg), by_head(kt * e_neg)], axis=0).astype(BF16)
                u["st_lhs"] = jnp.concatenate([jnp.exp(g - lw) * kk, jnp.exp(g) * r], axis=0).astype(BF16)
                g_last = g[last:last + 1, :]
                dl = jnp.exp(g_last - g)
                u["keep"] = jnp.exp(g_last)
                u["bk_d"] = jnp.concatenate([bb * dl, kt * dl], axis=0).astype(BF16)
                u["v"] = v
                u["v_blk"] = by_head(v).astype(BF16)
                units.append(u)

    for u in units:
        u["s_old"] = s_scr[u["idx"]]
        u["from_state"] = _mm(u["st_lhs"], u["s_old"], "nt")
    for u in units:
        gm = _mm(u["lhs"], u["rhs"], "nt")
        u["y"] = jnp.where(u["strict2"], -gm[:c, :2 * c], 0.0)
        u["a_ak"] = jnp.where(u["strict2"], gm[:c, 2 * c:], 0.0).astype(BF16)
        u["r_mat"] = jnp.concatenate([jnp.where(u["incl2"], -gm[c:, :2 * c], 0.0),
                                      jnp.where(u["incl2"], gm[c:, 2 * c:], 0.0)], axis=1).astype(BF16)
        u["inv"] = eye2 + u["y"]
        u["bd"] = blockdiag(u["y"])
    for u in units:
        u["rhs_e"] = u["from_state"][:c] + _mm(u["a_ak"], u["v_blk"])
    for _ in range(levels):
        for u in units:
            u["y"] = _mm(u["y"], u["bd"])
            u["bd"] = blockdiag(u["y"])
        for u in units:
            u["inv"] = u["inv"] + _mm(u["inv"], u["bd"])
    for u in units:
        u["e"] = _mm(u["inv"], by_head(u["rhs_e"]))
    for u in units:
        ev = jnp.concatenate([by_head(u["e"]), by_head(u["v"])], axis=0)
        u["y_ref"][u["b"], :, u["sl"]] = u["from_state"][c:] + _mm(u["r_mat"], ev)
    for u in units:
        upd = _mm(jnp.concatenate([-u["e"], u["v"]], axis=0), u["bk_d"], "tn")
        s_scr[u["idx"]] = u["s_old"] * u["keep"] + jnp.where(same_head, upd, 0.0)


def _rwkv_scan(r, v, kk, per_dir, ctx_len):
    nb, t, width = r.shape
    chunk = RW_CHUNK
    nsteps, nctx = t // chunk, ctx_len // chunk
    fwd, bwd = _chunk_maps(nsteps, nctx)
    spec = lambda cm: pl.BlockSpec((nb, chunk, width), lambda i: (0, cm(i), 0))
    args = []
    for d in range(2):
        lw, bb, kt = per_dir[d]
        args += [r, lw, kt, v, kk, bb]
    kern = functools.partial(_rwkv_kernel, nb=nb, npair=width // (2 * RW_HEAD), chunk=chunk)
    return pl.pallas_call(
        kern,
        grid=(nsteps,),
        in_specs=[spec(fwd)] * 6 + [spec(bwd)] * 6,
        out_specs=[spec(fwd), spec(bwd)],
        out_shape=[jax.ShapeDtypeStruct((nb, t, width), F32)] * 2,
        scratch_shapes=[pltpu.VMEM((2 * nb * (width // (2 * RW_HEAD)), 2 * RW_HEAD, 2 * RW_HEAD), F32)],
        compiler_params=_cparams(("arbitrary",)),
        name="rwkv_scan",
    )(*args)


def _rwkv_mixer(pa, p, ctx_len):
    width = p['rw_k_k'].shape[0]
    nh = width // RW_HEAD
    mu = p['rw_mu']
    pp = pa + mu[0] * (_seq_shift(pa, ctx_len, -1) - pa) + mu[1] * (_seq_shift(pa, ctx_len, 1) - pa)
    r, k, v = pp[..., :width], pp[..., width:2 * width], pp[..., 2 * width:3 * width]
    o = 3 * width
    wd = pp[..., o:o + RW_DECAY_RANK]
    ad = pp[..., o + RW_DECAY_RANK:o + RW_DECAY_RANK + RW_ICLR_RANK]
    gd = pp[..., o + RW_DECAY_RANK + RW_ICLR_RANK:]
    g = jax.nn.sigmoid(gd) @ p['rw_g_up']
    kk = _l2n(k * p['rw_k_k'], nh)
    per_dir, kts = [], []
    for d in range(2):
        logw = -RW_DECAY_SCALE * jax.nn.sigmoid(p['rw_w0'][d] + jnp.tanh(wd) @ p['rw_w_up'][d])
        a = jax.nn.sigmoid(p['rw_a0'][d] + ad @ p['rw_a_up'][d])
        kt = k * (1.0 + (a - 1.0) * p['rw_k_a'])
        per_dir.append((logw, kk * a, kt))
        kts.append(kt)
    yf, yb = _rwkv_scan(r, v, kk, per_dir, ctx_len)
    y = _head_layer_norm(yf + yb, nh, RW_GN_EPS) * p['rw_ln_w'] + p['rw_ln_b']
    b, t, _ = y.shape
    bh = lambda x: x.reshape(b, t, nh, RW_HEAD)
    bonus = sum(jnp.sum(bh(r) * bh(kt) * p['rw_r_k'], axis=-1, keepdims=True) * bh(v) for kt in kts)
    return (y + bonus.reshape(y.shape)) * g


def _attn_kernel(q_ref, k_ref, v_ref, o_ref, *, ctx_tiles, ctx_len, group):
    qi = pl.program_id(2)

    def run(keys, vals):
        for g in range(group):
            sl = slice(g * AT_HEAD, (g + 1) * AT_HEAD)
            s = lax.dot_general(q_ref[0, :, sl], keys, _DIMS["nt"], preferred_element_type=F32)
            p = jnp.exp(s - jnp.max(s, axis=-1, keepdims=True))
            o = jnp.dot(p.astype(BF16), vals, preferred_element_type=F32)
            o_ref[0, :, sl] = o / jnp.sum(p, axis=-1, keepdims=True)

    @pl.when(qi < ctx_tiles)
    def _():
        run(k_ref[0, :ctx_len, :], v_ref[0, :ctx_len, :])

    @pl.when(qi >= ctx_tiles)
    def _():
        run(k_ref[0], v_ref[0])


def _attention(q, k, v, ctx_len):
    nb, t, qw = q.shape
    hkv = k.shape[-1] // AT_HEAD
    group = qw // AT_HEAD // hkv
    tq = AT_TQ
    assert ctx_len % tq == 0 and t % tq == 0
    kern = functools.partial(_attn_kernel, ctx_tiles=ctx_len // tq, ctx_len=ctx_len, group=group)
    return pl.pallas_call(
        kern,
        grid=(nb, hkv, t // tq),
        in_specs=[pl.BlockSpec((1, tq, group * AT_HEAD), lambda b, h, i: (b, i, h)),
                  pl.BlockSpec((1, t, AT_HEAD), lambda b, h, i: (b, 0, h)),
                  pl.BlockSpec((1, t, AT_HEAD), lambda b, h, i: (b, 0, h))],
        out_specs=pl.BlockSpec((1, tq, group * AT_HEAD), lambda b, h, i: (b, i, h)),
        out_shape=jax.ShapeDtypeStruct((nb, t, qw), F32),
        compiler_params=_cparams(("parallel", "parallel", "arbitrary")),
        name="gqa",
    )(q, k, v)


def _rope_tables(t, ctx_len):
    n_lat = t - ctx_len
    pos = jnp.arange(n_lat)
    row = (pos // GRID_W).astype(F32)
    col = (pos % GRID_W).astype(F32)
    axis_dim = AT_HEAD // 2
    inv_freq = ROPE_THETA ** (-jnp.arange(0, axis_dim, 2, dtype=F32) / axis_dim)
    ang_r = row[:, None] * inv_freq[None, :]
    ang_c = col[:, None] * inv_freq[None, :]
    cos = jnp.concatenate([jnp.cos(ang_r)] * 2 + [jnp.cos(ang_c)] * 2, axis=-1)
    sin = jnp.concatenate([-jnp.sin(ang_r), jnp.sin(ang_r), -jnp.sin(ang_c), jnp.sin(ang_c)], axis=-1)
    cos = jnp.concatenate([jnp.ones((ctx_len, AT_HEAD), F32), cos], axis=0)
    sin = jnp.concatenate([jnp.zeros((ctx_len, AT_HEAD), F32), sin], axis=0)
    return cos, sin


def _attn_mixer(pd, at_q_norm, at_k_norm, ctx_len, rope):
    nb, t, w = pd.shape
    kvw = AT_KV_HEADS * AT_HEAD
    qw = w - 2 * kvw
    cos, sin = rope

    def prep(x, g):
        xh = x.reshape(nb, t, -1, AT_HEAD)
        xh = xh * lax.rsqrt(jnp.mean(xh * xh, axis=-1, keepdims=True) + NORM_EPS) * g
        sw = jnp.flip(xh.reshape(nb, t, -1, 2, 2, AT_HEAD // 4), axis=-2).reshape(xh.shape)
        return (xh * cos[:, None, :] + sw * sin[:, None, :]).reshape(x.shape)

    q = prep(pd[..., :qw], at_q_norm) * AT_HEAD ** -0.5
    k = prep(pd[..., qw:qw + kvw], at_k_norm)
    v = pd[..., qw + kvw:]
    return _attention(q.astype(BF16), k.astype(BF16), v.astype(BF16), ctx_len)


PREP_TM = 256
HALO = 8


def _halo_specs(tm, width, col, t):
    per = tm // HALO
    last = t // HALO - 1
    prev = pl.BlockSpec((1, HALO, width), lambda b, i: (b, jnp.maximum(i * per - 1, 0), col))
    nxt = pl.BlockSpec((1, HALO, width), lambda b, i: (b, jnp.minimum((i + 1) * per, last), col))
    return prev, nxt


def _neighbours(x, prev_ref, next_ref, start, ctx_len, t):
    tm = x.shape[0]
    row = lax.broadcasted_iota(jnp.int32, x.shape, 0)
    has_prev = jnp.logical_and(start != 0, start != ctx_len)
    has_next = jnp.logical_and(start + tm != ctx_len, start + tm != t)
    prev_row = jnp.where(has_prev, prev_ref[0, HALO - 1:HALO, :], 0.0)
    next_row = jnp.where(has_next, next_ref[0, 0:1, :], 0.0)
    x_prev = jnp.where(row == 0, prev_row, pltpu.roll(x, 1, 0))
    x_next = jnp.where(row == tm - 1, next_row, pltpu.roll(x, tm - 1, 0))
    return x_prev, x_next


def _rwkv_prep_kernel(x_ref, xp_ref, xn_ref, mu_ref, wa_ref, gup_ref, w0_ref, a0_ref, kk_ref, ka_ref, rk_ref, hs_ref,
                      r_o, v_o, kkn_o, lwf_o, bf_o, ktf_o, lwb_o, bb_o, ktb_o, g_o, bonus_o, *, ctx_len, t, width):
    start = pl.program_id(1) * x_ref.shape[1]
    x = x_ref[0]
    x_prev, x_next = _neighbours(x, xp_ref, xn_ref, start, ctx_len, t)
    pp = x + mu_ref[0:1, :] * (x_prev - x) + mu_ref[1:2, :] * (x_next - x)
    w = width
    r, k, v = pp[:, :w], pp[:, w:2 * w], pp[:, 2 * w:3 * w]
    low = pp[:, 3 * w:3 * w + LANES]
    lane = lax.broadcasted_iota(jnp.int32, low.shape, 1)
    low = jnp.where(lane < RW_DECAY_RANK, jnp.tanh(low), low)
    g_o[0] = _mm(jax.nn.sigmoid(pp[:, 3 * w + LANES:]), gup_ref[...])
    hs = hs_ref[...]
    kx = k * kk_ref[...]
    kk = kx * lax.rsqrt(_mm(kx * kx, hs, "nn", 3, 1) + 1e-6)
    r_o[0], v_o[0], kkn_o[0] = r, v, kk
    kt_sum = None
    for d, (lw_o, b_o, kt_o) in enumerate(((lwf_o, bf_o, ktf_o), (lwb_o, bb_o, ktb_o))):
        up = _mm(low, wa_ref[d])
        lw_o[0] = -RW_DECAY_SCALE * jax.nn.sigmoid(w0_ref[d:d + 1, :] + up[:, :w])
        a = jax.nn.sigmoid(a0_ref[d:d + 1, :] + up[:, w:])
        kt = k * (1.0 + (a - 1.0) * ka_ref[...])
        b_o[0] = kk * a
        kt_o[0] = kt
        kt_sum = kt if kt_sum is None else kt_sum + kt
    bonus_o[0] = _mm(r * kt_sum * rk_ref[...], hs, "nn", 3, 1) * v


def _rwkv_prep(proj, col, p, hs, ctx_len):
    nb, t, _ = proj.shape
    width = p['rw_k_k'].shape[0]
    cols = p['rw_mu'].shape[-1]
    tm = PREP_TM
    assert ctx_len % tm == 0 and t % tm == 0 and cols == 3 * width + LANES + RW_GATE_RANK
    rank = RW_DECAY_RANK
    wa = jnp.zeros((2, LANES, 2 * width), F32)
    wa = wa.at[:, :rank, :width].set(p['rw_w_up']).at[:, rank:, width:].set(p['rw_a_up'])
    params = [p['rw_mu'], wa, p['rw_g_up'], p['rw_w0'], p['rw_a0'], p['rw_k_k'].reshape(1, width),
              p['rw_k_a'].reshape(1, width), p['rw_r_k'].reshape(1, width), hs]
    full = lambda a: pl.BlockSpec(a.shape, lambda b, i: (0,) * a.ndim)
    prev, nxt = _halo_specs(tm, cols, col, t)
    out_spec = pl.BlockSpec((1, tm, width), lambda b, i: (b, i, 0))
    return pl.pallas_call(
        functools.partial(_rwkv_prep_kernel, ctx_len=ctx_len, t=t, width=width),
        grid=(nb, t // tm),
        in_specs=[pl.BlockSpec((1, tm, cols), lambda b, i: (b, i, col)), prev, nxt] + [full(a) for a in params],
        out_specs=[out_spec] * 11,
        out_shape=[jax.ShapeDtypeStruct((nb, t, width), F32)] * 11,
        compiler_params=_cparams(("parallel", "parallel")),
        name="rwkv_prep",
    )(proj, proj, proj, *params)


def _gdn_prep_kernel(*refs, ctx_len, t, nh):
    ins, conv_ref, outs = refs[:9], refs[9], refs[10:]
    start = pl.program_id(1) * ins[0].shape[1]
    width = nh * GD_HEAD
    for part in range(3):
        x_ref, xp_ref, xn_ref = ins[3 * part:3 * part + 3]
        x = x_ref[0]
        x_prev, x_next = _neighbours(x, xp_ref, xn_ref, start, ctx_len, t)
        w = conv_ref[:, part * width:(part + 1) * width]
        y = x_prev * w[0:1, :] + x * w[1:2, :] + x_next * w[2:3, :]
        y = y * jax.nn.sigmoid(y)
        if part == 2:
            outs[part][0] = y
            continue
        scale = GD_HEAD ** -0.5 if part == 0 else 1.0
        for h in range(nh):
            seg = y[:, h * GD_HEAD:(h + 1) * GD_HEAD]
            n = seg * lax.rsqrt(jnp.sum(seg * seg, axis=-1, keepdims=True) + 1e-6)
            outs[part][0, :, h * GD_HEAD:(h + 1) * GD_HEAD] = n * scale


def _gdn_prep(proj, col0, gd_conv, nh, ctx_len):
    nb, t, _ = proj.shape
    width = nh * GD_HEAD
    tm = PREP_TM
    assert gd_conv.shape[0] == 3 and ctx_len % tm == 0 and t % tm == 0
    in_specs, args = [], []
    for part in range(3):
        prev, nxt = _halo_specs(tm, width, col0 + part, t)
        in_specs += [pl.BlockSpec((1, tm, width), lambda b, i, c=col0 + part: (b, i, c)), prev, nxt]
        args += [proj, proj, proj]
    out_spec = pl.BlockSpec((1, tm, width), lambda b, i: (b, i, 0))
    return pl.pallas_call(
        functools.partial(_gdn_prep_kernel, ctx_len=ctx_len, t=t, nh=nh),
        grid=(nb, t // tm),
        in_specs=in_specs + [pl.BlockSpec(gd_conv.shape, lambda b, i: (0, 0))],
        out_specs=[out_spec] * 3,
        out_shape=[jax.ShapeDtypeStruct((nb, t, width), F32)] * 3,
        compiler_params=_cparams(("parallel", "parallel")),
        name="gdn_prep",
    )(*args, gd_conv)


def _attn_prep_kernel(q_ref, k_ref, v_ref, cos_ref, sin_ref, qg_ref, kg_ref, q_o, k_o, v_o):
    cos, sin = cos_ref[...], sin_ref[...]
    lane = lax.broadcasted_iota(jnp.int32, cos.shape, 1)
    low_half = (lane & (AT_HEAD // 2 - 1)) < AT_HEAD // 4

    def prep(seg, gain):
        n = seg * lax.rsqrt(jnp.mean(seg * seg, axis=-1, keepdims=True) + NORM_EPS) * gain
        partner = jnp.where(low_half, pltpu.roll(n, AT_HEAD - AT_HEAD // 4, 1), pltpu.roll(n, AT_HEAD // 4, 1))
        return n * cos + partner * sin

    for h in range(q_ref.shape[2] // AT_HEAD):
        sl = slice(h * AT_HEAD, (h + 1) * AT_HEAD)
        q_o[0, :, sl] = (prep(q_ref[0, :, sl], qg_ref[...]) * AT_HEAD ** -0.5).astype(BF16)
    for h in range(k_ref.shape[2] // AT_HEAD):
        sl = slice(h * AT_HEAD, (h + 1) * AT_HEAD)
        k_o[0, :, sl] = prep(k_ref[0, :, sl], kg_ref[...]).astype(BF16)
    v_o[0] = v_ref[0].astype(BF16)


def _attn_prep(proj, col_q, qw, kvw, at_q_norm, at_k_norm, rope):
    nb, t, _ = proj.shape
    tm = PREP_TM
    cos, sin = rope
    col_k = col_q * qw // kvw + qw // kvw
    blk = lambda w, c: pl.BlockSpec((1, tm, w), lambda b, i: (b, i, c))
    tab = pl.BlockSpec((tm, AT_HEAD), lambda b, i: (i, 0))
    vec = pl.BlockSpec((1, AT_HEAD), lambda b, i: (0, 0))
    out = lambda w: pl.BlockSpec((1, tm, w), lambda b, i: (b, i, 0))
    return pl.pallas_call(
        _attn_prep_kernel,
        grid=(nb, t // tm),
        in_specs=[blk(qw, col_q), blk(kvw, col_k), blk(kvw, col_k + 1), tab, tab, vec, vec],
        out_specs=[out(qw), out(kvw), out(kvw)],
        out_shape=[jax.ShapeDtypeStruct((nb, t, qw), BF16), jax.ShapeDtypeStruct((nb, t, kvw), BF16),
                   jax.ShapeDtypeStruct((nb, t, kvw), BF16)],
        compiler_params=_cparams(("parallel", "parallel")),
        name="attn_prep",
    )(proj, proj, proj, cos, sin, at_q_norm.reshape(1, AT_HEAD), at_k_norm.reshape(1, AT_HEAD))


def _mix_post_kernel(ryf, ryb, rbonus, rg, myf, myb, mo, gyf, gyb, gg, ay, lnw, lnb, mlg, gdg, hs_ref, o_ref, *, gw):
    y = ryf[0] + ryb[0]
    hs = hs_ref[...]
    inv_n = 1.0 / RW_HEAD
    mean = _mm(y, hs, "nn", 3, 1) * inv_n
    cen = y - mean
    var = _mm(cen * cen, hs, "nn", 3, 1) * inv_n
    o_ref[0, :, 0:gw] = ((cen * lax.rsqrt(var + RW_GN_EPS) * lnw[...] + lnb[...] + rbonus[0]) * rg[0]).astype(BF16)
    y = myf[0] + myb[0]
    for h in range(gw // ML_HEAD):
        sl = slice(h * ML_HEAD, (h + 1) * ML_HEAD)
        seg = y[:, sl]
        cen = seg - jnp.mean(seg, axis=-1, keepdims=True)
        var = jnp.mean(cen * cen, axis=-1, keepdims=True)
        out = cen * lax.rsqrt(var + NORM_EPS) * mlg[:, sl] * jax.nn.sigmoid(mo[0, :, sl])
        o_ref[0, :, gw + h * ML_HEAD:gw + (h + 1) * ML_HEAD] = out.astype(BF16)
    y = gyf[0] + gyb[0]
    for h in range(gw // GD_HEAD):
        sl = slice(h * GD_HEAD, (h + 1) * GD_HEAD)
        seg = y[:, sl]
        gate = gg[0, :, sl]
        out = seg * lax.rsqrt(jnp.mean(seg * seg, axis=-1, keepdims=True) + NORM_EPS) * gdg[...] * (gate * jax.nn.sigmoid(gate))
        o_ref[0, :, 2 * gw + h * GD_HEAD:2 * gw + (h + 1) * GD_HEAD] = out.astype(BF16)
    o_ref[0, :, 3 * gw:4 * gw] = ay[0].astype(BF16)


def _mix_post(rw, ml, gd, at, proj, col_o, col_g, params, hs):
    nb, t, gw = at.shape
    tm = PREP_TM
    row = lambda c=0: pl.BlockSpec((1, tm, gw), lambda b, i: (b, i, c))
    full = lambda a: pl.BlockSpec(a.shape, lambda b, i: (0,) * a.ndim)
    ln_w, ln_b, ml_g, gd_g = params
    consts = [ln_w.reshape(1, gw), ln_b.reshape(1, gw), ml_g.reshape(1, gw), gd_g.reshape(1, GD_HEAD), hs]
    return pl.pallas_call(
        functools.partial(_mix_post_kernel, gw=gw),
        grid=(nb, t // tm),
        in_specs=[row()] * 4 + [row(), row(), row(col_o)] + [row(), row(), row(col_g)] + [row()]
                 + [full(a) for a in consts],
        out_specs=pl.BlockSpec((1, tm, 4 * gw), lambda b, i: (b, i, 0)),
        out_shape=jax.ShapeDtypeStruct((nb, t, 4 * gw), BF16),
        compiler_params=_cparams(("parallel", "parallel")),
        name="mix_post",
    )(*rw, ml[0], ml[1], proj, gd[0], gd[1], proj, at, *consts)


MOD_TN = 1024
MOD_KC = 256


def _mod_kernel(s_ref, w_ref, b_ref, o_ref, *, ncond):
    d = w_ref.shape[1]
    acc = [jnp.zeros((1, w_ref.shape[2]), F32) for _ in range(ncond)]
    for kc in range(d // MOD_KC):
        rows = slice(kc * MOD_KC, (kc + 1) * MOD_KC)
        w = w_ref[0, rows, :]
        for j in range(ncond):
            acc[j] = acc[j] + jnp.sum(w * s_ref[rows, j:j + 1], axis=0, keepdims=True)
    bias = b_ref[0]
    pad = [jnp.zeros_like(bias)] * (o_ref.shape[1] - ncond)
    o_ref[0] = jnp.concatenate([a + bias for a in acc] + pad, axis=0)


def _modulation(conds, ada_w, ada_b):
    depth, d, n = ada_w.shape
    ncond = conds.shape[0]
    s_cols = jnp.pad(jax.nn.silu(conds).T, ((0, 0), (0, 8 - ncond)))
    return pl.pallas_call(
        functools.partial(_mod_kernel, ncond=ncond),
        grid=(depth, n // MOD_TN),
        in_specs=[pl.BlockSpec((d, 8), lambda l, j: (0, 0)),
                  pl.BlockSpec((1, d, MOD_TN), lambda l, j: (l, 0, j)),
                  pl.BlockSpec((1, 1, MOD_TN), lambda l, j: (l, 0, j))],
        out_specs=pl.BlockSpec((1, 8, MOD_TN), lambda l, j: (l, 0, j)),
        out_shape=jax.ShapeDtypeStruct((depth, 8, n), F32),
        compiler_params=_cparams(("parallel", "parallel")),
        name="adaln_modulation",
    )(s_cols, ada_w, ada_b.reshape(depth, 1, n))


PROJ_TM = 544
PROJ_TN = 1792


def _modulated_norm(x, gain, mod_ref, shift_row, row0, ctx_len):
    y = x * lax.rsqrt(jnp.mean(x * x, axis=-1, keepdims=True) + NORM_EPS) * gain
    rows = row0 + lax.broadcasted_iota(jnp.int32, (x.shape[0], 1), 0)
    is_ctx = rows < ctx_len
    shift = jnp.where(is_ctx, mod_ref[0, 0, shift_row:shift_row + 1, :], mod_ref[0, 1, shift_row:shift_row + 1, :])
    scale = jnp.where(is_ctx, mod_ref[0, 0, shift_row + 1:shift_row + 2, :],
                      mod_ref[0, 1, shift_row + 1:shift_row + 2, :])
    return y * (1.0 + scale) + shift


def _norm_proj_kernel(x_ref, g_ref, mod_ref, w_ref, o_ref, h_scr, *, tiles_per_seq, ctx_len):
    row0 = (pl.program_id(0) % tiles_per_seq) * x_ref.shape[0]

    @pl.when(pl.program_id(1) == 0)
    def _():
        h_scr[...] = _modulated_norm(x_ref[...], g_ref[...], mod_ref, 0, row0, ctx_len).astype(BF16)

    o_ref[...] = jnp.dot(h_scr[...], w_ref[...], preferred_element_type=F32)


def _norm_proj(x, gain, mod, w, ctx_len):
    nb, t, d = x.shape
    n = w.shape[1]
    tm, tn = _pick_tile(t, PROJ_TM), PROJ_TN
    assert n % tn == 0
    tps = t // tm
    out = pl.pallas_call(
        functools.partial(_norm_proj_kernel, tiles_per_seq=tps, ctx_len=ctx_len),
        grid=(nb * tps, n // tn),
        in_specs=[pl.BlockSpec((tm, d), lambda i, j: (i, 0)),
                  pl.BlockSpec((1, d), lambda i, j: (0, 0)),
                  pl.BlockSpec((1, 2, 8, d), lambda i, j: (i // tps, 0, 0, 0)),
                  pl.BlockSpec((d, tn), lambda i, j: (0, j))],
        out_specs=pl.BlockSpec((tm, tn), lambda i, j: (i, j)),
        out_shape=jax.ShapeDtypeStruct((nb * t, n), F32),
        scratch_shapes=[pltpu.VMEM((tm, d), BF16)],
        compiler_params=_cparams(("parallel", "arbitrary")),
        name="norm_in_proj",
    )(x.reshape(nb * t, d), gain.reshape(1, d), mod, w)
    return out.reshape(nb, t, n)


OUT_TM = 272


def _out_proj_kernel(y_ref, w_ref, x_ref, g_ref, mod_ref, wr_hi, wr_lo,
                     xo_ref, h_ref, aff_ref, *, tiles_per_seq, ctx_len, n_experts):
    acc = jnp.dot(y_ref[...], w_ref[...], preferred_element_type=F32)
    tm = x_ref.shape[0]
    row0 = (pl.program_id(0) % tiles_per_seq) * tm
    rows = row0 + lax.broadcasted_iota(jnp.int32, (tm, 1), 0)
    gate = jnp.where(rows < ctx_len, mod_ref[0, 0, 2:3, :], mod_ref[0, 1, 2:3, :])
    x_new = x_ref[...] + gate * acc
    xo_ref[...] = x_new
    h = _modulated_norm(x_new, g_ref[...], mod_ref, 3, row0, ctx_len)
    h_ref[...] = h
    h_hi = h.astype(BF16)
    h_lo = (h - h_hi.astype(F32)).astype(BF16)
    logits = (jnp.dot(h_hi, wr_hi[...], preferred_element_type=F32)
              + jnp.dot(h_lo, wr_hi[...], preferred_element_type=F32)
              + jnp.dot(h_hi, wr_lo[...], preferred_element_type=F32))
    lane = lax.broadcasted_iota(jnp.int32, logits.shape, 1)
    logits = jnp.where(lane < n_experts, logits, -jnp.inf)
    p = jnp.exp(logits - jnp.max(logits, axis=-1, keepdims=True))
    aff_ref[...] = p / jnp.sum(p, axis=-1, keepdims=True)


def _out_proj(ymix, w_out, x, gain, mod, w_router, ctx_len):
    nb, t, d = x.shape
    mw = ymix.shape[-1]
    ne = w_router.shape[1]
    tm = _pick_tile(t, OUT_TM)
    tps = t // tm
    wr =jnp.pad(w_router, ((0, 0), (0, LANES - ne)))
    wr_hi = wr.astype(BF16)
    wr_lo = (wr - wr_hi.astype(F32)).astype(BF16)
    row = lambda w: pl.BlockSpec((tm, w), lambda i: (i, 0))
    full = lambda a: pl.BlockSpec(a.shape, lambda i: (0,) * a.ndim)
    xo, h, aff = pl.pallas_call(
        functools.partial(_out_proj_kernel, tiles_per_seq=tps, ctx_len=ctx_len, n_experts=ne),
        grid=(nb * tps,),
        in_specs=[row(mw), full(w_out), row(d), pl.BlockSpec((1, d), lambda i: (0, 0)),
                  pl.BlockSpec((1, 2, 8, d), lambda i: (i // tps, 0, 0, 0)), full(wr_hi), full(wr_lo)],
        out_specs=[row(d), row(d), row(LANES)],
        out_shape=[jax.ShapeDtypeStruct((nb * t, d), F32), jax.ShapeDtypeStruct((nb * t, d), F32),
                   jax.ShapeDtypeStruct((nb * t, LANES), F32)],
        compiler_params=_cparams(("parallel",)),
        name="out_proj_norm_router",
    )(ymix.reshape(nb * t, mw), w_out, x.reshape(nb * t, d), gain.reshape(1, d), mod, wr_hi, wr_lo)
    return xo, h, aff.reshape(nb, t, LANES)


ROUTE_BLK = 256
ROUTE_ROWS = 8


def _route_kernel(aff_ref, *refs, segments, n_experts):
    out_refs, t_scr = refs[:-1], refs[-1]
    for (start, length), o_ref in zip(segments, out_refs):
        cap = EC_FACTOR * length // n_experts
        capp = o_ref.shape[2]
        x = aff_ref[0, start:start + length, :]

        def refine(i, thr):
            cand = thr | jnp.left_shift(jnp.int32(1), 30 - i)
            cnt = jnp.sum(jnp.where(x >= pltpu.bitcast(cand, F32), 1.0, 0.0), axis=0, keepdims=True)
            return jnp.where(cnt >= cap, cand, thr)

        thr = lax.fori_loop(0, 31, refine, jnp.zeros((1, LANES), jnp.int32))
        thr_f = pltpu.bitcast(thr, F32)
        need = cap - jnp.sum(jnp.where(x > thr_f, 1.0, 0.0), axis=0, keepdims=True)
        blk = min(ROUTE_BLK, length)
        before = (lax.broadcasted_iota(jnp.int32, (blk, blk), 1)
                  < lax.broadcasted_iota(jnp.int32, (blk, blk), 0)).astype(BF16)
        lane = lax.broadcasted_iota(jnp.int32, (blk, LANES), 1)
        slot_ids = lax.broadcasted_iota(jnp.int32, (capp, blk), 0).astype(F32)
        src = lax.broadcasted_iota(jnp.int32, (LANES, LANES), 0)
        dst = lax.broadcasted_iota(jnp.int32, (LANES, LANES), 1)
        assert start % blk == 0 and length % blk == 0 and blk == t_scr.shape[2]
        o_ref[0] = jnp.zeros((LANES, capp), F32)

        def block(kb, carry, start=start, blk=blk, thr_f=thr_f, need=need, before=before, lane=lane,
                  slot_ids=slot_ids, src=src, dst=dst, o_ref=o_ref):
            carry_tied, carry_sel = carry
            xb = aff_ref[0, pl.ds(pl.multiple_of(start + kb * blk, blk), blk), :]
            above_b = xb > thr_f
            tied_b = xb == thr_f
            tied_f = jnp.where(tied_b, 1.0, 0.0)
            tied_before = _mm(before, tied_f) + carry_tied
            sel_f = jnp.where(jnp.logical_or(above_b, jnp.logical_and(tied_b, tied_before < need)), 1.0, 0.0)
            slot = _mm(before, sel_f) + carry_sel
            tok = kb * blk + lax.broadcasted_iota(jnp.int32, (blk, LANES), 0)
            vals = jnp.where((lane & 7) == 0, (tok >> 6).astype(F32),
                             jnp.where((lane & 7) == 1, (tok & 63).astype(F32), 0.0))
            for r, piece in enumerate(_split(xb, 3)):
                to_lane = jnp.logical_and(dst == src * ROUTE_ROWS + 2 + r, src < n_experts)
                vals = vals + _mm(piece, to_lane.astype(BF16))
            t_scr[0] = slot.T
            t_scr[1] = sel_f.T
            t_scr[2] = vals.T

            def expert(e, _):
                row_slot = t_scr[0, pl.ds(e, 1), :]
                row_sel = t_scr[1, pl.ds(e, 1), :]
                rows = pl.ds(pl.multiple_of(e * ROUTE_ROWS, ROUTE_ROWS), ROUTE_ROWS)
                hit = jnp.logical_and(row_sel > 0.5, row_slot == slot_ids)
                o_ref[0, rows, :] += lax.dot_general(t_scr[2, rows, :], jnp.where(hit, 1.0, 0.0), _DIMS["nt"],
                                                     preferred_element_type=F32)
                return 0

            lax.fori_loop(0, n_experts, expert, 0, unroll=2)
            return (carry_tied + jnp.sum(tied_f, axis=0, keepdims=True),
                    carry_sel + jnp.sum(sel_f, axis=0, keepdims=True))

        zero = jnp.zeros((1, LANES), F32)
        lax.fori_loop(0, length // blk, block, (zero, zero))


def _route(aff, segments, n_experts):
    nb, t, _ = aff.shape
    caps = [EC_FACTOR * length // n_experts for _, length in segments]
    capps = [-(-c // LANES) * LANES for c in caps]
    outs = pl.pallas_call(
        functools.partial(_route_kernel, segments=tuple(segments), n_experts=n_experts),
        grid=(nb,),
        in_specs=[pl.BlockSpec((1, t, LANES), lambda b: (b, 0, 0))],
        out_specs=[pl.BlockSpec((1, LANES, cp), lambda b: (b, 0, 0)) for cp in capps],
        out_shape=[jax.ShapeDtypeStruct((nb, LANES, cp), F32) for cp in capps],
        scratch_shapes=[pltpu.VMEM((3, LANES, ROUTE_BLK), F32)],
        compiler_params=_cparams(("parallel",)),
        name="expert_choice_route",
    )(aff)
    picks = []
    for o, cap in zip(outs, caps):
        v = o.reshape(nb, LANES // ROUTE_ROWS, ROUTE_ROWS, -1)[:, :n_experts, :, :cap]
        tok = jnp.round(v[:, :, 0] * 64.0 + v[:, :, 1]).astype(jnp.int32)
        picks.append((tok, v[:, :, 2] + v[:, :, 3] + v[:, :, 4]))
    return picks


FFN_TF = 256


def _ffn_kernel(x_ref, w1_ref, w3_ref, w2_ref, o_ref):
    @pl.when(pl.program_id(1) == 0)
    def _():
        o_ref[...] = jnp.zeros_like(o_ref)

    xs = x_ref[0]
    a = jnp.dot(xs, w1_ref[0].astype(BF16), preferred_element_type=F32)
    b = jnp.dot(xs, w3_ref[0].astype(BF16), preferred_element_type=F32)
    hid = (a * jax.nn.sigmoid(a) * b).astype(BF16)
    o_ref[0] += jnp.dot(hid, w2_ref[0].astype(BF16), preferred_element_type=F32)


def _expert_ffn(xs, w1, w3, w2):
    ne, r, d = xs.shape
    f = w1.shape[2]
    tf = FFN_TF
    return pl.pallas_call(
        _ffn_kernel,
        grid=(ne, f // tf),
        in_specs=[pl.BlockSpec((1, r, d), lambda e, j: (e, 0, 0)),
                  pl.BlockSpec((1, d, tf), lambda e, j: (e, 0, j)),
                  pl.BlockSpec((1, d, tf), lambda e, j: (e, 0, j)),
                  pl.BlockSpec((1, tf, d), lambda e, j: (e, j, 0))],
        out_specs=pl.BlockSpec((1, r, d), lambda e, j: (e, 0, 0)),
        out_shape=jax.ShapeDtypeStruct((ne, r, d), F32),
        compiler_params=_cparams(("parallel", "arbitrary")),
        name="expert_ffn",
    )(xs, w1, w3, w2)


def _moe_ffn_kernel(idx_ref, gate_ref, g2_ref, w1_ref, w3_ref, w2_ref, h_hbm, x_hbm, xo_hbm,
                    xg, xs_bf, acc, rows, gsem, rsem, wsem, *, groups, ne, nj):
    del x_hbm
    e, j = pl.program_id(0), pl.program_id(1)
    r_tot = acc.shape[0]
    half = nj // 2
    per, wper = r_tot // nj, r_tot // half
    e_prev, e_next = jnp.maximum(e - 1, 0), jnp.minimum(e + 1, ne - 1)

    def row_copy(src, dst, s, d_, sem):
        return pltpu.make_async_copy(src.at[pl.ds(s, 1), :], dst.at[pl.ds(d_, 1), :], sem)

    def for_rows(n, fn):
        def body(i, c):
            fn(i)
            return c
        lax.fori_loop(0, n, body, 0, unroll=8)

    gather = lambda expert, r: row_copy(h_hbm, xg, idx_ref[expert, r], r, gsem)
    read = lambda expert, r: row_copy(xo_hbm, rows, idx_ref[expert, r], r, rsem)
    write = lambda expert, r: row_copy(rows, xo_hbm, r, idx_ref[expert, r], wsem)

    @pl.when(jnp.logical_and(e == 0, j == 0))
    def _():
        for_rows(r_tot, lambda r: gather(0, r).start())
        for_rows(r_tot, lambda r: read(0, r).start())
        for_rows(r_tot, lambda r: row_copy(h_hbm, xg, 0, r, gsem).wait())
        for_rows(r_tot, lambda r: row_copy(xo_hbm, rows, 0, r, rsem).wait())

    @pl.when(j == 0)
    def _():
        xs_bf[...] = xg[...].astype(BF16)
        acc[...] = jnp.zeros_like(acc)

    @pl.when(j == half)
    def _():
        for_rows(r_tot, lambda r: row_copy(rows, xo_hbm, r, 0, wsem).wait())

    def ffn_step():
        xs = xs_bf[...]
        a = jnp.dot(xs, w1_ref[0, 0].astype(BF16), preferred_element_type=F32)
        b = jnp.dot(xs, w3_ref[0, 0].astype(BF16), preferred_element_type=F32)
        hid = (a * jax.nn.sigmoid(a) * b).astype(BF16)
        acc[...] += jnp.dot(hid, w2_ref[0, 0].astype(BF16), preferred_element_type=F32)

    def first_half():
        for i in range(per):
            gather(e_next, j * per + i).start()
        for i in range(wper):
            write(e_prev, j * wper + i).start()
        ffn_step()

    def second_half():
        for i in range(per):
            gather(e_next, j * per + i).start()
        for i in range(wper):
            read(e, (j - half) * wper + i).start()
        ffn_step()

    lax.cond(j < half, first_half, second_half)

    @pl.when(j == nj - 1)
    def _():
        for_rows(r_tot, lambda r: row_copy(xo_hbm, rows, 0, r, rsem).wait())
        for g, (lo, n) in enumerate(groups):
            rows[lo:lo + n, :] += (gate_ref[0, lo:lo + n, :] * acc[lo:lo + n, :]) * g2_ref[g:g + 1, :]
        for_rows(r_tot, lambda r: row_copy(h_hbm, xg, 0, r, gsem).wait())

        @pl.when(e == ne - 1)
        def _():
            for_rows(r_tot, lambda r: write(e, r).start())
            for_rows(r_tot, lambda r: row_copy(rows, xo_hbm, r, 0, wsem).wait())


def _moe_ffn(x, h, idx, gate, g2, groups, w1, w3, w2, layer):
    n, d = x.shape
    ne, r = idx.shape
    f = w1.shape[3]
    tf = FFN_TF
    nj = f // tf
    assert nj % 2 == 0 and r % nj == 0
    grid_spec = pltpu.PrefetchScalarGridSpec(
        num_scalar_prefetch=1,
        grid=(ne, nj),
        in_specs=[pl.BlockSpec((1, r, 1), lambda e, j, idx: (e, 0, 0)),
                  pl.BlockSpec(g2.shape, lambda e, j, idx: (0, 0)),
                  pl.BlockSpec((1, 1, d, tf), lambda e, j, idx: (layer, e, 0, j)),
                  pl.BlockSpec((1, 1, d, tf), lambda e, j, idx: (layer, e, 0, j)),
                  pl.BlockSpec((1, 1, tf, d), lambda e, j, idx: (layer, e, j, 0)),
                  pl.BlockSpec(memory_space=pl.ANY),
                  pl.BlockSpec(memory_space=pl.ANY)],
        out_specs=pl.BlockSpec(memory_space=pl.ANY),
        scratch_shapes=[pltpu.VMEM((r, d), F32), pltpu.VMEM((r, d), BF16), pltpu.VMEM((r, d), F32),
                        pltpu.VMEM((r, d), F32), pltpu.SemaphoreType.DMA, pltpu.SemaphoreType.DMA,
                        pltpu.SemaphoreType.DMA])
    return pl.pallas_call(
        functools.partial(_moe_ffn_kernel, groups=tuple(groups), ne=ne, nj=nj),
        grid_spec=grid_spec,
        out_shape=jax.ShapeDtypeStruct((n, d), F32),
        input_output_aliases={7: 0},
        compiler_params=_cparams(("arbitrary", "arbitrary")),
        name="moe_ffn",
    )(idx, gate.reshape(ne, r, 1), g2, w1, w3, w2, h, x)


def _moe_glue_reference(h, aff, segments, w1, w3, w2):
    nb, t, d = h.shape
    ne = aff.shape[-1]
    picks, xs = [], []
    for start, length in segments:
        cap = EC_FACTOR * length // ne
        gate, idx = lax.top_k(jnp.swapaxes(aff[:, start:start + length], 1, 2), cap)
        idx = idx + start
        picks.append((gate, idx))
        rows = jnp.take_along_axis(h[:, None], idx[..., None], axis=2)
        xs.append(jnp.swapaxes(rows, 0, 1).reshape(ne, nb * cap, d))
    y = _expert_ffn(jnp.concatenate(xs, axis=1), w1, w3, w2)
    out = jnp.zeros((nb, t, d), F32)
    bidx = jnp.arange(nb)[:, None, None]
    off = 0
    for (start, length), (gate, idx) in zip(segments, picks):
        cap = gate.shape[-1]
        ys = jnp.swapaxes(y[:, off:off + nb * cap].reshape(ne, nb, cap, d), 0, 1)
        out = out.at[bidx, idx].add(ys * gate[..., None])
        off += nb * cap
    return out


def _final_norm_kernel(x_ref, g_ref, o_ref):
    x = x_ref[0]
    o_ref[0] = x * lax.rsqrt(jnp.mean(x * x, axis=-1, keepdims=True) + NORM_EPS) * g_ref[...]


def _final_norm(xs, gain, ctx_len):
    nb, t, d = xs.shape
    tm = PREP_TM
    assert ctx_len % tm == 0 and t % tm == 0
    skip = ctx_len // tm
    return pl.pallas_call(
        _final_norm_kernel,
        grid=(nb, (t - ctx_len) // tm),
        in_specs=[pl.BlockSpec((1, tm, d), lambda b, i: (b, i + skip, 0)), pl.BlockSpec((1, d), lambda b, i: (0, 0))],
        out_specs=pl.BlockSpec((1, tm, d), lambda b, i: (b, i, 0)),
        out_shape=jax.ShapeDtypeStruct((nb, t - ctx_len, d), F32),
        compiler_params=_cparams(("parallel", "parallel")),
        name="final_norm",
    )(xs, gain.reshape(1, d))


RELAYOUT_TM = 256


def _relayout_kernel(w_ref, o_ref, *, gw, rw_cols, ngate):
    ml0 = rw_cols
    gd0 = ml0 + 4 * gw + ngate
    at0 = gd0 + 4 * gw + ngate
    o_ref[:, 0:rw_cols] = w_ref[:, 0:rw_cols].astype(BF16)
    o_ref[:, rw_cols:rw_cols + ngate] = w_ref[:, ml0 + 4 * gw:gd0].astype(BF16)
    o_ref[:, rw_cols + ngate:rw_cols + 2 * ngate] = w_ref[:, gd0 + 4 * gw:at0].astype(BF16)
    o_ref[:, rw_cols + 2 * ngate:4 * gw] = jnp.zeros((o_ref.shape[0], 4 * gw - rw_cols - 2 * ngate), BF16)
    o_ref[:, 4 * gw:8 * gw] = w_ref[:, ml0:ml0 + 4 * gw].astype(BF16)
    o_ref[:, 8 * gw:12 * gw] = w_ref[:, gd0:gd0 + 4 * gw].astype(BF16)
    o_ref[:, 12 * gw:] = w_ref[:, at0:].astype(BF16)


def _relayout_w_in(w_in, layer, gw, rw_cols, ngate):
    _, d, n = w_in.shape
    assert rw_cols % LANES == 0 and rw_cols + 2 * ngate <= 4 * gw
    n_out = n - rw_cols - 2 * ngate + 4 * gw
    tm = RELAYOUT_TM
    return pl.pallas_call(
        functools.partial(_relayout_kernel, gw=gw, rw_cols=rw_cols, ngate=ngate),
        grid=(d // tm,),
        in_specs=[pl.BlockSpec((None, tm, n), lambda i: (layer, i, 0))],
        out_specs=pl.BlockSpec((tm, n_out), lambda i: (i, 0)),
        out_shape=jax.ShapeDtypeStruct((d, n_out), BF16),
        compiler_params=_cparams(("parallel",)),
        name="w_in_relayout",
    )(w_in)


def kernel(x, c, ctx, c_ctx, ada_w, ada_b, norm1_g, norm2_g, w_in, w_out, rw_mu, rw_w0, rw_w_up, rw_a0, rw_a_up,
           rw_g_up, rw_k_k, rw_k_a, rw_r_k, rw_ln_w, rw_ln_b, ml_ib, ml_fb, ml_norm_g, gd_conv, gd_a_log,
           gd_dt_bias, gd_norm_g, at_q_norm, at_k_norm, w_router, w_exp1, w_exp3, w_exp2, final_g):
    nb, n_lat, d = x.shape
    ctx_len = ctx.shape[1]
    t = ctx_len + n_lat
    depth = ada_w.shape[0]
    gw = d // N_GROUPS
    rw_cols = rw_mu.shape[-1]
    ngate = 4 * ml_ib.shape[-1]
    at_cols = w_in.shape[-1] - rw_cols - 2 * (4 * gw + ngate)

    xs = jnp.concatenate([ctx, x], axis=1)
    mods = _modulation(jnp.concatenate([c, c_ctx[None]], axis=0), ada_w, ada_b)
    rope = _rope_tables(t, ctx_len)
    nh = ml_ib.shape[-1]
    assert 4 * gw == 2048 and gw == nh * ML_HEAD == nh * GD_HEAD and at_cols == 2 * gw
    head_of = jnp.arange(gw) // RW_HEAD
    hs = (head_of[:, None] == head_of[None, :]).astype(BF16)

    for layer in range(depth):
        last = layer == depth - 1
        m = mods[layer, :nb + 1].reshape(nb + 1, 6, d)
        m = jnp.pad(m, ((0, 0), (0, 2), (0, 0)))
        mod = jnp.stack([jnp.broadcast_to(m[nb], (nb, 8, d)), m[:nb]], axis=1)
        proj = _norm_proj(xs, norm1_g[layer], mod, _relayout_w_in(w_in, layer, gw, rw_cols, ngate), ctx_len)
        p_rw = {k: v[layer] for k, v in dict(
            rw_mu=rw_mu, rw_w0=rw_w0, rw_w_up=rw_w_up, rw_a0=rw_a0, rw_a_up=rw_a_up, rw_g_up=rw_g_up,
            rw_k_k=rw_k_k, rw_k_a=rw_k_a, rw_r_k=rw_r_k).items()}
        graw = proj[..., rw_cols:rw_cols + 2 * ngate]
        r, v, kk, lwf, bf, ktf, lwb, bb, ktb, g_rw, bonus = _rwkv_prep(proj, 0, p_rw, hs, ctx_len)
        ryf, ryb = _rwkv_scan(r, v, kk, [(lwf, bf, ktf), (lwb, bb, ktb)], ctx_len)

        gm = graw[..., :ngate]
        li = gm[..., :2 * nh] + ml_ib[layer].reshape(-1)
        lf = jax.nn.log_sigmoid(gm[..., 2 * nh:] + ml_fb[layer].reshape(-1))
        g_ml = jnp.concatenate([li, lf], axis=-1)
        myf, myb = _mlstm_scan(proj, proj, proj, (4, 5, 6), g_ml, jnp.swapaxes(g_ml, 1, 2), ctx_len)

        gg = graw[..., ngate:]
        lg = -jnp.exp(gd_a_log[layer]).reshape(-1) * jax.nn.softplus(gg[..., :2 * nh] + gd_dt_bias[layer].reshape(-1))
        q_gd, k_gd, v_gd = _gdn_prep(proj, 8, gd_conv[layer], nh, ctx_len)
        gyf, gyb = _gdn_scan(q_gd, k_gd, v_gd, jnp.concatenate([lg, jax.nn.sigmoid(gg[..., 2 * nh:])], axis=-1), ctx_len)

        kvw = AT_KV_HEADS * AT_HEAD
        q_at, k_at, v_at = _attn_prep(proj, 12, at_cols - 2 * kvw, kvw, at_q_norm[layer], at_k_norm[layer], rope)
        ay = _attention(q_at, k_at, v_at, ctx_len)

        ymix = _mix_post((ryf, ryb, bonus, g_rw), (myf, myb), (gyf, gyb), ay, proj, 7, 11,
                         (rw_ln_w[layer], rw_ln_b[layer], ml_norm_g[layer], gd_norm_g[layer]), hs)
        xs_flat, h2, aff = _out_proj(ymix, w_out[layer].astype(BF16), xs, norm2_g[layer], mod, w_router[layer],
                                     ctx_len)
        segments = [(ctx_len, n_lat)] if last else [(0, ctx_len), (ctx_len, n_lat)]
        ne = w_router.shape[-1]
        idx_parts, gate_parts, groups, g2_rows, off = [], [], [], [], 0
        for (start, length), (tok, gate) in zip(segments, _route(aff, segments, ne)):
            for b in range(nb):
                idx_parts.append(tok[b] + (b * t + start))
                gate_parts.append(gate[b])
                groups.append((off, tok.shape[-1]))
                off += tok.shape[-1]
                g2_rows.append(mod[b, 0 if start < ctx_len else 1, 5])
        xs = _moe_ffn(xs_flat, h2, jnp.concatenate(idx_parts, axis=1), jnp.concatenate(gate_parts, axis=1),
                      jnp.stack(g2_rows), groups, w_exp1, w_exp3, w_exp2, layer).reshape(nb, t, d)

    return _final_norm(xs, final_g, ctx_len)
```

```python
import functools
import math

import jax
import jax.numpy as jnp
from jax import lax
from jax.experimental import pallas as pl
from jax.experimental.pallas import tpu as pltpu

F32 = jnp.float32
BF16 = jnp.bfloat16

NORM_EPS = 1e-6
GRID_W = 64
N_GROUPS = 4
RW_HEAD = 64
RW_DECAY_RANK = 64
RW_ICLR_RANK = 64
RW_GATE_RANK = 128
RW_DECAY_SCALE = math.exp(-0.5)
RW_GN_EPS = 64e-5
RW_CHUNK = 64
ML_HEAD = 128
ML_CHUNK = 256
GD_HEAD = 128
GD_CHUNK = 64
AT_HEAD = 128
AT_KV_HEADS = 2
ROPE_THETA = 10000.0
AT_TQ = 256
EC_FACTOR = 2

VMEM_LIMIT = 56 * 1024 * 1024
LANES = 128


def _cparams(sem):
    return pltpu.CompilerParams(dimension_semantics=sem, vmem_limit_bytes=VMEM_LIMIT)


def _pick_tile(n, limit, mult=16):
    best = None
    for cand in range(mult, min(n, limit) + 1, mult):
        if n % cand == 0:
            best = cand
    assert best is not None, (n, limit, mult)
    return best


_DIMS = {
    "nn": (((1,), (0,)), ((), ())),
    "nt": (((1,), (1,)), ((), ())),
    "tn": (((0,), (0,)), ((), ())),
}


def _split(a, n):
    if a.dtype == BF16:
        return [a]
    pieces, rest = [], a
    for i in range(n):
        p = rest.astype(BF16)
        pieces.append(p)
        if i + 1 < n:
            rest = rest - p.astype(F32)
    return pieces


def _mm(a, b, dims="nn", pa=1, pb=1):
    ap, bp = _split(a, pa), _split(b, pb)
    top = max(len(ap), len(bp))
    acc = None
    for i, x in enumerate(ap):
        for j, y in enumerate(bp):
            if i + j < top:
                t = lax.dot_general(x, y, _DIMS[dims], preferred_element_type=F32)
                acc = t if acc is None else acc + t
    return acc


def _tri(n, rev, strict, reps=1):
    t = lax.broadcasted_iota(jnp.int32, (n, reps * n), 0)
    s = lax.broadcasted_iota(jnp.int32, (n, reps * n), 1)
    if reps > 1:
        assert n & (n - 1) == 0
        s = s & (n - 1)
    if rev:
        return (s > t) if strict else (s >= t)
    return (s < t) if strict else (s <= t)


def _chunk_maps(nsteps, nctx):
    fwd = lambda i: i
    bwd = lambda i: jnp.where(i < nctx, nctx - 1 - i, nsteps - 1 - i + nctx)
    return fwd, bwd


def _mlstm_kernel(qf, kf, vf, gf, gtf, qb, kb, vb, gb, gtb, yf, yb, c_scr, m_scr, *, nb, nh, chunk):
    step = pl.program_id(0)

    @pl.when(step == 0)
    def _():
        c_scr[...] = jnp.zeros_like(c_scr)
        m_scr[...] = jnp.zeros_like(m_scr)

    scale = ML_HEAD ** -0.5
    lane = lax.broadcasted_iota(jnp.int32, (chunk, ML_HEAD), 1)
    ones_col = (lane == 0).astype(F32)
    units = []
    for d, (q_ref, k_ref, v_ref, g_ref, gt_ref, y_ref) in enumerate(
            ((qf, kf, vf, gf, gtf, yf), (qb, kb, vb, gb, gtb, yb))):
        rev = d == 1
        incl = _tri(chunk, rev, False)
        cum_m = incl.astype(BF16)
        last = 0 if rev else chunk - 1
        for b in range(nb):
            g = g_ref[b]
            gt = gt_ref[b]
            cum_c = _mm(cum_m, g, "nn", 1, 3)
            cum_r = _mm(gt, cum_m, "nt", 3, 1)
            for h in range(nh):
                ci, cf = d * nh + h, (2 + d) * nh + h
                sl = slice(h * ML_HEAD, (h + 1) * ML_HEAD)
                u = dict(y_ref=y_ref, b=b, sl=sl, idx=(d * nb + b) * nh + h, last=last)
                u["q"] = q_ref[b, :, sl].astype(BF16)
                u["k"] = k_ref[b, :, sl] * scale
                u["v_aug"] = jnp.concatenate([v_ref[b, :, sl], ones_col], axis=1).astype(BF16)
                u["m_old"] = m_scr[u["idx"]][0:1, 0:1]
                u["bc"] = cum_c[:, cf:cf + 1]
                u["li_c"] = g[:, ci:ci + 1]
                dlog = jnp.where(incl, u["bc"] - cum_r[cf:cf + 1, :] + gt[ci:ci + 1, :], -jnp.inf)
                inter = u["bc"] + u["m_old"]
                u["mt"] = jnp.maximum(inter, jnp.max(dlog, axis=1, keepdims=True))
                u["p"] = jnp.exp(dlog - u["mt"])
                u["wi"] = jnp.exp(inter - u["mt"])
                units.append(u)
    for u in units:
        u["s"] = (_mm(u["q"], u["k"], "nt") * u["p"]).astype(BF16)
    for u in units:
        u["c_old"] = c_scr[u["idx"]]
        u["acc"] = _mm(u["s"], u["v_aug"]) + u["wi"] * _mm(u["q"], u["c_old"])
    for u in units:
        num = u["acc"][:, :ML_HEAD]
        den = u["acc"][:, ML_HEAD:ML_HEAD + 1]
        u["y_ref"][u["b"], :, u["sl"]] = num / jnp.maximum(jnp.abs(den), jnp.exp(-u["mt"]))
    for u in units:
        last = u["last"]
        m_new = u["mt"][last:last + 1, :]
        b_last = u["bc"][last:last + 1, :]
        wk = jnp.exp(b_last - u["bc"] + u["li_c"] - m_new)
        dc = jnp.exp(b_last + u["m_old"] - m_new)
        c_scr[u["idx"]] = dc * u["c_old"] + _mm(u["k"] * wk, u["v_aug"], "tn")
        m_scr[u["idx"]] = jnp.broadcast_to(m_new, (8, LANES))


def _mlstm_scan(q_src, k_src, v_src, cols, g, gt, ctx_len):
    nb, t, _ = g.shape
    nh = g.shape[-1] // 4
    width = nh * ML_HEAD
    chunk = ML_CHUNK
    nsteps, nctx = t // chunk, ctx_len // chunk
    fwd, bwd = _chunk_maps(nsteps, nctx)

    def specs(cm):
        col = lambda c: pl.BlockSpec((nb, chunk, width), lambda i, c=c: (0, cm(i), c))
        return [col(cols[0]), col(cols[1]), col(cols[2]),
                pl.BlockSpec((nb, chunk, 4 * nh), lambda i: (0, cm(i), 0)),
                pl.BlockSpec((nb, 4 * nh, chunk), lambda i: (0, 0, cm(i)))]

    out_spec = lambda cm: pl.BlockSpec((nb, chunk, width), lambda i: (0, cm(i), 0))
    kern = functools.partial(_mlstm_kernel, nb=nb, nh=nh, chunk=chunk)
    return pl.pallas_call(
        kern,
        grid=(nsteps,),
        in_specs=specs(fwd) + specs(bwd),
        out_specs=[out_spec(fwd), out_spec(bwd)],
        out_shape=[jax.ShapeDtypeStruct((nb, t, width), F32)] * 2,
        scratch_shapes=[pltpu.VMEM((2 * nb * nh, ML_HEAD, 2 * ML_HEAD), F32),
                        pltpu.VMEM((2 * nb * nh, 8, LANES), F32)],
        compiler_params=_cparams(("arbitrary",)),
        name="mlstm_scan",
    )(q_src, k_src, v_src, g, gt, q_src, k_src, v_src, g, gt)


def _gdn_kernel(qf, kf, vf, gf, gtf, qb, kb, vb, gb, gtb, yf, yb, s_scr, *, nb, nh, chunk):
    step = pl.program_id(0)

    @pl.when(step == 0)
    def _():
        s_scr[...] = jnp.zeros_like(s_scr)

    levels = int(math.log2(chunk)) - 1
    c = chunk
    dh = GD_HEAD
    pw = 2 * dh
    first = lax.broadcasted_iota(jnp.int32, (c, pw), 1) < dh
    first2 = lax.broadcasted_iota(jnp.int32, (c, 2 * c), 1) < c
    first2_row = lax.broadcasted_iota(jnp.int32, (1, 2 * c), 1) < c
    first4 = (lax.broadcasted_iota(jnp.int32, (c, 2 * pw), 1) // dh) % 2 == 0
    same_head = (lax.broadcasted_iota(jnp.int32, (pw, pw), 0) // dh
                 == lax.broadcasted_iota(jnp.int32, (pw, pw), 1) // dh)
    same_blk = (lax.broadcasted_iota(jnp.int32, (2 * c, 2 * c), 0) // c
                == lax.broadcasted_iota(jnp.int32, (2 * c, 2 * c), 1) // c)
    eye2 = (lax.broadcasted_iota(jnp.int32, (c, 2 * c), 0)
            == lax.broadcasted_iota(jnp.int32, (c, 2 * c), 1) % c).astype(F32)

    def by_head(x, mask):
        return jnp.concatenate([jnp.where(mask, x, 0.0), jnp.where(mask, 0.0, x)], axis=0)

    def blockdiag(x):
        xb = x.astype(BF16)
        return jnp.where(same_blk, jnp.concatenate([xb, xb], axis=0), jnp.zeros((), BF16))

    units = []
    for d, (q_ref, k_ref, v_ref, g_ref, gt_ref, y_ref) in enumerate(
            ((qf, kf, vf, gf, gtf, yf), (qb, kb, vb, gb, gtb, yb))):
        rev = d == 1
        incl = _tri(c, rev, False)
        incl2 = _tri(c, rev, False, reps=2)
        strict2 = _tri(c, rev, True, reps=2)
        cum_m = incl.astype(BF16)
        cum_m2 = jnp.concatenate([cum_m, cum_m], axis=0)
        last = 0 if rev else c - 1
        for b in range(nb):
            g = g_ref[b]
            gt = gt_ref[b, 0]
            cum_c = _mm(cum_m, g, "nn", 1, 3)
            cum_r = _mm(gt, cum_m2, "nt", 3, 1)
            for hp in range(nh // 2):
                h0, h1 = 2 * hp, 2 * hp + 1
                sl = slice(hp * pw, (hp + 1) * pw)
                u = dict(y_ref=y_ref, b=b, sl=sl, idx=(d * nb + b) * (nh // 2) + hp, strict2=strict2)
                q, k, v = q_ref[b, :, sl], k_ref[b, :, sl], v_ref[b, :, sl]
                col = lambda arr, j: jnp.where(first, arr[:, j + h0:j + h0 + 1], arr[:, j + h1:j + h1 + 1])
                gc = col(cum_c, d * nh)
                beta = col(g, (2 + d) * nh)
                gc_c2 = jnp.where(first2, cum_c[:, d * nh + h0:d * nh + h0 + 1], cum_c[:, d * nh + h1:d * nh + h1 + 1])
                gc_r2 = jnp.where(first2_row, cum_r[d * nh + h0:d * nh + h0 + 1], cum_r[d * nh + h1:d * nh + h1 + 1])
                u["decay"] = jnp.where(incl2, jnp.exp(jnp.where(incl2, gc_c2 - gc_r2, 0.0)), 0.0)
                kb_ = k * beta
                eg = jnp.exp(gc)
                u["kq"] = jnp.concatenate([kb_, q], axis=0).astype(BF16)
                u["k_blk"] = by_head(k, first).astype(BF16)
                u["rhs"] = by_head(jnp.concatenate([v * beta, kb_ * eg], axis=1), first4).astype(BF16)
                u["qg"] = q * eg
                g_last = gc[last:last + 1, :]
                u["kd"] = (k * jnp.exp(g_last - gc)).astype(BF16)
                u["keep"] = jnp.exp(g_last)
                units.append(u)

    for u in units:
        kq = _mm(u["kq"], u["k_blk"], "nt")
        u["y"] = jnp.where(u["strict2"], -kq[:c] * u["decay"], 0.0)
        u["attn"] = (kq[c:] * u["decay"]).astype(BF16)
        u["inv"] = eye2 + u["y"]
        u["bd"] = blockdiag(u["y"])
    for _ in range(levels):
        for u in units:
            u["y"] = _mm(u["y"], u["bd"])
            u["bd"] = blockdiag(u["y"])
        for u in units:
            u["inv"] = u["inv"] + _mm(u["inv"], u["bd"])
    for u in units:
        u["sol"] = _mm(u["inv"], u["rhs"])
    for u in units:
        u["s_old"] = s_scr[u["idx"]]
        u["ws"] = _mm(jnp.concatenate([u["sol"][:, pw:], u["qg"]], axis=0), u["s_old"])
    for u in units:
        u["v_new"] = u["sol"][:, :pw] - u["ws"][:c]
        u["y_ref"][u["b"], :, u["sl"]] = u["ws"][c:] + _mm(u["attn"], by_head(u["v_new"], first))
    for u in units:
        upd = _mm(u["kd"], u["v_new"], "tn")
        s_scr[u["idx"]] = u["s_old"] * u["keep"] + jnp.where(same_head, upd, 0.0)


def _gdn_scan(q, k, v, g, ctx_len):
    nb, t, width = q.shape
    nh = width // GD_HEAD
    chunk = GD_CHUNK
    nsteps, nctx = t // chunk, ctx_len // chunk
    fwd, bwd = _chunk_maps(nsteps, nctx)
    gt = jnp.swapaxes(g.reshape(nb, nsteps, chunk, 4 * nh), 2, 3)

    def specs(cm):
        col = pl.BlockSpec((nb, chunk, width), lambda i: (0, cm(i), 0))
        return [col, col, col,
                pl.BlockSpec((nb, chunk, 4 * nh), lambda i: (0, cm(i), 0)),
                pl.BlockSpec((nb, 1, 4 * nh, chunk), lambda i: (0, cm(i), 0, 0))]

    out_spec = lambda cm: pl.BlockSpec((nb, chunk, width), lambda i: (0, cm(i), 0))
    kern = functools.partial(_gdn_kernel, nb=nb, nh=nh, chunk=chunk)
    return pl.pallas_call(
        kern,
        grid=(nsteps,),
        in_specs=specs(fwd) + specs(bwd),
        out_specs=[out_spec(fwd), out_spec(bwd)],
        out_shape=[jax.ShapeDtypeStruct((nb, t, width), F32)] * 2,
        scratch_shapes=[pltpu.VMEM((nb * nh, 2 * GD_HEAD, 2 * GD_HEAD), F32)],
        compiler_params=_cparams(("arbitrary",)),
        name="gdn_scan",
    )(q, k, v, g, gt, q, k, v, g, gt)


def _rwkv_kernel(*refs, nb, npair, chunk):
    ins, (yf, yb, s_scr) = refs[:12], refs[12:]
    step = pl.program_id(0)

    @pl.when(step == 0)
    def _():
        s_scr[...] = jnp.zeros_like(s_scr)

    levels = int(math.log2(chunk)) - 1
    pw = 2 * RW_HEAD
    c = chunk
    head0 = lax.broadcasted_iota(jnp.int32, (c, pw), 1) < RW_HEAD
    same_head = (lax.broadcasted_iota(jnp.int32, (pw, pw), 0) // RW_HEAD
                 == lax.broadcasted_iota(jnp.int32, (pw, pw), 1) // RW_HEAD)
    same_blk = (lax.broadcasted_iota(jnp.int32, (2 * c, 2 * c), 0) // c
                == lax.broadcasted_iota(jnp.int32, (2 * c, 2 * c), 1) // c)
    eye2 = (lax.broadcasted_iota(jnp.int32, (c, 2 * c), 0)
            == lax.broadcasted_iota(jnp.int32, (c, 2 * c), 1) % c).astype(F32)
    mid = c // 2

    def by_head(x):
        return jnp.concatenate([jnp.where(head0, x, 0.0), jnp.where(head0, 0.0, x)], axis=0)

    def blockdiag(x):
        xb = x.astype(BF16)
        return jnp.where(same_blk, jnp.concatenate([xb, xb], axis=0), jnp.zeros((), BF16))

    units = []
    for d, y_ref in enumerate((yf, yb)):
        r_ref, lw_ref, kt_ref, v_ref, kk_ref, b_ref = ins[6 * d:6 * d + 6]
        rev = d == 1
        incl = _tri(c, rev, False)
        strict = _tri(c, rev, True)
        incl2 = _tri(c, rev, False, reps=2)
        strict2 = _tri(c, rev, True, reps=2)
        cum_m = incl.astype(BF16)
        last = 0 if rev else c - 1
        for b in range(nb):
            lw_all = lw_ref[b]
            g_all = _mm(cum_m, lw_all, "nn", 1, 3)
            for p in range(npair):
                sl = slice(p * pw, (p + 1) * pw)
                u = dict(y_ref=y_ref, b=b, sl=sl, idx=(d * nb + b) * npair + p, incl2=incl2, strict2=strict2)
                lw, g = lw_all[:, sl], g_all[:, sl]
                r, kt, v, kk, bb = (x[b, :, sl] for x in (r_ref, kt_ref, v_ref, kk_ref, b_ref))
                gh = g - g[mid:mid + 1, :]
                e_pos, e_neg = jnp.exp(gh), jnp.exp(-gh)
                u["lhs"] = jnp.concatenate([jnp.exp(gh - lw) * kk, r * e_pos], axis=0).astype(BF16)
                u["rhs"] = jnp.concatenate([by_head(bb * e_neg), by_head(kt * e_neg)], axis=0).astype(BF16)
                u["st_lhs"] = jnp.concatenate([jnp.exp(g - lw) * kk, jnp.exp(g) * r], axis=0).astype(BF16)
                g_last = g[last:last + 1, :]
                dl = jnp.exp(g_last - g)
                u["keep"] = jnp.exp(g_last)
                u["bk_d"] = jnp.concatenate([bb * dl, kt * dl], axis=0).astype(BF16)
                u["v"] = v
                u["v_blk"] = by_head(v).astype(BF16)
                units.append(u)

    for u in units:
        u["s_old"] = s_scr[u["idx"]]
        u["from_state"] = _mm(u["st_lhs"], u["s_old"], "nt")
    for u in units:
        gm = _mm(u["lhs"], u["rhs"], "nt")
        u["y"] = jnp.where(u["strict2"], -gm[:c, :2 * c], 0.0)
        u["a_ak"] = jnp.where(u["strict2"], gm[:c, 2 * c:], 0.0).astype(BF16)
        u["r_mat"] = jnp.concatenate([jnp.where(u["incl2"], -gm[c:, :2 * c], 0.0),
                                      jnp.where(u["incl2"], gm[c:, 2 * c:], 0.0)], axis=1).astype(BF16)
        u["inv"] = eye2 + u["y"]
        u["bd"] = blockdiag(u["y"])
    for u in units:
        u["rhs_e"] = u["from_state"][:c] + _mm(u["a_ak"], u["v_blk"])
    for _ in range(levels):
        for u in units:
            u["y"] = _mm(u["y"], u["bd"])
            u["bd"] = blockdiag(u["y"])
        for u in units:
            u["inv"] = u["inv"] + _mm(u["inv"], u["bd"])
    for u in units:
        u["e"] = _mm(u["inv"], by_head(u["rhs_e"]))
    for u in units:
        ev = jnp.concatenate([by_head(u["e"]), by_head(u["v"])], axis=0)
        u["y_ref"][u["b"], :, u["sl"]] = u["from_state"][c:] + _mm(u["r_mat"], ev)
    for u in units:
        upd = _mm(jnp.concatenate([-u["e"], u["v"]], axis=0), u["bk_d"], "tn")
        s_scr[u["idx"]] = u["s_old"] * u["keep"] + jnp.where(same_head, upd, 0.0)


def _rwkv_scan(r, v, kk, per_dir, ctx_len):
    nb, t, width = r.shape
    chunk = RW_CHUNK
    nsteps, nctx = t // chunk, ctx_len // chunk
    fwd, bwd = _chunk_maps(nsteps, nctx)
    spec = lambda cm: pl.BlockSpec((nb, chunk, width), lambda i: (0, cm(i), 0))
    args = []
    for d in range(2):
        lw, bb, kt = per_dir[d]
        args += [r, lw, kt, v, kk, bb]
    kern = functools.partial(_rwkv_kernel, nb=nb, npair=width // (2 * RW_HEAD), chunk=chunk)
    return pl.pallas_call(
        kern,
        grid=(nsteps,),
        in_specs=[spec(fwd)] * 6 + [spec(bwd)] * 6,
        out_specs=[spec(fwd), spec(bwd)],
        out_shape=[jax.ShapeDtypeStruct((nb, t, width), F32)] * 2,
        scratch_shapes=[pltpu.VMEM((2 * nb * (width // (2 * RW_HEAD)), 2 * RW_HEAD, 2 * RW_HEAD), F32)],
        compiler_params=_cparams(("arbitrary",)),
        name="rwkv_scan",
    )(*args)


def _attn_kernel(q_ref, k_ref, v_ref, o_ref, *, ctx_tiles, ctx_len, group):
    qi = pl.program_id(2)

    def run(keys, vals):
        for g in range(group):
            sl = slice(g * AT_HEAD, (g + 1) * AT_HEAD)
            s = lax.dot_general(q_ref[0, :, sl], keys, _DIMS["nt"], preferred_element_type=F32)
            p = jnp.exp(s - jnp.max(s, axis=-1, keepdims=True))
            o = jnp.dot(p.astype(BF16), vals, preferred_element_type=F32)
            o_ref[0, :, sl] = o / jnp.sum(p, axis=-1, keepdims=True)

    @pl.when(qi < ctx_tiles)
    def _():
        run(k_ref[0, :ctx_len, :], v_ref[0, :ctx_len, :])

    @pl.when(qi >= ctx_tiles)
    def _():
        run(k_ref[0], v_ref[0])


def _attention(q, k, v, ctx_len):
    nb, t, qw = q.shape
    hkv = k.shape[-1] // AT_HEAD
    group = qw // AT_HEAD // hkv
    tq = AT_TQ
    assert ctx_len % tq == 0 and t % tq == 0
    kern = functools.partial(_attn_kernel, ctx_tiles=ctx_len // tq, ctx_len=ctx_len, group=group)
    return pl.pallas_call(
        kern,
        grid=(nb, hkv, t // tq),
        in_specs=[pl.BlockSpec((1, tq, group * AT_HEAD), lambda b, h, i: (b, i, h)),
                  pl.BlockSpec((1, t, AT_HEAD), lambda b, h, i: (b, 0, h)),
                  pl.BlockSpec((1, t, AT_HEAD), lambda b, h, i: (b, 0, h))],
        out_specs=pl.BlockSpec((1, tq, group * AT_HEAD), lambda b, h, i: (b, i, h)),
        out_shape=jax.ShapeDtypeStruct((nb, t, qw), F32),
        compiler_params=_cparams(("parallel", "parallel", "arbitrary")),
        name="gqa",
    )(q, k, v)


def _rope_tables(t, ctx_len):
    n_lat = t - ctx_len
    pos = jnp.arange(n_lat)
    row = (pos // GRID_W).astype(F32)
    col = (pos % GRID_W).astype(F32)
    axis_dim = AT_HEAD // 2
    inv_freq = ROPE_THETA ** (-jnp.arange(0, axis_dim, 2, dtype=F32) / axis_dim)
    ang_r = row[:, None] * inv_freq[None, :]
    ang_c = col[:, None] * inv_freq[None, :]
    cos = jnp.concatenate([jnp.cos(ang_r)] * 2 + [jnp.cos(ang_c)] * 2, axis=-1)
    sin = jnp.concatenate([-jnp.sin(ang_r), jnp.sin(ang_r), -jnp.sin(ang_c), jnp.sin(ang_c)], axis=-1)
    cos = jnp.concatenate([jnp.ones((ctx_len, AT_HEAD), F32), cos], axis=0)
    sin = jnp.concatenate([jnp.zeros((ctx_len, AT_HEAD), F32), sin], axis=0)
    return cos, sin


PREP_TM = 256
HALO = 8


def _halo_specs(tm, width, col, t):
    per = tm // HALO
    last = t // HALO - 1
    prev = pl.BlockSpec((1, HALO, width), lambda b, i: (b, jnp.maximum(i * per - 1, 0), col))
    nxt = pl.BlockSpec((1, HALO, width), lambda b, i: (b, jnp.minimum((i + 1) * per, last), col))
    return prev, nxt


def _neighbours(x, prev_ref, next_ref, start, ctx_len, t):
    tm = x.shape[0]
    row = lax.broadcasted_iota(jnp.int32, x.shape, 0)
    has_prev = jnp.logical_and(start != 0, start != ctx_len)
    has_next = jnp.logical_and(start + tm != ctx_len, start + tm != t)
    prev_row = jnp.where(has_prev, prev_ref[0, HALO - 1:HALO, :], 0.0)
    next_row = jnp.where(has_next, next_ref[0, 0:1, :], 0.0)
    x_prev = jnp.where(row == 0, prev_row, pltpu.roll(x, 1, 0))
    x_next = jnp.where(row == tm - 1, next_row, pltpu.roll(x, tm - 1, 0))
    return x_prev, x_next


def _rwkv_prep_kernel(x_ref, xp_ref, xn_ref, mu_ref, wa_ref, gup_ref, w0_ref, a0_ref, kk_ref, ka_ref, rk_ref, hs_ref,
                      r_o, v_o, kkn_o, lwf_o, bf_o, ktf_o, lwb_o, bb_o, ktb_o, g_o, bonus_o, *, ctx_len, t, width):
    start = pl.program_id(1) * x_ref.shape[1]
    x = x_ref[0]
    x_prev, x_next = _neighbours(x, xp_ref, xn_ref, start, ctx_len, t)
    pp = x + mu_ref[0:1, :] * (x_prev - x) + mu_ref[1:2, :] * (x_next - x)
    w = width
    r, k, v = pp[:, :w], pp[:, w:2 * w], pp[:, 2 * w:3 * w]
    low = pp[:, 3 * w:3 * w + LANES]
    lane = lax.broadcasted_iota(jnp.int32, low.shape, 1)
    low = jnp.where(lane < RW_DECAY_RANK, jnp.tanh(low), low)
    g_o[0] = _mm(jax.nn.sigmoid(pp[:, 3 * w + LANES:]), gup_ref[...])
    hs = hs_ref[...]
    kx = k * kk_ref[...]
    kk = kx * lax.rsqrt(_mm(kx * kx, hs, "nn", 3, 1) + 1e-6)
    r_o[0], v_o[0], kkn_o[0] = r, v, kk
    kt_sum = None
    for d, (lw_o, b_o, kt_o) in enumerate(((lwf_o, bf_o, ktf_o), (lwb_o, bb_o, ktb_o))):
        up = _mm(low, wa_ref[d])
        lw_o[0] = -RW_DECAY_SCALE * jax.nn.sigmoid(w0_ref[d:d + 1, :] + up[:, :w])
        a = jax.nn.sigmoid(a0_ref[d:d + 1, :] + up[:, w:])
        kt = k * (1.0 + (a - 1.0) * ka_ref[...])
        b_o[0] = kk * a
        kt_o[0] = kt
        kt_sum = kt if kt_sum is None else kt_sum + kt
    bonus_o[0] = _mm(r * kt_sum * rk_ref[...], hs, "nn", 3, 1) * v


def _rwkv_prep(proj, col, p, hs, ctx_len):
    nb, t, _ = proj.shape
    width = p['rw_k_k'].shape[0]
    cols = p['rw_mu'].shape[-1]
    tm = PREP_TM
    assert ctx_len % tm == 0 and t % tm == 0 and cols == 3 * width + LANES + RW_GATE_RANK
    assert RW_DECAY_RANK + RW_ICLR_RANK == LANES
    rank = RW_DECAY_RANK
    wa = jnp.zeros((2, LANES, 2 * width), F32)
    wa = wa.at[:, :rank, :width].set(p['rw_w_up']).at[:, rank:, width:].set(p['rw_a_up'])
    params = [p['rw_mu'], wa, p['rw_g_up'], p['rw_w0'], p['rw_a0'], p['rw_k_k'].reshape(1, width),
              p['rw_k_a'].reshape(1, width), p['rw_r_k'].reshape(1, width), hs]
    full = lambda a: pl.BlockSpec(a.shape, lambda b, i: (0,) * a.ndim)
    prev, nxt = _halo_specs(tm, cols, col, t)
    out_spec = pl.BlockSpec((1, tm, width), lambda b, i: (b, i, 0))
    return pl.pallas_call(
        functools.partial(_rwkv_prep_kernel, ctx_len=ctx_len, t=t, width=width),
        grid=(nb, t // tm),
        in_specs=[pl.BlockSpec((1, tm, cols), lambda b, i: (b, i, col)), prev, nxt] + [full(a) for a in params],
        out_specs=[out_spec] * 11,
        out_shape=[jax.ShapeDtypeStruct((nb, t, width), F32)] * 11,
        compiler_params=_cparams(("parallel", "parallel")),
        name="rwkv_prep",
    )(proj, proj, proj, *params)


def _gdn_prep_kernel(*refs, ctx_len, t, nh):
    ins, conv_ref, outs = refs[:9], refs[9], refs[10:]
    start = pl.program_id(1) * ins[0].shape[1]
    width = nh * GD_HEAD
    for part in range(3):
        x_ref, xp_ref, xn_ref = ins[3 * part:3 * part + 3]
        x = x_ref[0]
        x_prev, x_next = _neighbours(x, xp_ref, xn_ref, start, ctx_len, t)
        w = conv_ref[:, part * width:(part + 1) * width]
        y = x_prev * w[0:1, :] + x * w[1:2, :] + x_next * w[2:3, :]
        y = y * jax.nn.sigmoid(y)
        if part == 2:
            outs[part][0] = y
            continue
        scale = GD_HEAD ** -0.5 if part == 0 else 1.0
        for h in range(nh):
            seg = y[:, h * GD_HEAD:(h + 1) * GD_HEAD]
            n = seg * lax.rsqrt(jnp.sum(seg * seg, axis=-1, keepdims=True) + 1e-6)
            outs[part][0, :, h * GD_HEAD:(h + 1) * GD_HEAD] = n * scale


def _gdn_prep(proj, col0, gd_conv, nh, ctx_len):
    nb, t, _ = proj.shape
    width = nh * GD_HEAD
    tm = PREP_TM
    assert gd_conv.shape[0] == 3 and ctx_len % tm == 0 and t % tm == 0
    in_specs, args = [], []
    for part in range(3):
        prev, nxt = _halo_specs(tm, width, col0 + part, t)
        in_specs += [pl.BlockSpec((1, tm, width), lambda b, i, c=col0 + part: (b, i, c)), prev, nxt]
        args += [proj, proj, proj]
    out_spec = pl.BlockSpec((1, tm, width), lambda b, i: (b, i, 0))
    return pl.pallas_call(
        functools.partial(_gdn_prep_kernel, ctx_len=ctx_len, t=t, nh=nh),
        grid=(nb, t // tm),
        in_specs=in_specs + [pl.BlockSpec(gd_conv.shape, lambda b, i: (0, 0))],
        out_specs=[out_spec] * 3,
        out_shape=[jax.ShapeDtypeStruct((nb, t, width), F32)] * 3,
        compiler_params=_cparams(("parallel", "parallel")),
        name="gdn_prep",
    )(*args, gd_conv)


def _attn_prep_kernel(q_ref, k_ref, v_ref, cos_ref, sin_ref, qg_ref, kg_ref, q_o, k_o, v_o):
    cos, sin = cos_ref[...], sin_ref[...]
    lane = lax.broadcasted_iota(jnp.int32, cos.shape, 1)
    low_half = (lane & (AT_HEAD // 2 - 1)) < AT_HEAD // 4

    def prep(seg, gain):
        n = seg * lax.rsqrt(jnp.mean(seg * seg, axis=-1, keepdims=True) + NORM_EPS) * gain
        partner = jnp.where(low_half, pltpu.roll(n, AT_HEAD - AT_HEAD // 4, 1), pltpu.roll(n, AT_HEAD // 4, 1))
        return n * cos + partner * sin

    for h in range(q_ref.shape[2] // AT_HEAD):
        sl = slice(h * AT_HEAD, (h + 1) * AT_HEAD)
        q_o[0, :, sl] = (prep(q_ref[0, :, sl], qg_ref[...]) * AT_HEAD ** -0.5).astype(BF16)
    for h in range(k_ref.shape[2] // AT_HEAD):
        sl = slice(h * AT_HEAD, (h + 1) * AT_HEAD)
        k_o[0, :, sl] = prep(k_ref[0, :, sl], kg_ref[...]).astype(BF16)
    v_o[0] = v_ref[0].astype(BF16)


def _attn_prep(proj, col_q, qw, kvw, at_q_norm, at_k_norm, rope):
    nb, t, _ = proj.shape
    tm = PREP_TM
    cos, sin = rope
    col_k = col_q * qw // kvw + qw // kvw
    blk = lambda w, c: pl.BlockSpec((1, tm, w), lambda b, i: (b, i, c))
    tab = pl.BlockSpec((tm, AT_HEAD), lambda b, i: (i, 0))
    vec = pl.BlockSpec((1, AT_HEAD), lambda b, i: (0, 0))
    out = lambda w: pl.BlockSpec((1, tm, w), lambda b, i: (b, i, 0))
    return pl.pallas_call(
        _attn_prep_kernel,
        grid=(nb, t // tm),
        in_specs=[blk(qw, col_q), blk(kvw, col_k), blk(kvw, col_k + 1), tab, tab, vec, vec],
        out_specs=[out(qw), out(kvw), out(kvw)],
        out_shape=[jax.ShapeDtypeStruct((nb, t, qw), BF16), jax.ShapeDtypeStruct((nb, t, kvw), BF16),
                   jax.ShapeDtypeStruct((nb, t, kvw), BF16)],
        compiler_params=_cparams(("parallel", "parallel")),
        name="attn_prep",
    )(proj, proj, proj, cos, sin, at_q_norm.reshape(1, AT_HEAD), at_k_norm.reshape(1, AT_HEAD))


def _mix_post_kernel(ryf, ryb, rbonus, rg, myf, myb, mo, gyf, gyb, gg, ay, lnw, lnb, mlg, gdg, hs_ref, o_ref, *, gw):
    y = ryf[0] + ryb[0]
    hs = hs_ref[...]
    inv_n = 1.0 / RW_HEAD
    mean = _mm(y, hs, "nn", 3, 1) * inv_n
    cen = y - mean
    var = _mm(cen * cen, hs, "nn", 3, 1) * inv_n
    o_ref[0, :, 0:gw] = ((cen * lax.rsqrt(var + RW_GN_EPS) * lnw[...] + lnb[...] + rbonus[0]) * rg[0]).astype(BF16)
    y = myf[0] + myb[0]
    for h in range(gw // ML_HEAD):
        sl = slice(h * ML_HEAD, (h + 1) * ML_HEAD)
        seg = y[:, sl]
        cen = seg - jnp.mean(seg, axis=-1, keepdims=True)
        var = jnp.mean(cen * cen, axis=-1, keepdims=True)
        out = cen * lax.rsqrt(var + NORM_EPS) * mlg[:, sl] * jax.nn.sigmoid(mo[0, :, sl])
        o_ref[0, :, gw + h * ML_HEAD:gw + (h + 1) * ML_HEAD] = out.astype(BF16)
    y = gyf[0] + gyb[0]
    for h in range(gw // GD_HEAD):
        sl = slice(h * GD_HEAD, (h + 1) * GD_HEAD)
        seg = y[:, sl]
        gate = gg[0, :, sl]
        out = seg * lax.rsqrt(jnp.mean(seg * seg, axis=-1, keepdims=True) + NORM_EPS) * gdg[...] * (gate * jax.nn.sigmoid(gate))
        o_ref[0, :, 2 * gw + h * GD_HEAD:2 * gw + (h + 1) * GD_HEAD] = out.astype(BF16)
    o_ref[0, :, 3 * gw:4 * gw] = ay[0].astype(BF16)


def _mix_post(rw, ml, gd, at, proj, col_o, col_g, params, hs):
    nb, t, gw = at.shape
    tm = PREP_TM
    row = lambda c=0: pl.BlockSpec((1, tm, gw), lambda b, i: (b, i, c))
    full = lambda a: pl.BlockSpec(a.shape, lambda b, i: (0,) * a.ndim)
    ln_w, ln_b, ml_g, gd_g = params
    consts = [ln_w.reshape(1, gw), ln_b.reshape(1, gw), ml_g.reshape(1, gw), gd_g.reshape(1, GD_HEAD), hs]
    return pl.pallas_call(
        functools.partial(_mix_post_kernel, gw=gw),
        grid=(nb, t // tm),
        in_specs=[row()] * 4 + [row(), row(), row(col_o)] + [row(), row(), row(col_g)] + [row()]
                 + [full(a) for a in consts],
        out_specs=pl.BlockSpec((1, tm, 4 * gw), lambda b, i: (b, i, 0)),
        out_shape=jax.ShapeDtypeStruct((nb, t, 4 * gw), BF16),
        compiler_params=_cparams(("parallel", "parallel")),
        name="mix_post",
    )(*rw, ml[0], ml[1], proj, gd[0], gd[1], proj, at, *consts)


MOD_TN = 1024
MOD_KC = 256


def _mod_kernel(s_ref, w_ref, b_ref, o_ref, *, ncond):
    d = w_ref.shape[1]
    acc = [jnp.zeros((1, w_ref.shape[2]), F32) for _ in range(ncond)]
    for kc in range(d // MOD_KC):
        rows = slice(kc * MOD_KC, (kc + 1) * MOD_KC)
        w = w_ref[0, rows, :]
        for j in range(ncond):
            acc[j] = acc[j] + jnp.sum(w * s_ref[rows, j:j + 1], axis=0, keepdims=True)
    bias = b_ref[0]
    pad = [jnp.zeros_like(bias)] * (o_ref.shape[1] - ncond)
    o_ref[0] = jnp.concatenate([a + bias for a in acc] + pad, axis=0)


def _modulation(conds, ada_w, ada_b):
    depth, d, n = ada_w.shape
    ncond = conds.shape[0]
    s_cols = jnp.pad(jax.nn.silu(conds).T, ((0, 0), (0, 8 - ncond)))
    return pl.pallas_call(
        functools.partial(_mod_kernel, ncond=ncond),
        grid=(depth, n // MOD_TN),
        in_specs=[pl.BlockSpec((d, 8), lambda l, j: (0, 0)),
                  pl.BlockSpec((1, d, MOD_TN), lambda l, j: (l, 0, j)),
                  pl.BlockSpec((1, 1, MOD_TN), lambda l, j: (l, 0, j))],
        out_specs=pl.BlockSpec((1, 8, MOD_TN), lambda l, j: (l, 0, j)),
        out_shape=jax.ShapeDtypeStruct((depth, 8, n), F32),
        compiler_params=_cparams(("parallel", "parallel")),
        name="adaln_modulation",
    )(s_cols, ada_w, ada_b.reshape(depth, 1, n))


PROJ_TM = 544
PROJ_TN = 1792


def _modulated_norm(x, gain, mod_ref, shift_row, row0, ctx_len):
    y = x * lax.rsqrt(jnp.mean(x * x, axis=-1, keepdims=True) + NORM_EPS) * gain
    rows = row0 + lax.broadcasted_iota(jnp.int32, (x.shape[0], 1), 0)
    is_ctx = rows < ctx_len
    shift = jnp.where(is_ctx, mod_ref[0, 0, shift_row:shift_row + 1, :], mod_ref[0, 1, shift_row:shift_row + 1, :])
    scale = jnp.where(is_ctx, mod_ref[0, 0, shift_row + 1:shift_row + 2, :],
                      mod_ref[0, 1, shift_row + 1:shift_row + 2, :])
    return y * (1.0 + scale) + shift


def _norm_proj_kernel(x_ref, g_ref, mod_ref, w_ref, o_ref, h_scr, *, tiles_per_seq, ctx_len):
    row0 = (pl.program_id(0) % tiles_per_seq) * x_ref.shape[0]

    @pl.when(pl.program_id(1) == 0)
    def _():
        h_scr[...] = _modulated_norm(x_ref[...], g_ref[...], mod_ref, 0, row0, ctx_len).astype(BF16)

    o_ref[...] = jnp.dot(h_scr[...], w_ref[...], preferred_element_type=F32)


def _norm_proj(x, gain, mod, w, ctx_len):
    nb, t, d = x.shape
    n = w.shape[1]
    tm, tn = _pick_tile(t, PROJ_TM), PROJ_TN
    assert n % tn == 0
    tps = t // tm
    out = pl.pallas_call(
        functools.partial(_norm_proj_kernel, tiles_per_seq=tps, ctx_len=ctx_len),
        grid=(nb * tps, n // tn),
        in_specs=[pl.BlockSpec((tm, d), lambda i, j: (i, 0)),
                  pl.BlockSpec((1, d), lambda i, j: (0, 0)),
                  pl.BlockSpec((1, 2, 8, d), lambda i, j: (i // tps, 0, 0, 0)),
                  pl.BlockSpec((d, tn), lambda i, j: (0, j))],
        out_specs=pl.BlockSpec((tm, tn), lambda i, j: (i, j)),
        out_shape=jax.ShapeDtypeStruct((nb * t, n), F32),
        scratch_shapes=[pltpu.VMEM((tm, d), BF16)],
        compiler_params=_cparams(("parallel", "arbitrary")),
        name="norm_in_proj",
    )(x.reshape(nb * t, d), gain.reshape(1, d), mod, w)
    return out.reshape(nb, t, n)


OUT_TM = 272


def _out_proj_kernel(y_ref, w_ref, x_ref, g_ref, mod_ref, wr_hi, wr_lo,
                     xo_ref, h_ref, aff_ref, *, tiles_per_seq, ctx_len, n_experts):
    acc = jnp.dot(y_ref[...], w_ref[...], preferred_element_type=F32)
    tm = x_ref.shape[0]
    row0 = (pl.program_id(0) % tiles_per_seq) * tm
    rows = row0 + lax.broadcasted_iota(jnp.int32, (tm, 1), 0)
    gate = jnp.where(rows < ctx_len, mod_ref[0, 0, 2:3, :], mod_ref[0, 1, 2:3, :])
    x_new = x_ref[...] + gate * acc
    xo_ref[...] = x_new
    h = _modulated_norm(x_new, g_ref[...], mod_ref, 3, row0, ctx_len)
    h_ref[...] = h
    h_hi = h.astype(BF16)
    h_lo = (h - h_hi.astype(F32)).astype(BF16)
    logits = (jnp.dot(h_hi, wr_hi[...], preferred_element_type=F32)
              + jnp.dot(h_lo, wr_hi[...], preferred_element_type=F32)
              + jnp.dot(h_hi, wr_lo[...], preferred_element_type=F32))
    lane = lax.broadcasted_iota(jnp.int32, logits.shape, 1)
    logits = jnp.where(lane < n_experts, logits, -jnp.inf)
    p = jnp.exp(logits - jnp.max(logits, axis=-1, keepdims=True))
    aff_ref[...] = p / jnp.sum(p, axis=-1, keepdims=True)


def _out_proj(ymix, w_out, x, gain, mod, w_router, ctx_len):
    nb, t, d = x.shape
    mw = ymix.shape[-1]
    ne = w_router.shape[1]
    tm = _pick_tile(t, OUT_TM)
    tps = t // tm
    wr =jnp.pad(w_router, ((0, 0), (0, LANES - ne)))
    wr_hi = wr.astype(BF16)
    wr_lo = (wr - wr_hi.astype(F32)).astype(BF16)
    row = lambda w: pl.BlockSpec((tm, w), lambda i: (i, 0))
    full = lambda a: pl.BlockSpec(a.shape, lambda i: (0,) * a.ndim)
    xo, h, aff = pl.pallas_call(
        functools.partial(_out_proj_kernel, tiles_per_seq=tps, ctx_len=ctx_len, n_experts=ne),
        grid=(nb * tps,),
        in_specs=[row(mw), full(w_out), row(d), pl.BlockSpec((1, d), lambda i: (0, 0)),
                  pl.BlockSpec((1, 2, 8, d), lambda i: (i // tps, 0, 0, 0)), full(wr_hi), full(wr_lo)],
        out_specs=[row(d), row(d), row(LANES)],
        out_shape=[jax.ShapeDtypeStruct((nb * t, d), F32), jax.ShapeDtypeStruct((nb * t, d), F32),
                   jax.ShapeDtypeStruct((nb * t, LANES), F32)],
        compiler_params=_cparams(("parallel",)),
        name="out_proj_norm_router",
    )(ymix.reshape(nb * t, mw), w_out, x.reshape(nb * t, d), gain.reshape(1, d), mod, wr_hi, wr_lo)
    return xo, h, aff.reshape(nb, t, LANES)


ROUTE_BLK = 256
ROUTE_ROWS = 8


def _route_kernel(aff_ref, *refs, segments, n_experts):
    out_refs, t_scr = refs[:-1], refs[-1]
    for (start, length), o_ref in zip(segments, out_refs):
        cap = EC_FACTOR * length // n_experts
        capp = o_ref.shape[2]
        x = aff_ref[0, start:start + length, :]

        def refine(i, thr):
            cand = thr | jnp.left_shift(jnp.int32(1), 30 - i)
            cnt = jnp.sum(jnp.where(x >= pltpu.bitcast(cand, F32), 1.0, 0.0), axis=0, keepdims=True)
            return jnp.where(cnt >= cap, cand, thr)

        thr = lax.fori_loop(0, 31, refine, jnp.zeros((1, LANES), jnp.int32))
        thr_f = pltpu.bitcast(thr, F32)
        need = cap - jnp.sum(jnp.where(x > thr_f, 1.0, 0.0), axis=0, keepdims=True)
        blk = min(ROUTE_BLK, length)
        before = (lax.broadcasted_iota(jnp.int32, (blk, blk), 1)
                  < lax.broadcasted_iota(jnp.int32, (blk, blk), 0)).astype(BF16)
        lane = lax.broadcasted_iota(jnp.int32, (blk, LANES), 1)
        slot_ids = lax.broadcasted_iota(jnp.int32, (capp, blk), 0).astype(F32)
        src = lax.broadcasted_iota(jnp.int32, (LANES, LANES), 0)
        dst = lax.broadcasted_iota(jnp.int32, (LANES, LANES), 1)
        assert start % blk == 0 and length % blk == 0 and blk == t_scr.shape[2]
        o_ref[0] = jnp.zeros((LANES, capp), F32)

        def block(kb, carry, start=start, blk=blk, thr_f=thr_f, need=need, before=before, lane=lane,
                  slot_ids=slot_ids, src=src, dst=dst, o_ref=o_ref):
            carry_tied, carry_sel = carry
            xb = aff_ref[0, pl.ds(pl.multiple_of(start + kb * blk, blk), blk), :]
            above_b = xb > thr_f
            tied_b = xb == thr_f
            tied_f = jnp.where(tied_b, 1.0, 0.0)
            tied_before = _mm(before, tied_f) + carry_tied
            sel_f = jnp.where(jnp.logical_or(above_b, jnp.logical_and(tied_b, tied_before < need)), 1.0, 0.0)
            slot = _mm(before, sel_f) + carry_sel
            tok = kb * blk + lax.broadcasted_iota(jnp.int32, (blk, LANES), 0)
            vals = jnp.where((lane & 7) == 0, (tok >> 6).astype(F32),
                             jnp.where((lane & 7) == 1, (tok & 63).astype(F32), 0.0))
            for r, piece in enumerate(_split(xb, 3)):
                to_lane = jnp.logical_and(dst == src * ROUTE_ROWS + 2 + r, src < n_experts)
                vals = vals + _mm(piece, to_lane.astype(BF16))
            t_scr[0] = slot.T
            t_scr[1] = sel_f.T
            t_scr[2] = vals.T

            def expert(e, _):
                row_slot = t_scr[0, pl.ds(e, 1), :]
                row_sel = t_scr[1, pl.ds(e, 1), :]
                rows = pl.ds(pl.multiple_of(e * ROUTE_ROWS, ROUTE_ROWS), ROUTE_ROWS)
                hit = jnp.logical_and(row_sel > 0.5, row_slot == slot_ids)
                o_ref[0, rows, :] += lax.dot_general(t_scr[2, rows, :], jnp.where(hit, 1.0, 0.0), _DIMS["nt"],
                                                     preferred_element_type=F32)
                return 0

            lax.fori_loop(0, n_experts, expert, 0, unroll=2)
            return (carry_tied + jnp.sum(tied_f, axis=0, keepdims=True),
                    carry_sel + jnp.sum(sel_f, axis=0, keepdims=True))

        zero = jnp.zeros((1, LANES), F32)
        lax.fori_loop(0, length // blk, block, (zero, zero))


def _route(aff, segments, n_experts):
    nb, t, _ = aff.shape
    caps = [EC_FACTOR * length // n_experts for _, length in segments]
    capps = [-(-c // LANES) * LANES for c in caps]
    outs = pl.pallas_call(
        functools.partial(_route_kernel, segments=tuple(segments), n_experts=n_experts),
        grid=(nb,),
        in_specs=[pl.BlockSpec((1, t, LANES), lambda b: (b, 0, 0))],
        out_specs=[pl.BlockSpec((1, LANES, cp), lambda b: (b, 0, 0)) for cp in capps],
        out_shape=[jax.ShapeDtypeStruct((nb, LANES, cp), F32) for cp in capps],
        scratch_shapes=[pltpu.VMEM((3, LANES, ROUTE_BLK), F32)],
        compiler_params=_cparams(("parallel",)),
        name="expert_choice_route",
    )(aff)
    picks = []
    for o, cap in zip(outs, caps):
        v = o.reshape(nb, LANES // ROUTE_ROWS, ROUTE_ROWS, -1)[:, :n_experts, :, :cap]
        tok = jnp.round(v[:, :, 0] * 64.0 + v[:, :, 1]).astype(jnp.int32)
        picks.append((tok, v[:, :, 2] + v[:, :, 3] + v[:, :, 4]))
    return picks


FFN_TF = 256


def _moe_ffn_kernel(idx_ref, gate_ref, g2_ref, w1_ref, w3_ref, w2_ref, h_hbm, x_hbm, xo_hbm,
                    xg, xs_bf, acc, rows, gsem, rsem, wsem, *, groups, ne, nj):
    del x_hbm
    e, j = pl.program_id(0), pl.program_id(1)
    r_tot = acc.shape[0]
    half = nj // 2
    per, wper = r_tot // nj, r_tot // half
    e_prev, e_next = jnp.maximum(e - 1, 0), jnp.minimum(e + 1, ne - 1)

    def row_copy(src, dst, s, d_, sem):
        return pltpu.make_async_copy(src.at[pl.ds(s, 1), :], dst.at[pl.ds(d_, 1), :], sem)

    def for_rows(n, fn):
        def body(i, c):
            fn(i)
            return c
        lax.fori_loop(0, n, body, 0, unroll=8)

    gather = lambda expert, r: row_copy(h_hbm, xg, idx_ref[expert, r], r, gsem)
    read = lambda expert, r: row_copy(xo_hbm, rows, idx_ref[expert, r], r, rsem)
    write = lambda expert, r: row_copy(rows, xo_hbm, r, idx_ref[expert, r], wsem)

    @pl.when(jnp.logical_and(e == 0, j == 0))
    def _():
        for_rows(r_tot, lambda r: gather(0, r).start())
        for_rows(r_tot, lambda r: read(0, r).start())
        for_rows(r_tot, lambda r: row_copy(h_hbm, xg, 0, r, gsem).wait())
        for_rows(r_tot, lambda r: row_copy(xo_hbm, rows, 0, r, rsem).wait())

    @pl.when(j == 0)
    def _():
        xs_bf[...] = xg[...].astype(BF16)
        acc[...] = jnp.zeros_like(acc)

    @pl.when(j == half)
    def _():
        for_rows(r_tot, lambda r: row_copy(rows, xo_hbm, r, 0, wsem).wait())

    def ffn_step():
        xs = xs_bf[...]
        a = jnp.dot(xs, w1_ref[0, 0].astype(BF16), preferred_element_type=F32)
        b = jnp.dot(xs, w3_ref[0, 0].astype(BF16), preferred_element_type=F32)
        hid = (a * jax.nn.sigmoid(a) * b).astype(BF16)
        acc[...] += jnp.dot(hid, w2_ref[0, 0].astype(BF16), preferred_element_type=F32)

    def first_half():
        for i in range(per):
            gather(e_next, j * per + i).start()
        for i in range(wper):
            write(e_prev, j * wper + i).start()
        ffn_step()

    def second_half():
        for i in range(per):
            gather(e_next, j * per + i).start()
        for i in range(wper):
            read(e, (j - half) * wper + i).start()
        ffn_step()

    lax.cond(j < half, first_half, second_half)

    @pl.when(j == nj - 1)
    def _():
        for_rows(r_tot, lambda r: row_copy(xo_hbm, rows, 0, r, rsem).wait())
        for g, (lo, n) in enumerate(groups):
            rows[lo:lo + n, :] += (gate_ref[0, lo:lo + n, :] * acc[lo:lo + n, :]) * g2_ref[g:g + 1, :]
        for_rows(r_tot, lambda r: row_copy(h_hbm, xg, 0, r, gsem).wait())

        @pl.when(e == ne - 1)
        def _():
            for_rows(r_tot, lambda r: write(e, r).start())
            for_rows(r_tot, lambda r: row_copy(rows, xo_hbm, r, 0, wsem).wait())


def _moe_ffn(x, h, idx, gate, g2, groups, w1, w3, w2, layer):
    n, d = x.shape
    ne, r = idx.shape
    f = w1.shape[3]
    tf = FFN_TF
    nj = f // tf
    assert nj % 2 == 0 and r % nj == 0
    grid_spec = pltpu.PrefetchScalarGridSpec(
        num_scalar_prefetch=1,
        grid=(ne, nj),
        in_specs=[pl.BlockSpec((1, r, 1), lambda e, j, idx: (e, 0, 0)),
                  pl.BlockSpec(g2.shape, lambda e, j, idx: (0, 0)),
                  pl.BlockSpec((1, 1, d, tf), lambda e, j, idx: (layer, e, 0, j)),
                  pl.BlockSpec((1, 1, d, tf), lambda e, j, idx: (layer, e, 0, j)),
                  pl.BlockSpec((1, 1, tf, d), lambda e, j, idx: (layer, e, j, 0)),
                  pl.BlockSpec(memory_space=pl.ANY),
                  pl.BlockSpec(memory_space=pl.ANY)],
        out_specs=pl.BlockSpec(memory_space=pl.ANY),
        scratch_shapes=[pltpu.VMEM((r, d), F32), pltpu.VMEM((r, d), BF16), pltpu.VMEM((r, d), F32),
                        pltpu.VMEM((r, d), F32), pltpu.SemaphoreType.DMA, pltpu.SemaphoreType.DMA,
                        pltpu.SemaphoreType.DMA])
    return pl.pallas_call(
        functools.partial(_moe_ffn_kernel, groups=tuple(groups), ne=ne, nj=nj),
        grid_spec=grid_spec,
        out_shape=jax.ShapeDtypeStruct((n, d), F32),
        input_output_aliases={7: 0},
        compiler_params=_cparams(("arbitrary", "arbitrary")),
        name="moe_ffn",
    )(idx, gate.reshape(ne, r, 1), g2, w1, w3, w2, h, x)


def _final_norm_kernel(x_ref, g_ref, o_ref):
    x = x_ref[0]
    o_ref[0] = x * lax.rsqrt(jnp.mean(x * x, axis=-1, keepdims=True) + NORM_EPS) * g_ref[...]


def _final_norm(xs, gain, ctx_len):
    nb, t, d = xs.shape
    tm = PREP_TM
    assert ctx_len % tm == 0 and t % tm == 0
    skip = ctx_len // tm
    return pl.pallas_call(
        _final_norm_kernel,
        grid=(nb, (t - ctx_len) // tm),
        in_specs=[pl.BlockSpec((1, tm, d), lambda b, i: (b, i + skip, 0)), pl.BlockSpec((1, d), lambda b, i: (0, 0))],
        out_specs=pl.BlockSpec((1, tm, d), lambda b, i: (b, i, 0)),
        out_shape=jax.ShapeDtypeStruct((nb, t - ctx_len, d), F32),
        compiler_params=_cparams(("parallel", "parallel")),
        name="final_norm",
    )(xs, gain.reshape(1, d))


RELAYOUT_TM = 256


def _relayout_kernel(w_ref, o_ref, *, gw, rw_cols, ngate):
    ml0 = rw_cols
    gd0 = ml0 + 4 * gw + ngate
    at0 = gd0 + 4 * gw + ngate
    o_ref[:, 0:rw_cols] = w_ref[:, 0:rw_cols].astype(BF16)
    o_ref[:, rw_cols:rw_cols + ngate] = w_ref[:, ml0 + 4 * gw:gd0].astype(BF16)
    o_ref[:, rw_cols + ngate:rw_cols + 2 * ngate] = w_ref[:, gd0 + 4 * gw:at0].astype(BF16)
    o_ref[:, rw_cols + 2 * ngate:4 * gw] = jnp.zeros((o_ref.shape[0], 4 * gw - rw_cols - 2 * ngate), BF16)
    o_ref[:, 4 * gw:8 * gw] = w_ref[:, ml0:ml0 + 4 * gw].astype(BF16)
    o_ref[:, 8 * gw:12 * gw] = w_ref[:, gd0:gd0 + 4 * gw].astype(BF16)
    o_ref[:, 12 * gw:] = w_ref[:, at0:].astype(BF16)


def _relayout_w_in(w_in, layer, gw, rw_cols, ngate):
    _, d, n = w_in.shape
    assert rw_cols % LANES == 0 and rw_cols + 2 * ngate <= 4 * gw
    n_out = n - rw_cols - 2 * ngate + 4 * gw
    tm = RELAYOUT_TM
    return pl.pallas_call(
        functools.partial(_relayout_kernel, gw=gw, rw_cols=rw_cols, ngate=ngate),
        grid=(d // tm,),
        in_specs=[pl.BlockSpec((None, tm, n), lambda i: (layer, i, 0))],
        out_specs=pl.BlockSpec((tm, n_out), lambda i: (i, 0)),
        out_shape=jax.ShapeDtypeStruct((d, n_out), BF16),
        compiler_params=_cparams(("parallel",)),
        name="w_in_relayout",
    )(w_in)


def kernel(x, c, ctx, c_ctx, ada_w, ada_b, norm1_g, norm2_g, w_in, w_out, rw_mu, rw_w0, rw_w_up, rw_a0, rw_a_up,
           rw_g_up, rw_k_k, rw_k_a, rw_r_k, rw_ln_w, rw_ln_b, ml_ib, ml_fb, ml_norm_g, gd_conv, gd_a_log,
           gd_dt_bias, gd_norm_g, at_q_norm, at_k_norm, w_router, w_exp1, w_exp3, w_exp2, final_g):
    nb, n_lat, d = x.shape
    ctx_len = ctx.shape[1]
    t = ctx_len + n_lat
    depth = ada_w.shape[0]
    gw = d // N_GROUPS
    rw_cols = rw_mu.shape[-1]
    ngate = 4 * ml_ib.shape[-1]
    at_cols = w_in.shape[-1] - rw_cols - 2 * (4 * gw + ngate)

    xs = jnp.concatenate([ctx, x], axis=1)
    mods = _modulation(jnp.concatenate([c, c_ctx[None]], axis=0), ada_w, ada_b)
    rope = _rope_tables(t, ctx_len)
    nh = ml_ib.shape[-1]
    assert 4 * gw == 2048 and gw == nh * ML_HEAD == nh * GD_HEAD and at_cols == 2 * gw
    head_of = jnp.arange(gw) // RW_HEAD
    hs = (head_of[:, None] == head_of[None, :]).astype(BF16)

    for layer in range(depth):
        last = layer == depth - 1
        m = mods[layer, :nb + 1].reshape(nb + 1, 6, d)
        m = jnp.pad(m, ((0, 0), (0, 2), (0, 0)))
        mod = jnp.stack([jnp.broadcast_to(m[nb], (nb, 8, d)), m[:nb]], axis=1)
        proj = _norm_proj(xs, norm1_g[layer], mod, _relayout_w_in(w_in, layer, gw, rw_cols, ngate), ctx_len)
        p_rw = {k: v[layer] for k, v in dict(
            rw_mu=rw_mu, rw_w0=rw_w0, rw_w_up=rw_w_up, rw_a0=rw_a0, rw_a_up=rw_a_up, rw_g_up=rw_g_up,
            rw_k_k=rw_k_k, rw_k_a=rw_k_a, rw_r_k=rw_r_k).items()}
        graw = proj[..., rw_cols:rw_cols + 2 * ngate]
        r, v, kk, lwf, bf, ktf, lwb, bb, ktb, g_rw, bonus = _rwkv_prep(proj, 0, p_rw, hs, ctx_len)
        ryf, ryb = _rwkv_scan(r, v, kk, [(lwf, bf, ktf), (lwb, bb, ktb)], ctx_len)

        gm = graw[..., :ngate]
        li = gm[..., :2 * nh] + ml_ib[layer].reshape(-1)
        lf = jax.nn.log_sigmoid(gm[..., 2 * nh:] + ml_fb[layer].reshape(-1))
        g_ml = jnp.concatenate([li, lf], axis=-1)
        myf, myb = _mlstm_scan(proj, proj, proj, (4, 5, 6), g_ml, jnp.swapaxes(g_ml, 1, 2), ctx_len)

        gg = graw[..., ngate:]
        lg = -jnp.exp(gd_a_log[layer]).reshape(-1) * jax.nn.softplus(gg[..., :2 * nh] + gd_dt_bias[layer].reshape(-1))
        q_gd, k_gd, v_gd = _gdn_prep(proj, 8, gd_conv[layer], nh, ctx_len)
        gyf, gyb = _gdn_scan(q_gd, k_gd, v_gd, jnp.concatenate([lg, jax.nn.sigmoid(gg[..., 2 * nh:])], axis=-1), ctx_len)

        kvw = AT_KV_HEADS * AT_HEAD
        q_at, k_at, v_at = _attn_prep(proj, 12, at_cols - 2 * kvw, kvw, at_q_norm[layer], at_k_norm[layer], rope)
        ay = _attention(q_at, k_at, v_at, ctx_len)

        ymix = _mix_post((ryf, ryb, bonus, g_rw), (myf, myb), (gyf, gyb), ay, proj, 7, 11,
                         (rw_ln_w[layer], rw_ln_b[layer], ml_norm_g[layer], gd_norm_g[layer]), hs)
        xs_flat, h2, aff = _out_proj(ymix, w_out[layer].astype(BF16), xs, norm2_g[layer], mod, w_router[layer],
                                     ctx_len)
        segments = [(ctx_len, n_lat)] if last else [(0, ctx_len), (ctx_len, n_lat)]
        ne = w_router.shape[-1]
        idx_parts, gate_parts, groups, g2_rows, off = [], [], [], [], 0
        for (start, length), (tok, gate) in zip(segments, _route(aff, segments, ne)):
            for b in range(nb):
                idx_parts.append(tok[b] + (b * t + start))
                gate_parts.append(gate[b])
                groups.append((off, tok.shape[-1]))
                off += tok.shape[-1]
                g2_rows.append(mod[b, 0 if start < ctx_len else 1, 5])
        xs = _moe_ffn(xs_flat, h2, jnp.concatenate(idx_parts, axis=1), jnp.concatenate(gate_parts, axis=1),
                      jnp.stack(g2_rows), groups, w_exp1, w_exp3, w_exp2, layer).reshape(nb, t, d)

    return _final_norm(xs, final_g, ctx_len)
```
